```python
import math
import jax, jax.numpy as jnp
from jax import lax
import numpy as np

D_MODEL = 1024
BATCH = 8
SEQ = 8192
DEPTH = 2
DEC_BATCH = 16
DEC_SEQ = 4096
PAST_LEN = 128

HEAD_DIM = 64
GROUP_HEADS = 6
BRANCH_W = GROUP_HEADS * HEAD_DIM
N_MIXERS = 4
MIX_W = N_MIXERS * BRANCH_W
N_PROJ_PIECES = 15
PROJ_W = N_PROJ_PIECES * BRANCH_W
CHUNK = 128
DIL_PATTERNS = ((128, 1), (512, 4), (2048, 16))
LOCAL_BLOCK = 64
DIFF_QK_DIM = HEAD_DIM // 2
CONV_WIDTH = 3
Q_BLOCK = 128
EPS = 1e-6
NEG = -1e30

kernel_name = 'hymba_style_hybrid_encoder'


def rms_norm(x, g):
    xf = x.astype(jnp.float32)
    y = xf * lax.rsqrt(jnp.mean(xf * xf, axis=-1, keepdims=True) + EPS)
    return (y * g.astype(jnp.float32)).astype(x.dtype)


def alibi_slopes(n):
    return 2.0 ** (-8.0 * jnp.arange(1, n + 1, dtype=jnp.float32) / n)


def spatial_gating(u, v, norm_g, w_s, b_s):
    bsz, s, _ = u.shape
    v = rms_norm(v, norm_g)
    vc = v.reshape(bsz, s // CHUNK, CHUNK, GROUP_HEADS, HEAD_DIM)
    mixed = jnp.einsum('gts,bcsge->bctge', w_s.astype(jnp.float32), vc.astype(jnp.float32))
    mixed = mixed + b_s.astype(jnp.float32).T[None, None, :, :, None]
    return (u.astype(jnp.float32) * mixed.reshape(bsz, s, BRANCH_W)).astype(u.dtype)


def dilated_attention(q, k, v, slopes, window, dilation):
    bsz, s, h, e = q.shape
    n_side = window // (2 * dilation)
    L = s // dilation
    nb = -(-L // LOCAL_BLOCK)
    Lp = nb * LOCAL_BLOCK

    def to_sub(x):
        return x.reshape(bsz, L, dilation, h, e).transpose(0, 2, 1, 3, 4)

    qs = jnp.pad(to_sub(q), ((0, 0), (0, 0), (0, Lp - L), (0, 0), (0, 0)))
    pad_kv = ((0, 0), (0, 0), (LOCAL_BLOCK, Lp - L + LOCAL_BLOCK), (0, 0), (0, 0))
    ks = jnp.pad(to_sub(k), pad_kv)
    vs = jnp.pad(to_sub(v), pad_kv)

    def band(x):
        xb = x.reshape(bsz, dilation, nb + 2, LOCAL_BLOCK, h, e)
        return jnp.concatenate([xb[:, :, :-2], xb[:, :, 1:-1], xb[:, :, 2:]], axis=3)

    kb = band(ks).astype(jnp.float32)
    vb = band(vs).astype(jnp.float32)
    qb = qs.reshape(bsz, dilation, nb, LOCAL_BLOCK, h, e).astype(jnp.float32)
    scores = jnp.einsum('brnqhe,brnkhe->brhnqk', qb, kb) / math.sqrt(e)
    q_idx = jnp.arange(Lp).reshape(nb, LOCAL_BLOCK)
    k_idx = jnp.arange(nb)[:, None] * LOCAL_BLOCK - LOCAL_BLOCK + jnp.arange(3 * LOCAL_BLOCK)[None, :]
    rel = k_idx[:, None, :] - q_idx[:, :, None]
    valid = (jnp.abs(rel) <= n_side) & (k_idx[:, None, :] >= 0) & (k_idx[:, None, :] < L)
    bias = -slopes[:, None, None, None] * (jnp.abs(rel) * dilation).astype(jnp.float32)
    scores = jnp.where(valid, scores + bias, NEG)
    m = jnp.max(scores, axis=-1, keepdims=True)
    p = jnp.exp(scores - m)
    den = jnp.sum(p, axis=-1)
    o = jnp.einsum('brhnqk,brnkhe->brnqhe', p, vb) / den.transpose(0, 1, 3, 4, 2)[..., None]
    lse = (m[..., 0] + jnp.log(den)).transpose(0, 1, 3, 4, 2)
    o = o.reshape(bsz, dilation, Lp, h, e)[:, :, :L].transpose(0, 2, 1, 3, 4).reshape(bsz, s, h, e)
    lse = lse.reshape(bsz, dilation, Lp, h)[:, :, :L].transpose(0, 2, 1, 3).reshape(bsz, s, h)
    return o, lse


def dilated_mixture(q, k, v, slopes):
    res = [dilated_attention(q, k, v, slopes, w, d) for (w, d) in DIL_PATTERNS]
    outs = jnp.stack([r[0] for r in res], axis=0)
    lses = jnp.stack([r[1] for r in res], axis=0)
    wts = jax.nn.softmax(lses, axis=0)
    return jnp.sum(outs * wts[..., None], axis=0)


def diff_attention(q, k, v, slopes, lam, lambda_init, subln_g):
    bsz, s, h, _, dq = q.shape
    nqb = s // Q_BLOCK
    qb = q.reshape(bsz, nqb, Q_BLOCK, h, 2, dq).transpose(1, 0, 2, 3, 4, 5)
    kf = k.astype(jnp.float32)
    vf = v.astype(jnp.float32)
    key_pos = jnp.arange(s)

    def block(args):
        qblk, start = args
        sc = jnp.einsum('bqhcd,bkhcd->bchqk', qblk.astype(jnp.float32), kf) / math.sqrt(dq)
        dist = jnp.abs(start + jnp.arange(Q_BLOCK)[:, None] - key_pos[None, :]).astype(jnp.float32)
        sc = sc - slopes[:, None, None] * dist
        p = jax.nn.softmax(sc, axis=-1)
        attn = p[:, 0] - lam * p[:, 1]
        return jnp.einsum('bhqk,bkhe->bqhe', attn, vf)

    o = lax.map(block, (qb, jnp.arange(nqb) * Q_BLOCK))
    o = o.transpose(1, 0, 2, 3, 4).reshape(bsz, s, h, v.shape[-1])
    return rms_norm(o, subln_g) * (1.0 - lambda_init)


def short_conv(x_in, b_g, c_g, conv_w):
    z = c_g * x_in
    zp = jnp.pad(z, ((0, 0), (1, 1), (0, 0)))
    y = conv_w[0] * zp[:, :-2] + conv_w[1] * zp[:, 1:-1] + conv_w[2] * zp[:, 2:]
    return b_g * y


def layer(x, l, norm_g, w_in, sgu_g, w_s, b_s, qn_b, kn_b, qn_c, kn_c,
          lam_q1, lam_k1, lam_q2, lam_k2, subln_g, conv_w, w_out):
    bsz, s, _ = x.shape
    slopes = alibi_slopes(GROUP_HEADS)
    h = rms_norm(x, norm_g)
    proj = h @ w_in
    (a_u, a_v, a_g, b_q, b_k, b_v, b_g, c_q, c_k, c_v, c_g,
     d_in, d_b, d_c, d_g) = jnp.split(proj, N_PROJ_PIECES, axis=-1)

    def heads(t):
        return t.reshape(bsz, s, GROUP_HEADS, HEAD_DIM)

    out_a = spatial_gating(a_u, a_v, sgu_g, w_s, b_s)
    qb = rms_norm(heads(b_q), qn_b)
    kb = rms_norm(heads(b_k), kn_b)
    out_b = dilated_mixture(qb, kb, heads(b_v), slopes).reshape(bsz, s, BRANCH_W).astype(x.dtype)
    lambda_init = 0.8 - 0.6 * math.exp(-0.3 * l)
    lam = (jnp.exp(jnp.sum(lam_q1.astype(jnp.float32) * lam_k1.astype(jnp.float32)))
           - jnp.exp(jnp.sum(lam_q2.astype(jnp.float32) * lam_k2.astype(jnp.float32))) + lambda_init)
    qc = rms_norm(c_q.reshape(bsz, s, GROUP_HEADS, 2, DIFF_QK_DIM), qn_c)
    kc = rms_norm(c_k.reshape(bsz, s, GROUP_HEADS, 2, DIFF_QK_DIM), kn_c)
    out_c = diff_attention(qc, kc, heads(c_v), slopes, lam, lambda_init, subln_g)
    out_c = out_c.reshape(bsz, s, BRANCH_W).astype(x.dtype)
    out_d = short_conv(d_in, d_b, d_c, conv_w)
    mixed = jnp.concatenate([jax.nn.silu(a_g) * out_a, jax.nn.silu(b_g) * out_b,
                             jax.nn.silu(c_g) * out_c, jax.nn.silu(d_g) * out_d], axis=-1).astype(x.dtype)
    return x + mixed @ w_out


def trunk(x, norm_g, w_in, sgu_g, w_s, b_s, qn_b, kn_b, qn_c, kn_c,
          lam_q1, lam_k1, lam_q2, lam_k2, subln_g, conv_w, w_out):
    for l in range(DEPTH):
        x = layer(x, l, norm_g[l], w_in[l], sgu_g[l], w_s[l], b_s[l], qn_b[l], kn_b[l],
                  qn_c[l], kn_c[l], lam_q1[l], lam_k1[l], lam_q2[l], lam_k2[l],
                  subln_g[l], conv_w[l], w_out[l])
    return x


def setup_inputs(seed: int = 0) -> dict:
    key = jax.random.key(seed)
    ks = jax.random.split(key, 20)
    f32 = jnp.float32

    def nrm(k, shape, scale):
        return jax.random.normal(k, shape, f32) * scale

    def gain(k, shape):
        return 1.0 + 0.02 * jax.random.normal(k, shape, f32)

    return {
        'x_prompt': nrm(ks[0], (BATCH, SEQ, D_MODEL), 1.0),
        'x_sample': nrm(ks[1], (DEC_BATCH, DEC_SEQ, D_MODEL), 1.0),
        'norm_g': gain(ks[2], (DEPTH, D_MODEL)),
        'w_in': nrm(ks[3], (DEPTH, D_MODEL, PROJ_W), D_MODEL ** -0.5),
        'sgu_g': gain(ks[4], (DEPTH, BRANCH_W)),
        'w_s': nrm(ks[5], (DEPTH, GROUP_HEADS, CHUNK, CHUNK), 0.5 * CHUNK ** -0.5),
        'b_s': 1.0 + 0.1 * jax.random.normal(ks[6], (DEPTH, GROUP_HEADS, CHUNK), f32),
        'qn_b': gain(ks[7], (DEPTH, HEAD_DIM)),
        'kn_b': gain(ks[8], (DEPTH, HEAD_DIM)),
        'qn_c': gain(ks[9], (DEPTH, DIFF_QK_DIM)),
        'kn_c': gain(ks[10], (DEPTH, DIFF_QK_DIM)),
        'lam_q1': nrm(ks[11], (DEPTH, DIFF_QK_DIM), 0.1),
        'lam_k1': nrm(ks[12], (DEPTH, DIFF_QK_DIM), 0.1),
        'lam_q2': nrm(ks[13], (DEPTH, DIFF_QK_DIM), 0.1),
        'lam_k2': nrm(ks[14], (DEPTH, DIFF_QK_DIM), 0.1),
        'subln_g': gain(ks[15], (DEPTH, HEAD_DIM)),
        'conv_w': nrm(ks[16], (DEPTH, CONV_WIDTH, BRANCH_W), CONV_WIDTH ** -0.5),
        'w_out': nrm(ks[17], (DEPTH, MIX_W, D_MODEL), MIX_W ** -0.5),
    }


def reference(x_prompt, x_sample, norm_g, w_in, sgu_g, w_s, b_s, qn_b, kn_b, qn_c, kn_c,
              lam_q1, lam_k1, lam_q2, lam_k2, subln_g, conv_w, w_out):
    y_prompt = trunk(x_prompt, norm_g, w_in, sgu_g, w_s, b_s, qn_b, kn_b, qn_c, kn_c,
                     lam_q1, lam_k1, lam_q2, lam_k2, subln_g, conv_w, w_out)
    y_sample = trunk(x_sample, norm_g, w_in, sgu_g, w_s, b_s, qn_b, kn_b, qn_c, kn_c,
                     lam_q1, lam_k1, lam_q2, lam_k2, subln_g, conv_w, w_out)
    return (y_prompt, y_sample)
```

```python
import functools
import math

import jax
import jax.numpy as jnp
from jax import lax
from jax.experimental import pallas as pl
from jax.experimental.pallas import tpu as pltpu

F32 = jnp.float32
BF16 = jnp.bfloat16

D_MODEL = 1024
HEAD_DIM = 64
GROUP_HEADS = 6
BRANCH_W = GROUP_HEADS * HEAD_DIM
N_PIECES = 15
PROJ_W = N_PIECES * BRANCH_W
MIX_W = 4 * BRANCH_W
CHUNK = 128
DIL_PATTERNS = ((128, 1), (512, 4), (2048, 16))
N_SIDE = 64
DIFF_QK_DIM = HEAD_DIM // 2
EPS = 1e-6
NEG = -1e30
LOG2E = math.log2(math.e)

LANES = 128
HEAD_PAIRS = BRANCH_W // LANES
V_ROWS = 80

P_AU, P_AV, P_AG, P_BQ, P_BK, P_BV, P_BG, P_CQ, P_CK, P_CV, P_CG, P_DIN, P_DB, P_DC, P_DG = range(15)

IN_TM = 512
IN_TN = 1920
PREP_T = 512
MIX_T = 512
HALO = 16
DIL_TQ = 128
DIL_WIN = DIL_TQ + 2 * N_SIDE
DIFF_TQ = 256
DIFF_TK = 256
VMEM_LIMIT = 56 * 1024 * 1024


def _cparams(sem):
    return pltpu.CompilerParams(dimension_semantics=sem, vmem_limit_bytes=VMEM_LIMIT)


def _inproj_kernel(x_ref, g_ref, w_ref, o_ref):
    x = x_ref[...]
    ms = jnp.mean(x * x, axis=-1, keepdims=True)
    h = (x * lax.rsqrt(ms + EPS) * g_ref[...]).astype(BF16)
    o_ref[...] = jnp.dot(h, w_ref[...], preferred_element_type=F32).astype(BF16)


def _inproj(x2, norm_g, w_in_bf):
    n = x2.shape[0]
    return pl.pallas_call(
        _inproj_kernel,
        grid=(PROJ_W // IN_TN, n // IN_TM),
        in_specs=[
            pl.BlockSpec((IN_TM, D_MODEL), lambda j, i: (i, 0)),
            pl.BlockSpec((1, D_MODEL), lambda j, i: (0, 0)),
            pl.BlockSpec((D_MODEL, IN_TN), lambda j, i: (0, j)),
        ],
        out_specs=pl.BlockSpec((IN_TM, IN_TN), lambda j, i: (i, j)),
        out_shape=jax.ShapeDtypeStruct((n, PROJ_W), BF16),
        compiler_params=_cparams(("arbitrary", "arbitrary")),
        name="inproj",
    )(x2, norm_g.reshape(1, D_MODEL), w_in_bf)


def _group_rms(x, e_ref, width):
    x2 = x * x
    hi = x2.astype(BF16)
    lo = (x2 - hi.astype(F32)).astype(BF16)
    e = e_ref[...]
    ss = jnp.dot(hi, e, preferred_element_type=F32) + jnp.dot(lo, e, preferred_element_type=F32)
    return x * lax.rsqrt(ss * (1.0 / width) + EPS)


def _prep_kernel(bq_ref, bk_ref, cq_ref, ck_ref, cv_ref, gqb_ref, gkb_ref, gqc_ref, gkc_ref,
                 e64_ref, e32_ref, bqn_ref, bkn_ref, kcn_ref, qt_ref, vt_ref):
    t = bq_ref.shape[0]
    row = lax.broadcasted_iota(jnp.int32, (V_ROWS - HEAD_DIM, t), 0)
    ones_pad = jnp.where(row == 0, 1.0, 0.0).astype(BF16)
    for cb in range(HEAD_PAIRS):
        sl = slice(cb * LANES, (cb + 1) * LANES)
        bqn_ref[:, sl] = (_group_rms(bq_ref[:, sl].astype(F32), e64_ref, HEAD_DIM) * gqb_ref[:, sl]).astype(BF16)
        bkn_ref[:, sl] = (_group_rms(bk_ref[:, sl].astype(F32), e64_ref, HEAD_DIM) * gkb_ref[:, sl]).astype(BF16)
        kcn_ref[:, sl] = (_group_rms(ck_ref[:, sl].astype(F32), e32_ref, DIFF_QK_DIM) * gkc_ref[:, sl]).astype(BF16)
        qn = _group_rms(cq_ref[:, sl].astype(F32), e32_ref, DIFF_QK_DIM) * gqc_ref[:, sl]
        qt_ref[0, cb] = qn.T.astype(BF16)
        vt = cv_ref[:, sl].astype(F32).T.astype(BF16)
        for hh in range(2):
            vt_ref[0, cb, hh, 0:HEAD_DIM, :] = vt[hh * HEAD_DIM:(hh + 1) * HEAD_DIM, :]
            vt_ref[0, cb, hh, HEAD_DIM:V_ROWS, :] = ones_pad


def _block_diag_ones(group):
    idx = jnp.arange(LANES) // group
    return (idx[:, None] == idx[None, :]).astype(BF16)


def _prep(proj, bsz, s, gqb, gkb, gqc, gkc):
    n = proj.shape[0]
    tiles_per_seq = s // PREP_T

    def piece(p):
        return pl.BlockSpec((PREP_T, BRANCH_W), lambda i, p=p: (i, p))

    def vec():
        return pl.BlockSpec((1, BRANCH_W), lambda i: (0, 0))

    def tokens():
        return pl.BlockSpec((PREP_T, BRANCH_W), lambda i: (i, 0))

    return pl.pallas_call(
        _prep_kernel,
        grid=(n // PREP_T,),
        in_specs=[piece(P_BQ), piece(P_BK), piece(P_CQ), piece(P_CK), piece(P_CV),
                  vec(), vec(), vec(), vec(),
                  pl.BlockSpec((LANES, LANES), lambda i: (0, 0)),
                  pl.BlockSpec((LANES, LANES), lambda i: (0, 0))],
        out_specs=[tokens(), tokens(), tokens(),
                   pl.BlockSpec((1, HEAD_PAIRS, LANES, PREP_T),
                                lambda i: (i // tiles_per_seq, 0, 0, i % tiles_per_seq)),
                   pl.BlockSpec((1, HEAD_PAIRS, 2, V_ROWS, PREP_T),
                                lambda i: (i // tiles_per_seq, 0, 0, 0, i % tiles_per_seq))],
        out_shape=[jax.ShapeDtypeStruct((n, BRANCH_W), BF16),
                   jax.ShapeDtypeStruct((n, BRANCH_W), BF16),
                   jax.ShapeDtypeStruct((n, BRANCH_W), BF16),
                   jax.ShapeDtypeStruct((bsz, HEAD_PAIRS, LANES, s), BF16),
                   jax.ShapeDtypeStruct((bsz, HEAD_PAIRS, 2, V_ROWS, s), BF16)],
        compiler_params=_cparams(("arbitrary",)),
        name="prep",
    )(proj, proj, proj, proj, proj, gqb, gkb, gqc, gkc,
      _block_diag_ones(HEAD_DIM), _block_diag_ones(DIFF_QK_DIM))


def _dilated_kernel(slope_ref, q_ref, k_ref, v_ref, o_ref, lse_ref, *, dilation, seq_sub):
    hp = pl.program_id(2)
    qi = pl.program_id(3)
    tqb = q_ref.shape[1]
    lane = lax.broadcasted_iota(jnp.int32, (DIL_TQ, LANES), 1)
    first_head = lane < HEAD_DIM
    rel0 = (lax.broadcasted_iota(jnp.int32, (DIL_TQ, DIL_WIN), 1)
            - lax.broadcasted_iota(jnp.int32, (DIL_TQ, DIL_WIN), 0))

    def sub_block(sb, carry):
        r0 = pl.multiple_of(sb * DIL_TQ, DIL_TQ)
        i0 = qi * tqb + r0
        ws = pl.multiple_of(jnp.clip(i0 - N_SIDE, 0, seq_sub - DIL_WIN), N_SIDE)
        kw = k_ref[0, pl.ds(ws, DIL_WIN), :]
        vw = v_ref[0, pl.ds(ws, DIL_WIN), :]
        q = q_ref[0, pl.ds(r0, DIL_TQ), :]
        dist = jnp.abs(rel0 + (ws - i0))
        valid = dist <= N_SIDE
        distf = dist.astype(F32)
        outs, lses = [], []
        for hh in range(2):
            slope = slope_ref[2 * hp + hh] * float(dilation)
            qm = jnp.where(first_head if hh == 0 else jnp.logical_not(first_head), q, jnp.zeros_like(q))
            sc = lax.dot_general(qm, kw, (((1,), (1,)), ((), ())), preferred_element_type=F32)
            sc = jnp.where(valid, sc - slope * distf, NEG)
            m = jnp.max(sc, axis=-1, keepdims=True)
            p = jnp.exp2(sc - m)
            den = jnp.sum(p, axis=-1, keepdims=True)
            o = jnp.dot(p.astype(BF16), vw, preferred_element_type=F32) / den
            outs.append(o)
            lses.append(jnp.broadcast_to(m + jnp.log2(den), (DIL_TQ, LANES)))
        o_ref[0, pl.ds(r0, DIL_TQ), :] = jnp.where(first_head, outs[0], outs[1]).astype(o_ref.dtype)
        lse_ref[0, pl.ds(r0, DIL_TQ), :] = jnp.where(first_head, lses[0], lses[1])
        return carry

    lax.fori_loop(0, tqb // DIL_TQ, sub_block, 0)


def _dilated(slopes2, bqn, bkn, proj, bsz, s, dilation):
    seq_sub = s // dilation
    tqb = min(seq_sub, 1024)
    q3 = bqn.reshape(bsz, seq_sub, dilation * BRANCH_W)
    k3 = bkn.reshape(bsz, seq_sub, dilation * BRANCH_W)
    v3 = proj.reshape(bsz, seq_sub, dilation * PROJ_W)
    blocks_q = BRANCH_W // LANES
    blocks_p = PROJ_W // LANES
    v_off = P_BV * blocks_q

    def tile():
        return pl.BlockSpec((1, tqb, LANES), lambda b, r, hp, qi: (b, qi, r * blocks_q + hp))

    o, lse = pl.pallas_call(
        functools.partial(_dilated_kernel, dilation=dilation, seq_sub=seq_sub),
        grid=(bsz, dilation, HEAD_PAIRS, seq_sub // tqb),
        in_specs=[
            pl.BlockSpec(memory_space=pltpu.SMEM),
            tile(),
            pl.BlockSpec((1, seq_sub, LANES), lambda b, r, hp, qi: (b, 0, r * blocks_q + hp)),
            pl.BlockSpec((1, seq_sub, LANES), lambda b, r, hp, qi: (b, 0, r * blocks_p + v_off + hp)),
        ],
        out_specs=[tile(), tile()],
        out_shape=[jax.ShapeDtypeStruct(q3.shape, BF16), jax.ShapeDtypeStruct(q3.shape, F32)],
        compiler_params=_cparams(("arbitrary",) * 4),
        name=f"dilated{dilation}",
    )(slopes2, q3, k3, v3)
    return o.reshape(bsz * s, BRANCH_W), lse.reshape(bsz * s, BRANCH_W)


def _diff_kernel(lam_ref, slope_ref, qt_ref, k_ref, vt_ref, g_ref, o_ref, qm_ref, tm_ref, acc_ref,
                 *, out_scale):
    hp = pl.program_id(1)
    qi = pl.program_id(2)
    tq = qt_ref.shape[3]
    tk = DIFF_TK
    nk = k_ref.shape[1] // tk
    i0 = qi * tq

    qt = qt_ref[0, 0]
    row_group = lax.broadcasted_iota(jnp.int32, (LANES, tq), 0) // DIFF_QK_DIM
    for g in range(4):
        qm_ref[g] = jnp.where(row_group == g, qt, jnp.zeros_like(qt))
    dkq = (lax.broadcasted_iota(jnp.int32, (tk, tq), 0)
           - lax.broadcasted_iota(jnp.int32, (tk, tq), 1)).astype(F32)
    for hh in range(2):
        tm_ref[hh] = slope_ref[2 * hp + hh] * dkq
    acc_ref[...] = jnp.zeros_like(acc_ref)

    def key_tile(j, ms, mode):
        koff = pl.multiple_of(j * tk, tk)
        kt = k_ref[0, pl.ds(koff, tk), :]
        tile_dist = jnp.abs(i0 - koff).astype(F32)
        new_ms = []
        for hh in range(2):
            vt = vt_ref[0, 0, hh, :, pl.ds(koff, tk)]
            shift = slope_ref[2 * hp + hh] * tile_dist
            for c in range(2):
                g = 2 * hh + c
                sc = jnp.dot(kt, qm_ref[g], preferred_element_type=F32)
                if mode < 0:
                    sc = sc + tm_ref[hh]
                elif mode > 0:
                    sc = sc - tm_ref[hh]
                else:
                    sc = sc - jnp.abs(tm_ref[hh])
                m_old = ms[g]
                m_new = jnp.maximum(m_old, jnp.max(sc, axis=0, keepdims=True) - shift)
                p = jnp.exp2(sc - (m_new + shift))
                alpha = jnp.exp2(m_old - m_new)
                acc_ref[g] = alpha * acc_ref[g] + jnp.dot(vt, p.astype(BF16), preferred_element_type=F32)
                new_ms.append(m_new)
        return tuple(new_ms)

    ms = tuple(jnp.full((1, tq), NEG, F32) for _ in range(4))
    ms = lax.fori_loop(0, qi, lambda j, m: key_tile(j, m, -1), ms)
    ms = key_tile(qi, ms, 0)
    ms = lax.fori_loop(qi + 1, nk, lambda j, m: key_tile(j, m, 1), ms)

    lam = lam_ref[0]
    heads = []
    for hh in range(2):
        a0 = acc_ref[2 * hh]
        a1 = acc_ref[2 * hh + 1]
        o = (a0[0:HEAD_DIM] / a0[HEAD_DIM:HEAD_DIM + 1]
             - lam * (a1[0:HEAD_DIM] / a1[HEAD_DIM:HEAD_DIM + 1]))
        ms2 = jnp.mean(o * o, axis=0, keepdims=True)
        heads.append(o * lax.rsqrt(ms2 + EPS))
    out = jnp.concatenate(heads, axis=0).T
    o_ref[0] = (out * (g_ref[...] * out_scale)).astype(o_ref.dtype)


def _diff(lam, slopes2, qt, kcn, vt, subln_pair, bsz, s, lambda_init):
    k3 = kcn.reshape(bsz, s, BRANCH_W)
    return pl.pallas_call(
        functools.partial(_diff_kernel, out_scale=1.0 - lambda_init),
        grid=(bsz, HEAD_PAIRS, s // DIFF_TQ),
        in_specs=[
            pl.BlockSpec(memory_space=pltpu.SMEM),
            pl.BlockSpec(memory_space=pltpu.SMEM),
            pl.BlockSpec((1, 1, LANES, DIFF_TQ), lambda b, hp, qi: (b, hp, 0, qi)),
            pl.BlockSpec((1, s, LANES), lambda b, hp, qi: (b, 0, hp)),
            pl.BlockSpec((1, 1, 2, V_ROWS, s), lambda b, hp, qi: (b, hp, 0, 0, 0)),
            pl.BlockSpec((1, LANES), lambda b, hp, qi: (0, 0)),
        ],
        out_specs=pl.BlockSpec((1, DIFF_TQ, LANES), lambda b, hp, qi: (b, qi, hp)),
        out_shape=jax.ShapeDtypeStruct((bsz, s, BRANCH_W), BF16),
        scratch_shapes=[pltpu.VMEM((4, LANES, DIFF_TQ), BF16),
                        pltpu.VMEM((2, DIFF_TK, DIFF_TQ), F32),
                        pltpu.VMEM((4, V_ROWS, DIFF_TQ), F32)],
        compiler_params=_cparams(("arbitrary",) * 3),
        name="diff",
    )(lam, slopes2, qt, k3, vt, subln_pair).reshape(bsz * s, BRANCH_W)


def _silu(x):
    return x * jax.nn.sigmoid(x)


def _mix_kernel(x_ref, au_ref, av_ref, ag_ref, bg_ref, cg_ref, din_ref, db_ref, dc_ref, dg_ref,
                din_p_ref, dc_p_ref, din_n_ref, dc_n_ref,
                o1_ref, o4_ref, o16_ref, l1_ref, l4_ref, l16_ref, oc_ref,
                sgu_ref, ws_ref, bs_ref, cw_ref, wout_ref, y_ref, *, tiles_per_seq):
    i = pl.program_id(0)
    t = x_ref.shape[0]

    av = av_ref[...].astype(F32)
    vn = (av * lax.rsqrt(jnp.mean(av * av, axis=-1, keepdims=True) + EPS) * sgu_ref[...]).astype(BF16)
    lane = lax.broadcasted_iota(jnp.int32, (CHUNK, LANES), 1)
    first_head = lane < HEAD_DIM
    chunks = []
    for c in range(t // CHUNK):
        vc = vn[c * CHUNK:(c + 1) * CHUNK, :]
        pairs = []
        for hp in range(HEAD_PAIRS):
            vp = vc[:, hp * LANES:(hp + 1) * LANES]
            r0 = jnp.dot(ws_ref[2 * hp], vp, preferred_element_type=F32)
            r1 = jnp.dot(ws_ref[2 * hp + 1], vp, preferred_element_type=F32)
            pairs.append(jnp.where(first_head, r0, r1))
        chunks.append(jnp.concatenate(pairs, axis=1) + bs_ref[...])
    out_a = au_ref[...].astype(F32) * jnp.concatenate(chunks, axis=0)

    l1, l4, l16 = l1_ref[...], l4_ref[...], l16_ref[...]
    lm = jnp.maximum(jnp.maximum(l1, l4), l16)
    w1, w4, w16 = jnp.exp2(l1 - lm), jnp.exp2(l4 - lm), jnp.exp2(l16 - lm)
    out_b = ((w1 * o1_ref[...].astype(F32) + w4 * o4_ref[...].astype(F32) + w16 * o16_ref[...].astype(F32))
             / (w1 + w4 + w16))

    z = dc_ref[...].astype(F32) * din_ref[...].astype(F32)
    seq_pos = i % tiles_per_seq
    z_prev = (dc_p_ref[...].astype(F32) * din_p_ref[...].astype(F32))[HALO - 1:HALO, :]
    z_next = (dc_n_ref[...].astype(F32) * din_n_ref[...].astype(F32))[0:1, :]
    z_prev = jnp.where(seq_pos == 0, 0.0, z_prev)
    z_next = jnp.where(seq_pos == tiles_per_seq - 1, 0.0, z_next)
    row = lax.broadcasted_iota(jnp.int32, z.shape, 0)
    z_up = jnp.where(row == 0, z_prev, pltpu.roll(z, 1, axis=0))
    z_dn = jnp.where(row == t - 1, z_next, pltpu.roll(z, t - 1, axis=0))
    cw = cw_ref[...]
    out_d = db_ref[...].astype(F32) * (cw[0:1] * z_up + cw[1:2] * z + cw[2:3] * z_dn)

    mixed = jnp.concatenate([
        (_silu(ag_ref[...].astype(F32)) * out_a).astype(BF16),
        (_silu(bg_ref[...].astype(F32)) * out_b).astype(BF16),
        (_silu(cg_ref[...].astype(F32)) * oc_ref[...].astype(F32)).astype(BF16),
        (_silu(dg_ref[...].astype(F32)) * out_d).astype(BF16)], axis=1)
    y_ref[...] = x_ref[...] + jnp.dot(mixed, wout_ref[...], preferred_element_type=F32)


def _mix_out(x2, proj, dil, oc, sgu_g, ws_bf, bs_tab, conv_w, wout_bf, s):
    n = x2.shape[0]
    tiles_per_seq = s // MIX_T
    halo_per_tile = MIX_T // HALO
    n_halo = n // HALO

    def piece(p):
        return pl.BlockSpec((MIX_T, BRANCH_W), lambda i, p=p: (i, p))

    def halo_prev(p):
        return pl.BlockSpec((HALO, BRANCH_W), lambda i, p=p: (jnp.maximum(i * halo_per_tile - 1, 0), p))

    def halo_next(p):
        return pl.BlockSpec((HALO, BRANCH_W),
                            lambda i, p=p: (jnp.minimum((i + 1) * halo_per_tile, n_halo - 1), p))

    def tokens():
        return pl.BlockSpec((MIX_T, BRANCH_W), lambda i: (i, 0))

    def const(shape):
        return pl.BlockSpec(shape, lambda i: (0,) * len(shape))

    (o1, l1), (o4, l4), (o16, l16) = dil
    return pl.pallas_call(
        functools.partial(_mix_kernel, tiles_per_seq=tiles_per_seq),
        grid=(n // MIX_T,),
        in_specs=[pl.BlockSpec((MIX_T, D_MODEL), lambda i: (i, 0)),
                  piece(P_AU), piece(P_AV), piece(P_AG), piece(P_BG), piece(P_CG),
                  piece(P_DIN), piece(P_DB), piece(P_DC), piece(P_DG),
                  halo_prev(P_DIN), halo_prev(P_DC), halo_next(P_DIN), halo_next(P_DC),
                  tokens(), tokens(), tokens(), tokens(), tokens(), tokens(), tokens(),
                  const((1, BRANCH_W)), const((GROUP_HEADS, CHUNK, CHUNK)), const((CHUNK, BRANCH_W)),
                  const((3, BRANCH_W)), const((MIX_W, D_MODEL))],
        out_specs=pl.BlockSpec((MIX_T, D_MODEL), lambda i: (i, 0)),
        out_shape=jax.ShapeDtypeStruct((n, D_MODEL), F32),
        compiler_params=_cparams(("arbitrary",)),
        name="mixout",
    )(x2, *([proj] * 13), o1, o4, o16, l1, l4, l16, oc,
      sgu_g.reshape(1, BRANCH_W), ws_bf, bs_tab, conv_w, wout_bf)


def _layer(x2, bsz, s, l, norm_g, w_in, sgu_g, w_s, b_s, qn_b, kn_b, qn_c, kn_c,
           lam_q1, lam_k1, lam_q2, lam_k2, subln_g, conv_w, w_out):
    slopes2 = (2.0 ** (-8.0 * jnp.arange(1, GROUP_HEADS + 1, dtype=F32) / GROUP_HEADS)) * LOG2E
    lambda_init = 0.8 - 0.6 * math.exp(-0.3 * l)
    lam = (jnp.exp(jnp.sum(lam_q1 * lam_k1)) - jnp.exp(jnp.sum(lam_q2 * lam_k2)) + lambda_init).reshape(1)

    proj = _inproj(x2, norm_g, w_in.astype(BF16))
    gqb = (jnp.tile(qn_b, GROUP_HEADS) * (LOG2E / math.sqrt(HEAD_DIM))).reshape(1, BRANCH_W)
    gkb = jnp.tile(kn_b, GROUP_HEADS).reshape(1, BRANCH_W)
    gqc = (jnp.tile(qn_c, 2 * GROUP_HEADS) * (LOG2E / math.sqrt(DIFF_QK_DIM))).reshape(1, BRANCH_W)
    gkc = jnp.tile(kn_c, 2 * GROUP_HEADS).reshape(1, BRANCH_W)
    bqn, bkn, kcn, qt, vt = _prep(proj, bsz, s, gqb, gkb, gqc, gkc)

    dil = [_dilated(slopes2, bqn, bkn, proj, bsz, s, d) for (_, d) in DIL_PATTERNS]
    oc = _diff(lam, slopes2, qt, kcn, vt, jnp.tile(subln_g, 2).reshape(1, LANES), bsz, s, lambda_init)

    bs_tab = jnp.repeat(b_s.T, HEAD_DIM, axis=1)
    return _mix_out(x2, proj, dil, oc, sgu_g, w_s.astype(BF16), bs_tab, conv_w, w_out.astype(BF16), s)


def _trunk(x, params):
    bsz, s, _ = x.shape
    x2 = x.reshape(bsz * s, D_MODEL)
    depth = params[0].shape[0]
    for l in range(depth):
        x2 = _layer(x2, bsz, s, l, *[p[l] for p in params])
    return x2.reshape(bsz, s, D_MODEL)


def kernel(x_prompt, x_sample, norm_g, w_in, sgu_g, w_s, b_s, qn_b, kn_b, qn_c, kn_c,
           lam_q1, lam_k1, lam_q2, lam_k2, subln_g, conv_w, w_out):
    params = (norm_g, w_in, sgu_g, w_s, b_s, qn_b, kn_b, qn_c, kn_c,
              lam_q1, lam_k1, lam_q2, lam_k2, subln_g, conv_w, w_out)
    return (_trunk(x_prompt, params), _trunk(x_sample, params))
```

```python
import functools
import math

import jax
import jax.numpy as jnp
from jax import lax
from jax.experimental import pallas as pl
from jax.experimental.pallas import tpu as pltpu

F32 = jnp.float32
BF16 = jnp.bfloat16

D_MODEL = 1024
HEAD_DIM = 64
GROUP_HEADS = 6
BRANCH_W = GROUP_HEADS * HEAD_DIM
N_PIECES = 15
PROJ_W = N_PIECES * BRANCH_W
MIX_W = 4 * BRANCH_W
CHUNK = 128
DIL_PATTERNS = ((128, 1), (512, 4), (2048, 16))
N_SIDE = 64
DIFF_QK_DIM = HEAD_DIM // 2
EPS = 1e-6
NEG = -1e30
LOG2E = math.log2(math.e)

LANES = 128
HEAD_PAIRS = BRANCH_W // LANES
V_ROWS = 80

P_AU, P_AV, P_AG, P_BQ, P_BK, P_BV, P_BG, P_CQ, P_CK, P_CV, P_CG, P_DIN, P_DB, P_DC, P_DG = range(15)

IN_TM = 512
IN_TN = 1920
PREP_T = 512
MIX_T = 512
HALO = 16
DIL_TQ = 128
DIL_WIN = DIL_TQ + 2 * N_SIDE
DIFF_TQ = 256
DIFF_TK = 256
VMEM_LIMIT = 56 * 1024 * 1024
MAX_FIXED_SHIFT = 40.0


def _cparams(sem):
    return pltpu.CompilerParams(dimension_semantics=sem, vmem_limit_bytes=VMEM_LIMIT)


def _inproj_kernel(x_ref, g_ref, w_ref, o_ref):
    x = x_ref[...]
    ms = jnp.mean(x * x, axis=-1, keepdims=True)
    h = (x * lax.rsqrt(ms + EPS) * g_ref[...]).astype(BF16)
    o_ref[...] = jnp.dot(h, w_ref[...], preferred_element_type=F32).astype(BF16)


def _inproj(x2, norm_g, w_in_bf):
    n = x2.shape[0]
    return pl.pallas_call(
        _inproj_kernel,
        grid=(PROJ_W // IN_TN, n // IN_TM),
        in_specs=[
            pl.BlockSpec((IN_TM, D_MODEL), lambda j, i: (i, 0)),
            pl.BlockSpec((1, D_MODEL), lambda j, i: (0, 0)),
            pl.BlockSpec((D_MODEL, IN_TN), lambda j, i: (0, j)),
        ],
        out_specs=pl.BlockSpec((IN_TM, IN_TN), lambda j, i: (i, j)),
        out_shape=jax.ShapeDtypeStruct((n, PROJ_W), BF16),
        compiler_params=_cparams(("arbitrary", "arbitrary")),
        name="inproj",
    )(x2, norm_g.reshape(1, D_MODEL), w_in_bf)


def _group_rms(x, e_ref, width):
    x2 = x * x
    hi = x2.astype(BF16)
    lo = (x2 - hi.astype(F32)).astype(BF16)
    e = e_ref[...]
    ss = jnp.dot(hi, e, preferred_element_type=F32) + jnp.dot(lo, e, preferred_element_type=F32)
    return x * lax.rsqrt(ss * (1.0 / width) + EPS)


def _prep_kernel(bq_ref, bk_ref, bv_ref, cq_ref, ck_ref, cv_ref, gqb_ref, gkb_ref, gqc_ref, gkc_ref,
                 e64_ref, e32_ref, bqn_ref, bkn_ref, kcn_ref, qt_ref, vt_ref,
                 q4_ref, k4_ref, v4_ref, q16_ref, k16_ref, v16_ref, sq_ref, sk_ref, sv_ref):
    t = bq_ref.shape[0]
    row = lax.broadcasted_iota(jnp.int32, (V_ROWS - HEAD_DIM, t), 0)
    ones_pad = jnp.where(row == 0, 1.0, 0.0).astype(BF16)
    for cb in range(HEAD_PAIRS):
        sl = slice(cb * LANES, (cb + 1) * LANES)
        sq_ref[cb] = _group_rms(bq_ref[:, sl].astype(F32), e64_ref, HEAD_DIM) * gqb_ref[:, sl]
        sk_ref[cb] = _group_rms(bk_ref[:, sl].astype(F32), e64_ref, HEAD_DIM) * gkb_ref[:, sl]
        sv_ref[cb] = bv_ref[:, sl].astype(F32)
        bqn_ref[:, sl] = sq_ref[cb].astype(BF16)
        bkn_ref[:, sl] = sk_ref[cb].astype(BF16)
        kcn_ref[:, sl] = (_group_rms(ck_ref[:, sl].astype(F32), e32_ref, DIFF_QK_DIM) * gkc_ref[:, sl]).astype(BF16)
        qn = _group_rms(cq_ref[:, sl].astype(F32), e32_ref, DIFF_QK_DIM) * gqc_ref[:, sl]
        qt_ref[0, cb] = qn.T.astype(BF16)
        vt = cv_ref[:, sl].astype(F32).T.astype(BF16)
        for hh in range(2):
            vt_ref[0, cb, hh, 0:HEAD_DIM, :] = vt[hh * HEAD_DIM:(hh + 1) * HEAD_DIM, :]
            vt_ref[0, cb, hh, HEAD_DIM:V_ROWS, :] = ones_pad
    for stage, out4, out16 in ((sq_ref, q4_ref, q16_ref), (sk_ref, k4_ref, k16_ref), (sv_ref, v4_ref, v16_ref)):
        for dilation, out in ((4, out4), (16, out16)):
            for r in range(dilation):
                for cb in range(HEAD_PAIRS):
                    out[0, r, :, cb * LANES:(cb + 1) * LANES] = (
                        stage[cb, pl.ds(r, t // dilation, stride=dilation), :].astype(BF16))


def _block_diag_ones(group):
    idx = jnp.arange(LANES) // group
    return (idx[:, None] == idx[None, :]).astype(BF16)


def _prep(proj, bsz, s, gqb, gkb, gqc, gkc):
    n = proj.shape[0]
    tiles_per_seq = s // PREP_T

    def piece(p):
        return pl.BlockSpec((PREP_T, BRANCH_W), lambda i, p=p: (i, p))

    def vec():
        return pl.BlockSpec((1, BRANCH_W), lambda i: (0, 0))

    def tokens():
        return pl.BlockSpec((PREP_T, BRANCH_W), lambda i: (i, 0))

    def residues(d):
        return pl.BlockSpec((1, d, PREP_T // d, BRANCH_W),
                            lambda i: (i // tiles_per_seq, 0, i % tiles_per_seq, 0))

    return pl.pallas_call(
        _prep_kernel,
        grid=(n // PREP_T,),
        in_specs=[piece(P_BQ), piece(P_BK), piece(P_BV), piece(P_CQ), piece(P_CK), piece(P_CV),
                  vec(), vec(), vec(), vec(),
                  pl.BlockSpec((LANES, LANES), lambda i: (0, 0)),
                  pl.BlockSpec((LANES, LANES), lambda i: (0, 0))],
        out_specs=[tokens(), tokens(), tokens(),
                   pl.BlockSpec((1, HEAD_PAIRS, LANES, PREP_T),
                                lambda i: (i // tiles_per_seq, 0, 0, i % tiles_per_seq)),
                   pl.BlockSpec((1, HEAD_PAIRS, 2, V_ROWS, PREP_T),
                                lambda i: (i // tiles_per_seq, 0, 0, 0, i % tiles_per_seq))]
                  + [residues(d) for d in (4, 4, 4, 16, 16, 16)],
        out_shape=[jax.ShapeDtypeStruct((n, BRANCH_W), BF16),
                   jax.ShapeDtypeStruct((n, BRANCH_W), BF16),
                   jax.ShapeDtypeStruct((n, BRANCH_W), BF16),
                   jax.ShapeDtypeStruct((bsz, HEAD_PAIRS, LANES, s), BF16),
                   jax.ShapeDtypeStruct((bsz, HEAD_PAIRS, 2, V_ROWS, s), BF16)]
                  + [jax.ShapeDtypeStruct((bsz, d, s // d, BRANCH_W), BF16) for d in (4, 4, 4, 16, 16, 16)],
        scratch_shapes=[pltpu.VMEM((HEAD_PAIRS, PREP_T, LANES), F32)] * 3,
        compiler_params=_cparams(("arbitrary",)),
        name="prep",
    )(proj, proj, proj, proj, proj, proj, gqb, gkb, gqc, gkc,
      _block_diag_ones(HEAD_DIM), _block_diag_ones(DIFF_QK_DIM))


def _dilated_kernel(slope_ref, q_ref, k_ref, v_ref, o_ref, lse_ref, *, dilation, seq_sub):
    hp = pl.program_id(2)
    qi = pl.program_id(3)
    tqb = q_ref.shape[1]
    lane = lax.broadcasted_iota(jnp.int32, (DIL_TQ, LANES), 1)
    first_head = lane < HEAD_DIM
    rel0 = (lax.broadcasted_iota(jnp.int32, (DIL_TQ, DIL_WIN), 1)
            - lax.broadcasted_iota(jnp.int32, (DIL_TQ, DIL_WIN), 0))

    def sub_block(sb, carry):
        r0 = pl.multiple_of(sb * DIL_TQ, DIL_TQ)
        i0 = qi * tqb + r0
        ws = pl.multiple_of(jnp.clip(i0 - N_SIDE, 0, seq_sub - DIL_WIN), N_SIDE)
        kw = k_ref[0, pl.ds(ws, DIL_WIN), :]
        vw = v_ref[0, pl.ds(ws, DIL_WIN), :]
        q = q_ref[0, pl.ds(r0, DIL_TQ), :]
        dist = jnp.abs(rel0 + (ws - i0))
        valid = dist <= N_SIDE
        distf = dist.astype(F32)
        outs, lses = [], []
        for hh in range(2):
            slope = slope_ref[2 * hp + hh] * float(dilation)
            qm = jnp.where(first_head if hh == 0 else jnp.logical_not(first_head), q, jnp.zeros_like(q))
            sc = lax.dot_general(qm, kw, (((1,), (1,)), ((), ())), preferred_element_type=F32)
            sc = jnp.where(valid, sc - slope * distf, NEG)
            m = jnp.max(sc, axis=-1, keepdims=True)
            p = jnp.exp2(sc - m)
            den = jnp.sum(p, axis=-1, keepdims=True)
            o = jnp.dot(p.astype(BF16), vw, preferred_element_type=F32) / den
            outs.append(o)
            lses.append(jnp.broadcast_to(m + jnp.log2(den), (DIL_TQ, LANES)))
        o_ref[0, pl.ds(r0, DIL_TQ), :] = jnp.where(first_head, outs[0], outs[1]).astype(o_ref.dtype)
        lse_ref[0, pl.ds(r0, DIL_TQ), :] = jnp.where(first_head, lses[0], lses[1])
        return carry

    lax.fori_loop(0, tqb // DIL_TQ, sub_block, 0)


def _dilated(slopes2, bqn, bkn, proj, bsz, s, dilation):
    seq_sub = s // dilation
    tqb = min(seq_sub, 1024)
    q3 = bqn.reshape(bsz, seq_sub, dilation * BRANCH_W)
    k3 = bkn.reshape(bsz, seq_sub, dilation * BRANCH_W)
    v3 = proj.reshape(bsz, seq_sub, dilation * PROJ_W)
    blocks_q = BRANCH_W // LANES
    blocks_p = PROJ_W // LANES
    v_off = P_BV * blocks_q

    def tile():
        return pl.BlockSpec((1, tqb, LANES), lambda b, r, hp, qi: (b, qi, r * blocks_q + hp))

    o, lse = pl.pallas_call(
        functools.partial(_dilated_kernel, dilation=dilation, seq_sub=seq_sub),
        grid=(bsz, dilation, HEAD_PAIRS, seq_sub // tqb),
        in_specs=[
            pl.BlockSpec(memory_space=pltpu.SMEM),
            tile(),
            pl.BlockSpec((1, seq_sub, LANES), lambda b, r, hp, qi: (b, 0, r * blocks_q + hp)),
            pl.BlockSpec((1, seq_sub, LANES), lambda b, r, hp, qi: (b, 0, r * blocks_p + v_off + hp)),
        ],
        out_specs=[tile(), tile()],
        out_shape=[jax.ShapeDtypeStruct(q3.shape, BF16), jax.ShapeDtypeStruct(q3.shape, F32)],
        compiler_params=_cparams(("arbitrary",) * 4),
        name=f"dilated{dilation}",
    )(slopes2, q3, k3, v3)
    return o.reshape(bsz * s, BRANCH_W), lse.reshape(bsz * s, BRANCH_W)


def _dilated_bounded_kernel(tab_ref, q_ref, k_ref, v_ref, num_ref, den_ref, p_ref, *, seq_sub):
    qi = pl.program_id(3)
    tqb = q_ref.shape[2]
    first_head = lax.broadcasted_iota(jnp.int32, (DIL_WIN, LANES), 1) < HEAD_DIM
    first_head_q = lax.broadcasted_iota(jnp.int32, (DIL_TQ, LANES), 1) < HEAD_DIM

    def window(sb):
        r0 = pl.multiple_of(sb * DIL_TQ, DIL_TQ)
        i0 = qi * tqb + r0
        ws = pl.multiple_of(jnp.clip(i0 - N_SIDE, 0, seq_sub - DIL_WIN), N_SIDE)
        variant = jnp.where(i0 == 0, 0, jnp.where(i0 == seq_sub - DIL_TQ, 2, 1))
        return r0, ws, variant

    def probabilities(u):
        for half in range(2):
            r0, ws, variant = window(2 * u + half)
            kw = k_ref[0, 0, pl.ds(ws, DIL_WIN), :]
            q = q_ref[0, 0, pl.ds(r0, DIL_TQ), :]
            zero = jnp.zeros_like(q)
            q2 = jnp.concatenate([jnp.where(first_head_q, q, zero), jnp.where(first_head_q, zero, q)], axis=0)
            sc = lax.dot_general(q2, kw, (((1,), (1,)), ((), ())), preferred_element_type=F32)
            for hh in range(2):
                p_ref[half, hh] = jnp.exp2(sc[hh * DIL_TQ:(hh + 1) * DIL_TQ] + tab_ref[hh, variant]).astype(BF16)

    def outputs(u):
        for half in range(2):
            r0, ws, _ = window(2 * u + half)
            vw = v_ref[0, 0, pl.ds(ws, DIL_WIN), :]
            ones = jnp.ones_like(vw)
            a0 = jnp.dot(p_ref[half, 0], jnp.where(first_head, vw, ones), preferred_element_type=F32)
            a1 = jnp.dot(p_ref[half, 1], jnp.where(first_head, ones, vw), preferred_element_type=F32)
            num_ref[0, 0, pl.ds(r0, DIL_TQ), :] = jnp.where(first_head_q, a0, a1).astype(num_ref.dtype)
            den_ref[0, 0, pl.ds(r0, DIL_TQ), :] = pltpu.roll(jnp.where(first_head_q, a1, a0), HEAD_DIM, axis=1)

    n_pairs = tqb // (2 * DIL_TQ)
    probabilities(0)

    def step(u, carry):
        outputs(u - 1)
        probabilities(u)
        return carry

    lax.fori_loop(1, n_pairs, step, 0)
    outputs(n_pairs - 1)


def _dilated_tables(slopes2, bound, dilation):
    rel = ((jnp.arange(DIL_WIN)[None, :] - jnp.arange(DIL_TQ)[:, None])[None]
           - jnp.array([0, N_SIDE, 2 * N_SIDE])[:, None, None])
    dist = jnp.abs(rel).astype(F32)
    bias = -(slopes2 * float(dilation))[:, None, None, None] * dist[None] - bound
    return jnp.where(dist[None] <= N_SIDE, bias, NEG).astype(F32)


def _dilated_bounded(tables, q4d, k4d, v4d, v_block_offset, bsz, s, dilation):
    seq_sub = s // dilation
    tqb = min(seq_sub, 2048)

    def tile():
        return pl.BlockSpec((1, 1, tqb, LANES), lambda b, r, hp, qi: (b, r, qi, hp))

    out_shape = (bsz, dilation, seq_sub, BRANCH_W)
    return pl.pallas_call(
        functools.partial(_dilated_bounded_kernel, seq_sub=seq_sub),
        grid=(bsz, dilation, HEAD_PAIRS, seq_sub // tqb),
        in_specs=[
            pl.BlockSpec((2, 3, DIL_TQ, DIL_WIN), lambda b, r, hp, qi: (hp, 0, 0, 0)),
            tile(),
            pl.BlockSpec((1, 1, seq_sub, LANES), lambda b, r, hp, qi: (b, r, 0, hp)),
            pl.BlockSpec((1, 1, seq_sub, LANES), lambda b, r, hp, qi: (b, r, 0, v_block_offset + hp)),
        ],
        out_specs=[tile(), tile()],
        out_shape=[jax.ShapeDtypeStruct(out_shape, BF16), jax.ShapeDtypeStruct(out_shape, F32)],
        scratch_shapes=[pltpu.VMEM((2, 2, DIL_TQ, DIL_WIN), BF16)],
        compiler_params=_cparams(("arbitrary",) * 4),
        name=f"dilated_bounded{dilation}",
    )(tables, q4d, k4d, v4d)


def _diff_kernel(lam_ref, slope_ref, qt_ref, k_ref, vt_ref, g_ref, o_ref, qm_ref, tm_ref, acc_ref,
                 *, out_scale):
    hp = pl.program_id(1)
    qi = pl.program_id(2)
    tq = qt_ref.shape[3]
    tk = DIFF_TK
    nk = k_ref.shape[1] // tk
    i0 = qi * tq

    qt = qt_ref[0, 0]
    row_group = lax.broadcasted_iota(jnp.int32, (LANES, tq), 0) // DIFF_QK_DIM
    for g in range(4):
        qm_ref[g] = jnp.where(row_group == g, qt, jnp.zeros_like(qt))
    dkq = (lax.broadcasted_iota(jnp.int32, (tk, tq), 0)
           - lax.broadcasted_iota(jnp.int32, (tk, tq), 1)).astype(F32)
    for hh in range(2):
        tm_ref[hh] = slope_ref[2 * hp + hh] * dkq
    acc_ref[...] = jnp.zeros_like(acc_ref)

    def key_tile(j, ms, mode):
        koff = pl.multiple_of(j * tk, tk)
        kt = k_ref[0, pl.ds(koff, tk), :]
        tile_dist = jnp.abs(i0 - koff).astype(F32)
        new_ms = []
        for hh in range(2):
            vt = vt_ref[0, 0, hh, :, pl.ds(koff, tk)]
            shift = slope_ref[2 * hp + hh] * tile_dist
            for c in range(2):
                g = 2 * hh + c
                sc = jnp.dot(kt, qm_ref[g], preferred_element_type=F32)
                if mode < 0:
                    sc = sc + tm_ref[hh]
                elif mode > 0:
                    sc = sc - tm_ref[hh]
                else:
                    sc = sc - jnp.abs(tm_ref[hh])
                m_old = ms[g]
                m_new = jnp.maximum(m_old, jnp.max(sc, axis=0, keepdims=True) - shift)
                p = jnp.exp2(sc - (m_new + shift))
                alpha = jnp.exp2(m_old - m_new)
                acc_ref[g] = alpha * acc_ref[g] + jnp.dot(vt, p.astype(BF16), preferred_element_type=F32)
                new_ms.append(m_new)
        return tuple(new_ms)

    ms = tuple(jnp.full((1, tq), NEG, F32) for _ in range(4))
    ms = lax.fori_loop(0, qi, lambda j, m: key_tile(j, m, -1), ms)
    ms = key_tile(qi, ms, 0)
    lax.fori_loop(qi + 1, nk, lambda j, m: key_tile(j, m, 1), ms)
    _diff_finish(lam_ref, g_ref, o_ref, acc_ref, out_scale)


def _diff_bounded_kernel(lam_ref, slope_ref, qt_ref, k_ref, vt_ref, g_ref, o_ref, qm_ref, tm_ref, acc_ref,
                         p_ref, *, out_scale):
    hp = pl.program_id(1)
    qi = pl.program_id(2)
    tq = qt_ref.shape[3]
    tk = DIFF_TK
    nk = k_ref.shape[1] // tk
    i0 = qi * tq
    bound = lam_ref[1]

    qt = qt_ref[0, 0]
    row_group = lax.broadcasted_iota(jnp.int32, (LANES, tq), 0) // DIFF_QK_DIM
    for g in range(4):
        qm_ref[g] = jnp.where(row_group == g, qt, jnp.zeros_like(qt))
    dkq = (lax.broadcasted_iota(jnp.int32, (tk, tq), 0)
           - lax.broadcasted_iota(jnp.int32, (tk, tq), 1)).astype(F32)
    for hh in range(2):
        tm = slope_ref[2 * hp + hh] * dkq
        tm_ref[hh, 0] = tm
        tm_ref[hh, 1] = -tm
        tm_ref[hh, 2] = -jnp.abs(tm)
    acc_ref[...] = jnp.zeros_like(acc_ref)

    def stage(scores=None, values=None):
        for hh in range(2):
            slope = slope_ref[2 * hp + hh]
            for c in range(2):
                g = 2 * hh + c
                parts = []
                for half in range(2):
                    if values is not None:
                        u, slot = values
                        koff = pl.multiple_of((2 * u + half) * tk, tk)
                        parts.append(jnp.dot(vt_ref[0, 0, hh, :, pl.ds(koff, tk)], p_ref[slot, half, g],
                                             preferred_element_type=F32))
                    if scores is not None:
                        u, slot = scores
                        j = 2 * u + half
                        koff = pl.multiple_of(j * tk, tk)
                        shift = -(slope * jnp.abs(i0 - koff).astype(F32) + bound)
                        table = jnp.where(j < qi, 0, jnp.where(j > qi, 1, 2))
                        sc = jnp.dot(k_ref[0, pl.ds(koff, tk), :], qm_ref[g], preferred_element_type=F32)
                        p_ref[slot, half, g] = jnp.exp2(sc + tm_ref[hh, table] + shift).astype(BF16)
                if parts:
                    acc_ref[g] += parts[0] + parts[1]

    n_pairs = nk // 2
    stage(scores=(0, 0))

    def trip(w, carry):
        stage(scores=(2 * w + 1, 1), values=(2 * w, 0))
        stage(scores=(2 * w + 2, 0), values=(2 * w + 1, 1))
        return carry

    lax.fori_loop(0, n_pairs // 2 - 1, trip, 0)
    stage(scores=(n_pairs - 1, 1), values=(n_pairs - 2, 0))
    stage(values=(n_pairs - 1, 1))

    _diff_finish(lam_ref, g_ref, o_ref, acc_ref, out_scale)


def _diff_finish(lam_ref, g_ref, o_ref, acc_ref, out_scale):
    lam = lam_ref[0]
    heads = []
    for hh in range(2):
        a0 = acc_ref[2 * hh]
        a1 = acc_ref[2 * hh + 1]
        o = (a0[0:HEAD_DIM] / a0[HEAD_DIM:HEAD_DIM + 1]
             - lam * (a1[0:HEAD_DIM] / a1[HEAD_DIM:HEAD_DIM + 1]))
        ms2 = jnp.mean(o * o, axis=0, keepdims=True)
        heads.append(o * lax.rsqrt(ms2 + EPS))
    out = jnp.concatenate(heads, axis=0).T
    o_ref[0] = (out * (g_ref[...] * out_scale)).astype(o_ref.dtype)


def _diff(bounded, lam, score_bound, slopes2, qt, kcn, vt, subln_pair, bsz, s, lambda_init):
    k3 = kcn.reshape(bsz, s, BRANCH_W)
    scalars = jnp.concatenate([lam, score_bound]).astype(F32)
    if bounded:
        body, name = _diff_bounded_kernel, "diff_bounded"
        table_scratch = [pltpu.VMEM((2, 3, DIFF_TK, DIFF_TQ), F32)]
        stage_scratch = [pltpu.VMEM((2, 2, 4, DIFF_TK, DIFF_TQ), BF16)]
    else:
        body, name = _diff_kernel, "diff_online"
        table_scratch = [pltpu.VMEM((2, DIFF_TK, DIFF_TQ), F32)]
        stage_scratch = []
    return pl.pallas_call(
        functools.partial(body, out_scale=1.0 - lambda_init),
        grid=(bsz, HEAD_PAIRS, s // DIFF_TQ),
        in_specs=[
            pl.BlockSpec(memory_space=pltpu.SMEM),
            pl.BlockSpec(memory_space=pltpu.SMEM),
            pl.BlockSpec((1, 1, LANES, DIFF_TQ), lambda b, hp, qi: (b, hp, 0, qi)),
            pl.BlockSpec((1, s, LANES), lambda b, hp, qi: (b, 0, hp)),
            pl.BlockSpec((1, 1, 2, V_ROWS, s), lambda b, hp, qi: (b, hp, 0, 0, 0)),
            pl.BlockSpec((1, LANES), lambda b, hp, qi: (0, 0)),
        ],
        out_specs=pl.BlockSpec((1, DIFF_TQ, LANES), lambda b, hp, qi: (b, qi, hp)),
        out_shape=jax.ShapeDtypeStruct((bsz, s, BRANCH_W), BF16),
        scratch_shapes=[pltpu.VMEM((4, LANES, DIFF_TQ), BF16)] + table_scratch
                       + [pltpu.VMEM((4, V_ROWS, DIFF_TQ), F32)] + stage_scratch,
        compiler_params=_cparams(("arbitrary",) * 3),
        name=name,
    )(scalars, slopes2, qt, k3, vt, subln_pair).reshape(bsz * s, BRANCH_W)


def _silu(x):
    return x * jax.nn.sigmoid(x)


def _mix_kernel(x_ref, au_ref, av_ref, ag_ref, bg_ref, cg_ref, din_ref, db_ref, dc_ref, dg_ref,
                din_p_ref, dc_p_ref, din_n_ref, dc_n_ref,
                o1_ref, o4_ref, o16_ref, l1_ref, l4_ref, l16_ref, oc_ref,
                sgu_ref, ws_ref, bs_ref, cw_ref, wout_ref, y_ref, *stage_refs, tiles_per_seq, bounded):
    i = pl.program_id(0)
    t = x_ref.shape[0]

    av = av_ref[...].astype(F32)
    vn = (av * lax.rsqrt(jnp.mean(av * av, axis=-1, keepdims=True) + EPS) * sgu_ref[...]).astype(BF16)
    lane = lax.broadcasted_iota(jnp.int32, (CHUNK, LANES), 1)
    first_head = lane < HEAD_DIM
    chunks = []
    for c in range(t // CHUNK):
        vc = vn[c * CHUNK:(c + 1) * CHUNK, :]
        pairs = []
        for hp in range(HEAD_PAIRS):
            vp = vc[:, hp * LANES:(hp + 1) * LANES]
            r0 = jnp.dot(ws_ref[2 * hp], vp, preferred_element_type=F32)
            r1 = jnp.dot(ws_ref[2 * hp + 1], vp, preferred_element_type=F32)
            pairs.append(jnp.where(first_head, r0, r1))
        chunks.append(jnp.concatenate(pairs, axis=1) + bs_ref[...])
    out_a = au_ref[...].astype(F32) * jnp.concatenate(chunks, axis=0)

    if bounded:
        for stage, blk, dilation in zip(stage_refs, (o4_ref, o16_ref, l4_ref, l16_ref), (4, 16, 4, 16)):
            for r in range(dilation):
                for cb in range(HEAD_PAIRS):
                    stage[cb, pl.ds(r, t // dilation, stride=dilation), :] = (
                        blk[0, r, :, cb * LANES:(cb + 1) * LANES].astype(F32))

        def in_order(stage):
            return jnp.concatenate([stage[cb] for cb in range(HEAD_PAIRS)], axis=1)

        num = o1_ref[...].astype(F32) + in_order(stage_refs[0]) + in_order(stage_refs[1])
        den = l1_ref[...] + in_order(stage_refs[2]) + in_order(stage_refs[3])
        out_b = num / den
    else:
        l1, l4, l16 = l1_ref[...], l4_ref[...], l16_ref[...]
        lm = jnp.maximum(jnp.maximum(l1, l4), l16)
        w1, w4, w16 = jnp.exp2(l1 - lm), jnp.exp2(l4 - lm), jnp.exp2(l16 - lm)
        out_b = ((w1 * o1_ref[...].astype(F32) + w4 * o4_ref[...].astype(F32) + w16 * o16_ref[...].astype(F32))
                 / (w1 + w4 + w16))

    z = dc_ref[...].astype(F32) * din_ref[...].astype(F32)
    seq_pos = i % tiles_per_seq
    z_prev = (dc_p_ref[...].astype(F32) * din_p_ref[...].astype(F32))[HALO - 1:HALO, :]
    z_next = (dc_n_ref[...].astype(F32) * din_n_ref[...].astype(F32))[0:1, :]
    z_prev = jnp.where(seq_pos == 0, 0.0, z_prev)
    z_next = jnp.where(seq_pos == tiles_per_seq - 1, 0.0, z_next)
    row = lax.broadcasted_iota(jnp.int32, z.shape, 0)
    z_up = jnp.where(row == 0, z_prev, pltpu.roll(z, 1, axis=0))
    z_dn = jnp.where(row == t - 1, z_next, pltpu.roll(z, t - 1, axis=0))
    cw = cw_ref[...]
    out_d = db_ref[...].astype(F32) * (cw[0:1] * z_up + cw[1:2] * z + cw[2:3] * z_dn)

    mixed = jnp.concatenate([
        (_silu(ag_ref[...].astype(F32)) * out_a).astype(BF16),
        (_silu(bg_ref[...].astype(F32)) * out_b).astype(BF16),
        (_silu(cg_ref[...].astype(F32)) * oc_ref[...].astype(F32)).astype(BF16),
        (_silu(dg_ref[...].astype(F32)) * out_d).astype(BF16)], axis=1)
    y_ref[...] = x_ref[...] + jnp.dot(mixed, wout_ref[...], preferred_element_type=F32)


def _mix_out(bounded, x2, proj, dil, oc, sgu_g, ws_bf, bs_tab, conv_w, wout_bf, s):
    n = x2.shape[0]
    tiles_per_seq = s // MIX_T
    halo_per_tile = MIX_T // HALO
    n_halo = n // HALO

    def piece(p):
        return pl.BlockSpec((MIX_T, BRANCH_W), lambda i, p=p: (i, p))

    def halo_prev(p):
        return pl.BlockSpec((HALO, BRANCH_W), lambda i, p=p: (jnp.maximum(i * halo_per_tile - 1, 0), p))

    def halo_next(p):
        return pl.BlockSpec((HALO, BRANCH_W),
                            lambda i, p=p: (jnp.minimum((i + 1) * halo_per_tile, n_halo - 1), p))

    def tokens():
        return pl.BlockSpec((MIX_T, BRANCH_W), lambda i: (i, 0))

    def const(shape):
        return pl.BlockSpec(shape, lambda i: (0,) * len(shape))

    def residues(d):
        return pl.BlockSpec((1, d, MIX_T // d, BRANCH_W),
                            lambda i: (i // tiles_per_seq, 0, i % tiles_per_seq, 0))

    (o1, l1), (o4, l4), (o16, l16) = dil
    if bounded:
        dil_specs = [tokens(), residues(4), residues(16)] * 2
        scratch = [pltpu.VMEM((HEAD_PAIRS, MIX_T, LANES), F32)] * 4
    else:
        dil_specs = [tokens()] * 6
        scratch = []
    return pl.pallas_call(
        functools.partial(_mix_kernel, tiles_per_seq=tiles_per_seq, bounded=bounded),
        grid=(n // MIX_T,),
        in_specs=[pl.BlockSpec((MIX_T, D_MODEL), lambda i: (i, 0)),
                  piece(P_AU), piece(P_AV), piece(P_AG), piece(P_BG), piece(P_CG),
                  piece(P_DIN), piece(P_DB), piece(P_DC), piece(P_DG),
                  halo_prev(P_DIN), halo_prev(P_DC), halo_next(P_DIN), halo_next(P_DC),
                  *dil_specs, tokens(),
                  const((1, BRANCH_W)), const((GROUP_HEADS, CHUNK, CHUNK)), const((CHUNK, BRANCH_W)),
                  const((3, BRANCH_W)), const((MIX_W, D_MODEL))],
        out_specs=pl.BlockSpec((MIX_T, D_MODEL), lambda i: (i, 0)),
        out_shape=jax.ShapeDtypeStruct((n, D_MODEL), F32),
        scratch_shapes=scratch,
        compiler_params=_cparams(("arbitrary",)),
        name="mixout_bounded" if bounded else "mixout",
    )(x2, *([proj] * 13), o1, o4, o16, l1, l4, l16, oc,
      sgu_g.reshape(1, BRANCH_W), ws_bf, bs_tab, conv_w, wout_bf)


def _layer(x2, bsz, s, l, norm_g, w_in, sgu_g, w_s, b_s, qn_b, kn_b, qn_c, kn_c,
           lam_q1, lam_k1, lam_q2, lam_k2, subln_g, conv_w, w_out):
    slopes2 = (2.0 ** (-8.0 * jnp.arange(1, GROUP_HEADS + 1, dtype=F32) / GROUP_HEADS)) * LOG2E
    lambda_init = 0.8 - 0.6 * math.exp(-0.3 * l)
    lam = (jnp.exp(jnp.sum(lam_q1 * lam_k1)) - jnp.exp(jnp.sum(lam_q2 * lam_k2)) + lambda_init).reshape(1)

    proj = _inproj(x2, norm_g, w_in.astype(BF16))
    gqb = (jnp.tile(qn_b, GROUP_HEADS) * (LOG2E / math.sqrt(HEAD_DIM))).reshape(1, BRANCH_W)
    gkb = jnp.tile(kn_b, GROUP_HEADS).reshape(1, BRANCH_W)
    gqc = (jnp.tile(qn_c, 2 * GROUP_HEADS) * (LOG2E / math.sqrt(DIFF_QK_DIM))).reshape(1, BRANCH_W)
    gkc = jnp.tile(kn_c, 2 * GROUP_HEADS).reshape(1, BRANCH_W)
    bqn, bkn, kcn, qt, vt, q4, k4, v4, q16, k16, v16 = _prep(proj, bsz, s, gqb, gkb, gqc, gkc)

    bound_b = (jnp.max(jnp.abs(qn_b)) * jnp.max(jnp.abs(kn_b)) * (LOG2E * math.sqrt(HEAD_DIM))).reshape(1)
    bound_c = (jnp.max(jnp.abs(qn_c)) * jnp.max(jnp.abs(kn_c)) * (LOG2E * math.sqrt(DIFF_QK_DIM))).reshape(1)
    subln_pair = jnp.tile(subln_g, 2).reshape(1, LANES)
    bs_tab = jnp.repeat(b_s.T, HEAD_DIM, axis=1)
    ws_bf, wout_bf = w_s.astype(BF16), w_out.astype(BF16)

    def fixed_shift_path():
        operands = {1: (bqn.reshape(bsz, 1, s, BRANCH_W), bkn.reshape(bsz, 1, s, BRANCH_W),
                        proj.reshape(bsz, 1, s, PROJ_W), P_BV * HEAD_PAIRS),
                    4: (q4, k4, v4, 0), 16: (q16, k16, v16, 0)}
        dil = []
        for _, d in DIL_PATTERNS:
            qd, kd, vd, v_off = operands[d]
            num, den = _dilated_bounded(_dilated_tables(slopes2, bound_b[0], d), qd, kd, vd, v_off, bsz, s, d)
            if d == 1:
                num, den = num.reshape(bsz * s, BRANCH_W), den.reshape(bsz * s, BRANCH_W)
            dil.append((num, den))
        oc = _diff(True, lam, bound_c, slopes2, qt, kcn, vt, subln_pair, bsz, s, lambda_init)
        return _mix_out(True, x2, proj, dil, oc, sgu_g, ws_bf, bs_tab, conv_w, wout_bf, s)

    def running_max_path():
        dil = [_dilated(slopes2, bqn, bkn, proj, bsz, s, d) for (_, d) in DIL_PATTERNS]
        oc = _diff(False, lam, bound_c, slopes2, qt, kcn, vt, subln_pair, bsz, s, lambda_init)
        return _mix_out(False, x2, proj, dil, oc, sgu_g, ws_bf, bs_tab, conv_w, wout_bf, s)

    use_fixed_shift = jnp.maximum(bound_b[0], bound_c[0]) <= MAX_FIXED_SHIFT
    return lax.cond(use_fixed_shift, fixed_shift_path, running_max_path)


def _trunk(x, params):
    bsz, s, _ = x.shape
    x2 = x.reshape(bsz * s, D_MODEL)
    depth = params[0].shape[0]
    for l in range(depth):
        x2 = _layer(x2, bsz, s, l, *[p[l] for p in params])
    return x2.reshape(bsz, s, D_MODEL)


def kernel(x_prompt, x_sample, norm_g, w_in, sgu_g, w_s, b_s, qn_b, kn_b, qn_c, kn_c,
           lam_q1, lam_k1, lam_q2, lam_k2, subln_g, conv_w, w_out):
    params = (norm_g, w_in, sgu_g, w_s, b_s, qn_b, kn_b, qn_c, kn_c,
              lam_q1, lam_k1, lam_q2, lam_k2, subln_g, conv_w, w_out)
    return (_trunk(x_prompt, params), _trunk(x_sample, params))
```

```python
import functools
import math

import jax
import jax.numpy as jnp
from jax import lax
from jax.experimental import pallas as pl
from jax.experimental.pallas import tpu as pltpu

F32 = jnp.float32
BF16 = jnp.bfloat16

D_MODEL = 1024
HEAD_DIM = 64
GROUP_HEADS = 6
BRANCH_W = GROUP_HEADS * HEAD_DIM
N_PIECES = 15
PROJ_W = N_PIECES * BRANCH_W
MIX_W = 4 * BRANCH_W
CHUNK = 128
DIL_PATTERNS = ((128, 1), (512, 4), (2048, 16))
N_SIDE = 64
DIFF_QK_DIM = HEAD_DIM // 2
EPS = 1e-6
NEG = -1e30
LOG2E = math.log2(math.e)

LANES = 128
HEAD_PAIRS = BRANCH_W // LANES
V_ROWS = 80

P_AU, P_AV, P_AG, P_BQ, P_BK, P_BV, P_BG, P_CQ, P_CK, P_CV, P_CG, P_DIN, P_DB, P_DC, P_DG = range(15)

IN_TM = 1024
IN_TN = 1920
PREP_T = 512
MIX_T = 512
HALO = 16
DIL_TQ = 128
DIL_WIN = DIL_TQ + 2 * N_SIDE
DIL_ROWS = 2048
DIL_GROUP = 8
DIFF_TQ = 256
DIFF_TQ_FIXED = 512
DIFF_TK = 256
VMEM_LIMIT = 56 * 1024 * 1024
MAX_FIXED_SHIFT = 40.0


def _cparams(sem):
    return pltpu.CompilerParams(dimension_semantics=sem, vmem_limit_bytes=VMEM_LIMIT)


def _inproj_kernel(x_ref, g_ref, w_ref, o_ref):
    x = x_ref[...]
    ms = jnp.mean(x * x, axis=-1, keepdims=True)
    h = (x * lax.rsqrt(ms + EPS) * g_ref[...]).astype(BF16)
    o_ref[...] = jnp.dot(h, w_ref[...], preferred_element_type=F32).astype(BF16)


def _inproj(x2, norm_g, w_in_bf):
    n = x2.shape[0]
    return pl.pallas_call(
        _inproj_kernel,
        grid=(PROJ_W // IN_TN, n // IN_TM),
        in_specs=[
            pl.BlockSpec((IN_TM, D_MODEL), lambda j, i: (i, 0)),
            pl.BlockSpec((1, D_MODEL), lambda j, i: (0, 0)),
            pl.BlockSpec((D_MODEL, IN_TN), lambda j, i: (0, j)),
        ],
        out_specs=pl.BlockSpec((IN_TM, IN_TN), lambda j, i: (i, j)),
        out_shape=jax.ShapeDtypeStruct((n, PROJ_W), BF16),
        compiler_params=_cparams(("arbitrary", "arbitrary")),
        name="inproj",
    )(x2, norm_g.reshape(1, D_MODEL), w_in_bf)


def _group_rms(x, e_ref, width):
    x2 = x * x
    hi = x2.astype(BF16)
    lo = (x2 - hi.astype(F32)).astype(BF16)
    e = e_ref[...]
    ss = jnp.dot(hi, e, preferred_element_type=F32) + jnp.dot(lo, e, preferred_element_type=F32)
    return x * lax.rsqrt(ss * (1.0 / width) + EPS)


def _prep_kernel(bq_ref, bk_ref, bv_ref, cq_ref, ck_ref, cv_ref, gqb_ref, gkb_ref, gqc_ref, gkc_ref,
                 e64_ref, e32_ref, bqn_ref, bkn_ref, kcn_ref, qt_ref, vt_ref,
                 q4_ref, k4_ref, v4_ref, q16_ref, k16_ref, v16_ref, sq_ref, sk_ref, sv_ref):
    t = bq_ref.shape[0]
    row = lax.broadcasted_iota(jnp.int32, (V_ROWS - HEAD_DIM, t), 0)
    ones_pad = jnp.where(row == 0, 1.0, 0.0).astype(BF16)
    for cb in range(HEAD_PAIRS):
        sl = slice(cb * LANES, (cb + 1) * LANES)
        sq_ref[cb] = _group_rms(bq_ref[:, sl].astype(F32), e64_ref, HEAD_DIM) * gqb_ref[:, sl]
        sk_ref[cb] = _group_rms(bk_ref[:, sl].astype(F32), e64_ref, HEAD_DIM) * gkb_ref[:, sl]
        sv_ref[cb] = bv_ref[:, sl].astype(F32)
        bqn_ref[:, sl] = sq_ref[cb].astype(BF16)
        bkn_ref[:, sl] = sk_ref[cb].astype(BF16)
        kcn_ref[:, sl] = (_group_rms(ck_ref[:, sl].astype(F32), e32_ref, DIFF_QK_DIM) * gkc_ref[:, sl]).astype(BF16)
        qn = _group_rms(cq_ref[:, sl].astype(F32), e32_ref, DIFF_QK_DIM) * gqc_ref[:, sl]
        qt_ref[0, cb] = qn.T.astype(BF16)
        vt = cv_ref[:, sl].astype(F32).T.astype(BF16)
        for hh in range(2):
            vt_ref[0, cb, hh, 0:HEAD_DIM, :] = vt[hh * HEAD_DIM:(hh + 1) * HEAD_DIM, :]
            vt_ref[0, cb, hh, HEAD_DIM:V_ROWS, :] = ones_pad
    for stage, out4, out16 in ((sq_ref, q4_ref, q16_ref), (sk_ref, k4_ref, k16_ref), (sv_ref, v4_ref, v16_ref)):
        for dilation, out in ((4, out4), (16, out16)):
            for r in range(dilation):
                for cb in range(HEAD_PAIRS):
                    out[0, r, :, cb * LANES:(cb + 1) * LANES] = (
                        stage[cb, pl.ds(r, t // dilation, stride=dilation), :].astype(BF16))


def _block_diag_ones(group):
    idx = jnp.arange(LANES) // group
    return (idx[:, None] == idx[None, :]).astype(BF16)


def _prep(proj, bsz, s, gqb, gkb, gqc, gkc):
    n = proj.shape[0]
    tiles_per_seq = s // PREP_T

    def piece(p):
        return pl.BlockSpec((PREP_T, BRANCH_W), lambda i, p=p: (i, p))

    def vec():
        return pl.BlockSpec((1, BRANCH_W), lambda i: (0, 0))

    def tokens():
        return pl.BlockSpec((PREP_T, BRANCH_W), lambda i: (i, 0))

    def residues(d):
        return pl.BlockSpec((1, d, PREP_T // d, BRANCH_W),
                            lambda i: (i // tiles_per_seq, 0, i % tiles_per_seq, 0))

    return pl.pallas_call(
        _prep_kernel,
        grid=(n // PREP_T,),
        in_specs=[piece(P_BQ), piece(P_BK), piece(P_BV), piece(P_CQ), piece(P_CK), piece(P_CV),
                  vec(), vec(), vec(), vec(),
                  pl.BlockSpec((LANES, LANES), lambda i: (0, 0)),
                  pl.BlockSpec((LANES, LANES), lambda i: (0, 0))],
        out_specs=[tokens(), tokens(), tokens(),
                   pl.BlockSpec((1, HEAD_PAIRS, LANES, PREP_T),
                                lambda i: (i // tiles_per_seq, 0, 0, i % tiles_per_seq)),
                   pl.BlockSpec((1, HEAD_PAIRS, 2, V_ROWS, PREP_T),
                                lambda i: (i // tiles_per_seq, 0, 0, 0, i % tiles_per_seq))]
                  + [residues(d) for d in (4, 4, 4, 16, 16, 16)],
        out_shape=[jax.ShapeDtypeStruct((n, BRANCH_W), BF16),
                   jax.ShapeDtypeStruct((n, BRANCH_W), BF16),
                   jax.ShapeDtypeStruct((n, BRANCH_W), BF16),
                   jax.ShapeDtypeStruct((bsz, HEAD_PAIRS, LANES, s), BF16),
                   jax.ShapeDtypeStruct((bsz, HEAD_PAIRS, 2, V_ROWS, s), BF16)]
                  + [jax.ShapeDtypeStruct((bsz, d, s // d, BRANCH_W), BF16) for d in (4, 4, 4, 16, 16, 16)],
        scratch_shapes=[pltpu.VMEM((HEAD_PAIRS, PREP_T, LANES), F32)] * 3,
        compiler_params=_cparams(("arbitrary",)),
        name="prep",
    )(proj, proj, proj, proj, proj, proj, gqb, gkb, gqc, gkc,
      _block_diag_ones(HEAD_DIM), _block_diag_ones(DIFF_QK_DIM))


def _dilated_kernel(slope_ref, q_ref, k_ref, v_ref, o_ref, lse_ref, *, dilation, seq_sub):
    hp = pl.program_id(2)
    qi = pl.program_id(3)
    tqb = q_ref.shape[1]
    lane = lax.broadcasted_iota(jnp.int32, (DIL_TQ, LANES), 1)
    first_head = lane < HEAD_DIM
    rel0 = (lax.broadcasted_iota(jnp.int32, (DIL_TQ, DIL_WIN), 1)
            - lax.broadcasted_iota(jnp.int32, (DIL_TQ, DIL_WIN), 0))

    def sub_block(sb, carry):
        r0 = pl.multiple_of(sb * DIL_TQ, DIL_TQ)
        i0 = qi * tqb + r0
        ws = pl.multiple_of(jnp.clip(i0 - N_SIDE, 0, seq_sub - DIL_WIN), N_SIDE)
        kw = k_ref[0, pl.ds(ws, DIL_WIN), :]
        vw = v_ref[0, pl.ds(ws, DIL_WIN), :]
        q = q_ref[0, pl.ds(r0, DIL_TQ), :]
        dist = jnp.abs(rel0 + (ws - i0))
        valid = dist <= N_SIDE
        distf = dist.astype(F32)
        outs, lses = [], []
        for hh in range(2):
            slope = slope_ref[2 * hp + hh] * float(dilation)
            qm = jnp.where(first_head if hh == 0 else jnp.logical_not(first_head), q, jnp.zeros_like(q))
            sc = lax.dot_general(qm, kw, (((1,), (1,)), ((), ())), preferred_element_type=F32)
            sc = jnp.where(valid, sc - slope * distf, NEG)
            m = jnp.max(sc, axis=-1, keepdims=True)
            p = jnp.exp2(sc - m)
            den = jnp.sum(p, axis=-1, keepdims=True)
            o = jnp.dot(p.astype(BF16), vw, preferred_element_type=F32) / den
            outs.append(o)
            lses.append(jnp.broadcast_to(m + jnp.log2(den), (DIL_TQ, LANES)))
        o_ref[0, pl.ds(r0, DIL_TQ), :] = jnp.where(first_head, outs[0], outs[1]).astype(o_ref.dtype)
        lse_ref[0, pl.ds(r0, DIL_TQ), :] = jnp.where(first_head, lses[0], lses[1])
        return carry

    lax.fori_loop(0, tqb // DIL_TQ, sub_block, 0)


def _dilated(slopes2, bqn, bkn, proj, bsz, s, dilation):
    seq_sub = s // dilation
    tqb = min(seq_sub, 1024)
    q3 = bqn.reshape(bsz, seq_sub, dilation * BRANCH_W)
    k3 = bkn.reshape(bsz, seq_sub, dilation * BRANCH_W)
    v3 = proj.reshape(bsz, seq_sub, dilation * PROJ_W)
    blocks_q = BRANCH_W // LANES
    blocks_p = PROJ_W // LANES
    v_off = P_BV * blocks_q

    def tile():
        return pl.BlockSpec((1, tqb, LANES), lambda b, r, hp, qi: (b, qi, r * blocks_q + hp))

    o, lse = pl.pallas_call(
        functools.partial(_dilated_kernel, dilation=dilation, seq_sub=seq_sub),
        grid=(bsz, dilation, HEAD_PAIRS, seq_sub // tqb),
        in_specs=[
            pl.BlockSpec(memory_space=pltpu.SMEM),
            tile(),
            pl.BlockSpec((1, seq_sub, LANES), lambda b, r, hp, qi: (b, 0, r * blocks_q + hp)),
            pl.BlockSpec((1, seq_sub, LANES), lambda b, r, hp, qi: (b, 0, r * blocks_p + v_off + hp)),
        ],
        out_specs=[tile(), tile()],
        out_shape=[jax.ShapeDtypeStruct(q3.shape, BF16), jax.ShapeDtypeStruct(q3.shape, F32)],
        compiler_params=_cparams(("arbitrary",) * 4),
        name=f"dilated{dilation}",
    )(slopes2, q3, k3, v3)
    return o.reshape(bsz * s, BRANCH_W), lse.reshape(bsz * s, BRANCH_W)


def _dilated_bounded_kernel(tab_ref, q_ref, k_ref, v_ref, num_ref, den_ref, p_ref, *, seq_sub):
    qi = pl.program_id(3)
    n_res, tqb = q_ref.shape[1], q_ref.shape[2]
    sub_per_res = tqb // DIL_TQ
    first_head_q = lax.broadcasted_iota(jnp.int32, (DIL_TQ, LANES), 1) < HEAD_DIM
    v_lane = lax.broadcasted_iota(jnp.int32, (DIL_WIN, 2 * LANES), 1)
    value_lanes = (v_lane < HEAD_DIM) | (v_lane >= 2 * LANES - HEAD_DIM)

    def window(sb):
        res = sb // sub_per_res
        r0 = pl.multiple_of((sb % sub_per_res) * DIL_TQ, DIL_TQ)
        i0 = qi * tqb + r0
        ws = pl.multiple_of(jnp.clip(i0 - N_SIDE, 0, seq_sub - DIL_WIN), N_SIDE)
        variant = jnp.where(i0 == 0, 0, jnp.where(i0 == seq_sub - DIL_TQ, 2, 1))
        return res, r0, ws, variant

    def probabilities(u):
        for half in range(DIL_GROUP):
            res, r0, ws, variant = window(DIL_GROUP * u + half)
            kw = k_ref[0, res, pl.ds(ws, DIL_WIN), :]
            q = q_ref[0, res, pl.ds(r0, DIL_TQ), :]
            zero = jnp.zeros_like(q)
            q2 = jnp.concatenate([jnp.where(first_head_q, q, zero), jnp.where(first_head_q, zero, q)], axis=0)
            sc = lax.dot_general(q2, kw, (((1,), (1,)), ((), ())), preferred_element_type=F32)
            for hh in range(2):
                p_ref[half, hh * DIL_TQ:(hh + 1) * DIL_TQ, :] = jnp.exp2(
                    sc[hh * DIL_TQ:(hh + 1) * DIL_TQ] + tab_ref[hh, variant]).astype(BF16)

    def outputs(u):
        for half in range(DIL_GROUP):
            res, r0, ws, _ = window(DIL_GROUP * u + half)
            vw = v_ref[0, res, pl.ds(ws, DIL_WIN), :]
            vw2 = jnp.concatenate([vw, vw], axis=1)
            vcat = jnp.where(value_lanes, vw2, jnp.ones_like(vw2))
            a = jnp.dot(p_ref[half], vcat, preferred_element_type=F32)
            a0 = a[0:DIL_TQ, 0:LANES]
            a1 = a[DIL_TQ:2 * DIL_TQ, LANES:2 * LANES]
            num_ref[0, res, pl.ds(r0, DIL_TQ), :] = jnp.where(first_head_q, a0, a1).astype(num_ref.dtype)
            den_ref[0, res, pl.ds(r0, DIL_TQ), :] = pltpu.roll(jnp.where(first_head_q, a1, a0), HEAD_DIM, axis=1)

    n_groups = n_res * sub_per_res // DIL_GROUP
    probabilities(0)

    def step(u, carry):
        outputs(u - 1)
        probabilities(u)
        return carry

    lax.fori_loop(1, n_groups, step, 0)
    outputs(n_groups - 1)


def _dilated_tables(slopes2, bound, dilation):
    rel = ((jnp.arange(DIL_WIN)[None, :] - jnp.arange(DIL_TQ)[:, None])[None]
           - jnp.array([0, N_SIDE, 2 * N_SIDE])[:, None, None])
    dist = jnp.abs(rel).astype(F32)
    bias = -(slopes2 * float(dilation))[:, None, None, None] * dist[None] - bound
    return jnp.where(dist[None] <= N_SIDE, bias, NEG).astype(F32)


def _dilated_bounded(tables, q4d, k4d, v4d, v_block_offset, bsz, s, dilation):
    seq_sub = s // dilation
    tqb = min(seq_sub, DIL_ROWS)
    n_res = min(dilation, DIL_ROWS // tqb)

    def tile():
        return pl.BlockSpec((1, n_res, tqb, LANES), lambda b, r, hp, qi: (b, r, qi, hp))

    out_shape = (bsz, dilation, seq_sub, BRANCH_W)
    return pl.pallas_call(
        functools.partial(_dilated_bounded_kernel, seq_sub=seq_sub),
        grid=(bsz, dilation // n_res, HEAD_PAIRS, seq_sub // tqb),
        in_specs=[
            pl.BlockSpec((2, 3, DIL_TQ, DIL_WIN), lambda b, r, hp, qi: (hp, 0, 0, 0)),
            tile(),
            pl.BlockSpec((1, n_res, seq_sub, LANES), lambda b, r, hp, qi: (b, r, 0, hp)),
            pl.BlockSpec((1, n_res, seq_sub, LANES), lambda b, r, hp, qi: (b, r, 0, v_block_offset + hp)),
        ],
        out_specs=[tile(), tile()],
        out_shape=[jax.ShapeDtypeStruct(out_shape, BF16), jax.ShapeDtypeStruct(out_shape, F32)],
        scratch_shapes=[pltpu.VMEM((DIL_GROUP, 2 * DIL_TQ, DIL_WIN), BF16)],
        compiler_params=_cparams(("arbitrary",) * 4),
        name=f"dilated_bounded{dilation}",
    )(tables, q4d, k4d, v4d)


def _diff_kernel(lam_ref, slope_ref, qt_ref, k_ref, vt_ref, g_ref, o_ref, qm_ref, tm_ref, acc_ref,
                 *, out_scale):
    hp = pl.program_id(1)
    qi = pl.program_id(2)
    tq = qt_ref.shape[3]
    tk = DIFF_TK
    nk = k_ref.shape[1] // tk
    i0 = qi * tq

    qt = qt_ref[0, 0]
    row_group = lax.broadcasted_iota(jnp.int32, (LANES, tq), 0) // DIFF_QK_DIM
    for g in range(4):
        qm_ref[g] = jnp.where(row_group == g, qt, jnp.zeros_like(qt))
    dkq = (lax.broadcasted_iota(jnp.int32, (tk, tq), 0)
           - lax.broadcasted_iota(jnp.int32, (tk, tq), 1)).astype(F32)
    for hh in range(2):
        tm_ref[hh] = slope_ref[2 * hp + hh] * dkq
    acc_ref[...] = jnp.zeros_like(acc_ref)

    def key_tile(j, ms, mode):
        koff = pl.multiple_of(j * tk, tk)
        kt = k_ref[0, pl.ds(koff, tk), :]
        tile_dist = jnp.abs(i0 - koff).astype(F32)
        new_ms = []
        for hh in range(2):
            vt = vt_ref[0, 0, hh, :, pl.ds(koff, tk)]
            shift = slope_ref[2 * hp + hh] * tile_dist
            for c in range(2):
                g = 2 * hh + c
                sc = jnp.dot(kt, qm_ref[g], preferred_element_type=F32)
                if mode < 0:
                    sc = sc + tm_ref[hh]
                elif mode > 0:
                    sc = sc - tm_ref[hh]
                else:
                    sc = sc - jnp.abs(tm_ref[hh])
                m_old = ms[g]
                m_new = jnp.maximum(m_old, jnp.max(sc, axis=0, keepdims=True) - shift)
                p = jnp.exp2(sc - (m_new + shift))
                alpha = jnp.exp2(m_old - m_new)
                acc_ref[g] = alpha * acc_ref[g] + jnp.dot(vt, p.astype(BF16), preferred_element_type=F32)
                new_ms.append(m_new)
        return tuple(new_ms)

    ms = tuple(jnp.full((1, tq), NEG, F32) for _ in range(4))
    ms = lax.fori_loop(0, qi, lambda j, m: key_tile(j, m, -1), ms)
    ms = key_tile(qi, ms, 0)
    lax.fori_loop(qi + 1, nk, lambda j, m: key_tile(j, m, 1), ms)
    _diff_finish(lam_ref, g_ref, o_ref, acc_ref, out_scale)


def _diff_bounded_kernel(lam_ref, slope_ref, qt_ref, k_ref, vt_ref, g_ref, o_ref, qm_ref, tm_ref, acc_ref,
                         p_ref, *, out_scale):
    hp = pl.program_id(1)
    qi = pl.program_id(2)
    tq = qt_ref.shape[3]
    tk = DIFF_TK
    tiles_per_q = tq // tk
    nk = k_ref.shape[1] // tk
    i0 = qi * tq
    bound = lam_ref[1]

    qt = qt_ref[0, 0]
    row_group = lax.broadcasted_iota(jnp.int32, (LANES, tq), 0) // DIFF_QK_DIM
    for g in range(4):
        qm_ref[g] = jnp.where(row_group == g, qt, jnp.zeros_like(qt))
    @pl.when(qi == 0)
    def _():
        dkq = (lax.broadcasted_iota(jnp.int32, (tk, tq), 0)
               - lax.broadcasted_iota(jnp.int32, (tk, tq), 1)).astype(F32)
        for hh in range(2):
            slope = slope_ref[2 * hp + hh]
            tm = slope * dkq
            tm_ref[hh, 0] = tm
            tm_ref[hh, 1] = -tm
            for r in range(tiles_per_q):
                origin = float(r * tk)
                tm_ref[hh, 2 + r] = slope * (origin - jnp.abs(dkq + origin))

    acc_ref[...] = jnp.zeros_like(acc_ref)

    def stage(scores=None, values=None):
        for hh in range(2):
            slope = slope_ref[2 * hp + hh]
            for c in range(2):
                g = 2 * hh + c
                parts = []
                for half in range(2):
                    if values is not None:
                        u, slot = values
                        koff = pl.multiple_of((2 * u + half) * tk, tk)
                        parts.append(jnp.dot(vt_ref[0, 0, hh, :, pl.ds(koff, tk)], p_ref[slot, half, g],
                                             preferred_element_type=F32))
                    if scores is not None:
                        u, slot = scores
                        j = 2 * u + half
                        koff = pl.multiple_of(j * tk, tk)
                        shift = -(slope * jnp.abs(i0 - koff).astype(F32) + bound)
                        first = tiles_per_q * qi
                        table = jnp.where(j < first, 0, jnp.where(j >= first + tiles_per_q, 1, 2 + j - first))
                        sc = jnp.dot(k_ref[0, pl.ds(koff, tk), :], qm_ref[g], preferred_element_type=F32)
                        p_ref[slot, half, g] = jnp.exp2(sc + tm_ref[hh, table] + shift).astype(BF16)
                if parts:
                    acc_ref[g] += parts[0] + parts[1]

    n_pairs = nk // 2
    stage(scores=(0, 0))

    def trip(w, carry):
        stage(scores=(2 * w + 1, 1), values=(2 * w, 0))
        stage(scores=(2 * w + 2, 0), values=(2 * w + 1, 1))
        return carry

    lax.fori_loop(0, n_pairs // 2 - 1, trip, 0)
    stage(scores=(n_pairs - 1, 1), values=(n_pairs - 2, 0))
    stage(values=(n_pairs - 1, 1))

    _diff_finish(lam_ref, g_ref, o_ref, acc_ref, out_scale)


def _diff_finish(lam_ref, g_ref, o_ref, acc_ref, out_scale):
    lam = lam_ref[0]
    heads = []
    for hh in range(2):
        a0 = acc_ref[2 * hh]
        a1 = acc_ref[2 * hh + 1]
        o = (a0[0:HEAD_DIM] / a0[HEAD_DIM:HEAD_DIM + 1]
             - lam * (a1[0:HEAD_DIM] / a1[HEAD_DIM:HEAD_DIM + 1]))
        ms2 = jnp.mean(o * o, axis=0, keepdims=True)
        heads.append(o * lax.rsqrt(ms2 + EPS))
    out = jnp.concatenate(heads, axis=0).T
    o_ref[0] = (out * (g_ref[...] * out_scale)).astype(o_ref.dtype)


def _diff(bounded, lam, score_bound, slopes2, qt, kcn, vt, subln_pair, bsz, s, lambda_init):
    k3 = kcn.reshape(bsz, s, BRANCH_W)
    scalars = jnp.concatenate([lam, score_bound]).astype(F32)
    if bounded:
        body, name, tq = _diff_bounded_kernel, "diff_bounded", DIFF_TQ_FIXED
        table_scratch = [pltpu.VMEM((2, 2 + tq // DIFF_TK, DIFF_TK, tq), F32)]
        stage_scratch = [pltpu.VMEM((2, 2, 4, DIFF_TK, tq), BF16)]
    else:
        body, name, tq = _diff_kernel, "diff_online", DIFF_TQ
        table_scratch = [pltpu.VMEM((2, DIFF_TK, tq), F32)]
        stage_scratch = []
    return pl.pallas_call(
        functools.partial(body, out_scale=1.0 - lambda_init),
        grid=(bsz, HEAD_PAIRS, s // tq),
        in_specs=[
            pl.BlockSpec(memory_space=pltpu.SMEM),
            pl.BlockSpec(memory_space=pltpu.SMEM),
            pl.BlockSpec((1, 1, LANES, tq), lambda b, hp, qi: (b, hp, 0, qi)),
            pl.BlockSpec((1, s, LANES), lambda b, hp, qi: (b, 0, hp)),
            pl.BlockSpec((1, 1, 2, V_ROWS, s), lambda b, hp, qi: (b, hp, 0, 0, 0)),
            pl.BlockSpec((1, LANES), lambda b, hp, qi: (0, 0)),
        ],
        out_specs=pl.BlockSpec((1, tq, LANES), lambda b, hp, qi: (b, qi, hp)),
        out_shape=jax.ShapeDtypeStruct((bsz, s, BRANCH_W), BF16),
        scratch_shapes=[pltpu.VMEM((4, LANES, tq), BF16)] + table_scratch
                       + [pltpu.VMEM((4, V_ROWS, tq), F32)] + stage_scratch,
        compiler_params=_cparams(("arbitrary",) * 3),
        name=name,
    )(scalars, slopes2, qt, k3, vt, subln_pair).reshape(bsz * s, BRANCH_W)


def _silu(x):
    return x * jax.nn.sigmoid(x)


def _mix_kernel(x_ref, au_ref, av_ref, ag_ref, bg_ref, cg_ref, din_ref, db_ref, dc_ref, dg_ref,
                din_p_ref, dc_p_ref, din_n_ref, dc_n_ref,
                o1_ref, o4_ref, o16_ref, l1_ref, l4_ref, l16_ref, oc_ref,
                sgu_ref, ws_ref, bs_ref, cw_ref, wout_ref, y_ref, *stage_refs, tiles_per_seq, bounded):
    i = pl.program_id(0)
    t = x_ref.shape[0]

    av = av_ref[...].astype(F32)
    vn = (av * lax.rsqrt(jnp.mean(av * av, axis=-1, keepdims=True) + EPS) * sgu_ref[...]).astype(BF16)
    lane = lax.broadcasted_iota(jnp.int32, (CHUNK, LANES), 1)
    first_head = lane < HEAD_DIM
    chunks = []
    for c in range(t // CHUNK):
        vc = vn[c * CHUNK:(c + 1) * CHUNK, :]
        pairs = []
        for hp in range(HEAD_PAIRS):
            vp = vc[:, hp * LANES:(hp + 1) * LANES]
            r0 = jnp.dot(ws_ref[2 * hp], vp, preferred_element_type=F32)
            r1 = jnp.dot(ws_ref[2 * hp + 1], vp, preferred_element_type=F32)
            pairs.append(jnp.where(first_head, r0, r1))
        chunks.append(jnp.concatenate(pairs, axis=1) + bs_ref[...])
    out_a = au_ref[...].astype(F32) * jnp.concatenate(chunks, axis=0)

    if bounded:
        for stage, blk, dilation in zip(stage_refs, (o4_ref, o16_ref, l4_ref, l16_ref), (4, 16, 4, 16)):
            for r in range(dilation):
                for cb in range(HEAD_PAIRS):
                    stage[cb, pl.ds(r, t // dilation, stride=dilation), :] = (
                        blk[0, r, :, cb * LANES:(cb + 1) * LANES].astype(F32))

        def in_order(stage):
            return jnp.concatenate([stage[cb] for cb in range(HEAD_PAIRS)], axis=1)

        num = o1_ref[...].astype(F32) + in_order(stage_refs[0]) + in_order(stage_refs[1])
        den = l1_ref[...] + in_order(stage_refs[2]) + in_order(stage_refs[3])
        out_b = num / den
    else:
        l1, l4, l16 = l1_ref[...], l4_ref[...], l16_ref[...]
        lm = jnp.maximum(jnp.maximum(l1, l4), l16)
        w1, w4, w16 = jnp.exp2(l1 - lm), jnp.exp2(l4 - lm), jnp.exp2(l16 - lm)
        out_b = ((w1 * o1_ref[...].astype(F32) + w4 * o4_ref[...].astype(F32) + w16 * o16_ref[...].astype(F32))
                 / (w1 + w4 + w16))

    z = dc_ref[...].astype(F32) * din_ref[...].astype(F32)
    seq_pos = i % tiles_per_seq
    z_prev = (dc_p_ref[...].astype(F32) * din_p_ref[...].astype(F32))[HALO - 1:HALO, :]
    z_next = (dc_n_ref[...].astype(F32) * din_n_ref[...].astype(F32))[0:1, :]
    z_prev = jnp.where(seq_pos == 0, 0.0, z_prev)
    z_next = jnp.where(seq_pos == tiles_per_seq - 1, 0.0, z_next)
    row = lax.broadcasted_iota(jnp.int32, z.shape, 0)
    z_up = jnp.where(row == 0, z_prev, pltpu.roll(z, 1, axis=0))
    z_dn = jnp.where(row == t - 1, z_next, pltpu.roll(z, t - 1, axis=0))
    cw = cw_ref[...]
    out_d = db_ref[...].astype(F32) * (cw[0:1] * z_up + cw[1:2] * z + cw[2:3] * z_dn)

    mixed = jnp.concatenate([
        (_silu(ag_ref[...].astype(F32)) * out_a).astype(BF16),
        (_silu(bg_ref[...].astype(F32)) * out_b).astype(BF16),
        (_silu(cg_ref[...].astype(F32)) * oc_ref[...].astype(F32)).astype(BF16),
        (_silu(dg_ref[...].astype(F32)) * out_d).astype(BF16)], axis=1)
    y_ref[...] = x_ref[...] + jnp.dot(mixed, wout_ref[...], preferred_element_type=F32)


def _mix_out(bounded, x2, proj, dil, oc, sgu_g, ws_bf, bs_tab, conv_w, wout_bf, s):
    n = x2.shape[0]
    tiles_per_seq = s // MIX_T
    halo_per_tile = MIX_T // HALO
    n_halo = n // HALO

    def piece(p):
        return pl.BlockSpec((MIX_T, BRANCH_W), lambda i, p=p: (i, p))

    def halo_prev(p):
        return pl.BlockSpec((HALO, BRANCH_W), lambda i, p=p: (jnp.maximum(i * halo_per_tile - 1, 0), p))

    def halo_next(p):
        return pl.BlockSpec((HALO, BRANCH_W),
                            lambda i, p=p: (jnp.minimum((i + 1) * halo_per_tile, n_halo - 1), p))

    def tokens():
        return pl.BlockSpec((MIX_T, BRANCH_W), lambda i: (i, 0))

    def const(shape):
        return pl.BlockSpec(shape, lambda i: (0,) * len(shape))

    def residues(d):
        return pl.BlockSpec((1, d, MIX_T // d, BRANCH_W),
                            lambda i: (i // tiles_per_seq, 0, i % tiles_per_seq, 0))

    (o1, l1), (o4, l4), (o16, l16) = dil
    if bounded:
        dil_specs = [tokens(), residues(4), residues(16)] * 2
        scratch = [pltpu.VMEM((HEAD_PAIRS, MIX_T, LANES), F32)] * 4
    else:
        dil_specs = [tokens()] * 6
        scratch = []
    return pl.pallas_call(
        functools.partial(_mix_kernel, tiles_per_seq=tiles_per_seq, bounded=bounded),
        grid=(n // MIX_T,),
        in_specs=[pl.BlockSpec((MIX_T, D_MODEL), lambda i: (i, 0)),
                  piece(P_AU), piece(P_AV), piece(P_AG), piece(P_BG), piece(P_CG),
                  piece(P_DIN), piece(P_DB), piece(P_DC), piece(P_DG),
                  halo_prev(P_DIN), halo_prev(P_DC), halo_next(P_DIN), halo_next(P_DC),
                  *dil_specs, tokens(),
                  const((1, BRANCH_W)), const((GROUP_HEADS, CHUNK, CHUNK)), const((CHUNK, BRANCH_W)),
                  const((3, BRANCH_W)), const((MIX_W, D_MODEL))],
        out_specs=pl.BlockSpec((MIX_T, D_MODEL), lambda i: (i, 0)),
        out_shape=jax.ShapeDtypeStruct((n, D_MODEL), F32),
        scratch_shapes=scratch,
        compiler_params=_cparams(("arbitrary",)),
        name="mixout_bounded" if bounded else "mixout",
    )(x2, *([proj] * 13), o1, o4, o16, l1, l4, l16, oc,
      sgu_g.reshape(1, BRANCH_W), ws_bf, bs_tab, conv_w, wout_bf)


def _layer(x2, bsz, s, l, norm_g, w_in, sgu_g, w_s, b_s, qn_b, kn_b, qn_c, kn_c,
           lam_q1, lam_k1, lam_q2, lam_k2, subln_g, conv_w, w_out):
    slopes2 = (2.0 ** (-8.0 * jnp.arange(1, GROUP_HEADS + 1, dtype=F32) / GROUP_HEADS)) * LOG2E
    lambda_init = 0.8 - 0.6 * math.exp(-0.3 * l)
    lam = (jnp.exp(jnp.sum(lam_q1 * lam_k1)) - jnp.exp(jnp.sum(lam_q2 * lam_k2)) + lambda_init).reshape(1)

    proj = _inproj(x2, norm_g, w_in.astype(BF16))
    gqb = (jnp.tile(qn_b, GROUP_HEADS) * (LOG2E / math.sqrt(HEAD_DIM))).reshape(1, BRANCH_W)
    gkb = jnp.tile(kn_b, GROUP_HEADS).reshape(1, BRANCH_W)
    gqc = (jnp.tile(qn_c, 2 * GROUP_HEADS) * (LOG2E / math.sqrt(DIFF_QK_DIM))).reshape(1, BRANCH_W)
    gkc = jnp.tile(kn_c, 2 * GROUP_HEADS).reshape(1, BRANCH_W)
    bqn, bkn, kcn, qt, vt, q4, k4, v4, q16, k16, v16 = _prep(proj, bsz, s, gqb, gkb, gqc, gkc)

    bound_b = (jnp.max(jnp.abs(qn_b)) * jnp.max(jnp.abs(kn_b)) * (LOG2E * math.sqrt(HEAD_DIM))).reshape(1)
    bound_c = (jnp.max(jnp.abs(qn_c)) * jnp.max(jnp.abs(kn_c)) * (LOG2E * math.sqrt(DIFF_QK_DIM))).reshape(1)
    subln_pair = jnp.tile(subln_g, 2).reshape(1, LANES)
    bs_tab = jnp.repeat(b_s.T, HEAD_DIM, axis=1)
    ws_bf, wout_bf = w_s.astype(BF16), w_out.astype(BF16)

    def fixed_shift_path():
        operands = {1: (bqn.reshape(bsz, 1, s, BRANCH_W), bkn.reshape(bsz, 1, s, BRANCH_W),
                        proj.reshape(bsz, 1, s, PROJ_W), P_BV * HEAD_PAIRS),
                    4: (q4, k4, v4, 0), 16: (q16, k16, v16, 0)}
        dil = []
        for _, d in DIL_PATTERNS:
            qd, kd, vd, v_off = operands[d]
            num, den = _dilated_bounded(_dilated_tables(slopes2, bound_b[0], d), qd, kd, vd, v_off, bsz, s, d)
            if d == 1:
                num, den = num.reshape(bsz * s, BRANCH_W), den.reshape(bsz * s, BRANCH_W)
            dil.append((num, den))
        oc = _diff(True, lam, bound_c, slopes2, qt, kcn, vt, subln_pair, bsz, s, lambda_init)
        return _mix_out(True, x2, proj, dil, oc, sgu_g, ws_bf, bs_tab, conv_w, wout_bf, s)

    def running_max_path():
        dil = [_dilated(slopes2, bqn, bkn, proj, bsz, s, d) for (_, d) in DIL_PATTERNS]
        oc = _diff(False, lam, bound_c, slopes2, qt, kcn, vt, subln_pair, bsz, s, lambda_init)
        return _mix_out(False, x2, proj, dil, oc, sgu_g, ws_bf, bs_tab, conv_w, wout_bf, s)

    use_fixed_shift = jnp.maximum(bound_b[0], bound_c[0]) <= MAX_FIXED_SHIFT
    return lax.cond(use_fixed_shift, fixed_shift_path, running_max_path)


def _trunk(x, params):
    bsz, s, _ = x.shape
    x2 = x.reshape(bsz * s, D_MODEL)
    depth = params[0].shape[0]
    for l in range(depth):
        x2 = _layer(x2, bsz, s, l, *[p[l] for p in params])
    return x2.reshape(bsz, s, D_MODEL)


def kernel(x_prompt, x_sample, norm_g, w_in, sgu_g, w_s, b_s, qn_b, kn_b, qn_c, kn_c,
           lam_q1, lam_k1, lam_q2, lam_k2, subln_g, conv_w, w_out):
    params = (norm_g, w_in, sgu_g, w_s, b_s, qn_b, kn_b, qn_c, kn_c,
              lam_q1, lam_k1, lam_q2, lam_k2, subln_g, conv_w, w_out)
    return (_trunk(x_prompt, params), _trunk(x_sample, params))
```

```python
import functools
import math

import jax
import jax.numpy as jnp
from jax import lax
from jax.experimental import pallas as pl
from jax.experimental.pallas import tpu as pltpu

F32 = jnp.float32
BF16 = jnp.bfloat16
F8 = jnp.float8_e4m3fn

D_MODEL = 1024
HEAD_DIM = 64
GROUP_HEADS = 6
BRANCH_W = GROUP_HEADS * HEAD_DIM
N_PIECES = 15
PROJ_W = N_PIECES * BRANCH_W
MIX_W = 4 * BRANCH_W
CHUNK = 128
DIL_PATTERNS = ((128, 1), (512, 4), (2048, 16))
N_SIDE = 64
DIFF_QK_DIM = HEAD_DIM // 2
EPS = 1e-6
NEG = -1e30
LOG2E = math.log2(math.e)

LANES = 128
HEAD_PAIRS = BRANCH_W // LANES
V_ROWS = 80

P_AU, P_AV, P_AG, P_BQ, P_BK, P_BV, P_BG, P_CQ, P_CK, P_CV, P_CG, P_DIN, P_DB, P_DC, P_DG = range(15)

IN_TM = 1024
IN_TN = 1920
PREP_T = 512
MIX_T = 512
MIX_SUB = 512
HALO = 16
DIL_TQ = 128
DIL_WIN = DIL_TQ + 2 * N_SIDE
DIL_ROWS = 2048
DIL_GROUP = 8
DIFF_TQ = 256
DIFF_TQ_FIXED = 512
DIFF_TK = 256
VMEM_LIMIT = 56 * 1024 * 1024
MAX_FIXED_SHIFT = 40.0


def _cparams(sem):
    return pltpu.CompilerParams(dimension_semantics=sem, vmem_limit_bytes=VMEM_LIMIT)


def _inproj_kernel(x_ref, g_ref, w_ref, o_ref):
    x = x_ref[...]
    ms = jnp.mean(x * x, axis=-1, keepdims=True)
    h = (x * lax.rsqrt(ms + EPS) * g_ref[...]).astype(BF16)
    o_ref[...] = jnp.dot(h, w_ref[...], preferred_element_type=F32).astype(BF16)


def _inproj(x2, norm_g, w_in_bf):
    n = x2.shape[0]
    return pl.pallas_call(
        _inproj_kernel,
        grid=(PROJ_W // IN_TN, n // IN_TM),
        in_specs=[
            pl.BlockSpec((IN_TM, D_MODEL), lambda j, i: (i, 0)),
            pl.BlockSpec((1, D_MODEL), lambda j, i: (0, 0)),
            pl.BlockSpec((D_MODEL, IN_TN), lambda j, i: (0, j)),
        ],
        out_specs=pl.BlockSpec((IN_TM, IN_TN), lambda j, i: (i, j)),
        out_shape=jax.ShapeDtypeStruct((n, PROJ_W), BF16),
        compiler_params=_cparams(("arbitrary", "arbitrary")),
        name="inproj",
    )(x2, norm_g.reshape(1, D_MODEL), w_in_bf)


def _group_rms(x, e_ref, width):
    x2 = x * x
    hi = x2.astype(BF16)
    lo = (x2 - hi.astype(F32)).astype(BF16)
    e = e_ref[...]
    ss = jnp.dot(hi, e, preferred_element_type=F32) + jnp.dot(lo, e, preferred_element_type=F32)
    return x * lax.rsqrt(ss * (1.0 / width) + EPS)


def _prep_kernel(bq_ref, bk_ref, bv_ref, cq_ref, ck_ref, cv_ref, gqb_ref, gkb_ref, gqc_ref, gkc_ref,
                 e64_ref, e32_ref, bqn_ref, bkn_ref, kcn_ref, qt_ref, vt_ref,
                 q4_ref, k4_ref, v4_ref, q16_ref, k16_ref, v16_ref, k8_ref, q8_ref, sq_ref, sk_ref, sv_ref):
    t = bq_ref.shape[0]
    low_lanes = lax.broadcasted_iota(jnp.int32, (t, LANES), 1) < HEAD_DIM
    row = lax.broadcasted_iota(jnp.int32, (V_ROWS - HEAD_DIM, t), 0)
    ones_pad = jnp.where(row == 0, 1.0, 0.0).astype(BF16)
    for cb in range(HEAD_PAIRS):
        sl = slice(cb * LANES, (cb + 1) * LANES)
        sq_ref[cb] = _group_rms(bq_ref[:, sl].astype(F32), e64_ref, HEAD_DIM) * gqb_ref[:, sl]
        sk_ref[cb] = _group_rms(bk_ref[:, sl].astype(F32), e64_ref, HEAD_DIM) * gkb_ref[:, sl]
        sv_ref[cb] = bv_ref[:, sl].astype(F32)
        bqn_ref[:, sl] = sq_ref[cb].astype(BF16)
        bkn_ref[:, sl] = sk_ref[cb].astype(BF16)
        kn = _group_rms(ck_ref[:, sl].astype(F32), e32_ref, DIFF_QK_DIM) * gkc_ref[:, sl]
        kcn_ref[:, sl] = kn.astype(BF16)
        k_hi = kn.astype(F8).astype(F32)
        k_lo_swapped = pltpu.roll((kn - k_hi).astype(F8).astype(F32), HEAD_DIM, axis=1)
        zero = jnp.zeros_like(kn)
        k8_ref[0, cb, 0, :, 0:LANES] = jnp.where(low_lanes, k_hi, k_lo_swapped).astype(F8)
        k8_ref[0, cb, 0, :, LANES:2 * LANES] = jnp.where(low_lanes, k_hi, zero).astype(F8)
        k8_ref[0, cb, 1, :, 0:LANES] = jnp.where(low_lanes, k_lo_swapped, k_hi).astype(F8)
        k8_ref[0, cb, 1, :, LANES:2 * LANES] = jnp.where(low_lanes, zero, k_hi).astype(F8)
        qn_t = (_group_rms(cq_ref[:, sl].astype(F32), e32_ref, DIFF_QK_DIM) * gqc_ref[:, sl]).T
        qt_ref[0, cb] = qn_t.astype(BF16)
        q_hi = qn_t.astype(F8)
        q8_ref[0, cb, 0] = q_hi
        q8_ref[0, cb, 1] = (qn_t - q_hi.astype(F32)).astype(F8)
        vt = cv_ref[:, sl].astype(F32).T.astype(BF16)
        for hh in range(2):
            vt_ref[0, cb, hh, 0:HEAD_DIM, :] = vt[hh * HEAD_DIM:(hh + 1) * HEAD_DIM, :]
            vt_ref[0, cb, hh, HEAD_DIM:V_ROWS, :] = ones_pad
    for stage, out4, out16 in ((sq_ref, q4_ref, q16_ref), (sk_ref, k4_ref, k16_ref), (sv_ref, v4_ref, v16_ref)):
        for dilation, out in ((4, out4), (16, out16)):
            for r in range(dilation):
                for cb in range(HEAD_PAIRS):
                    out[0, r, :, cb * LANES:(cb + 1) * LANES] = (
                        stage[cb, pl.ds(r, t // dilation, stride=dilation), :].astype(BF16))


def _block_diag_ones(group):
    idx = jnp.arange(LANES) // group
    return (idx[:, None] == idx[None, :]).astype(BF16)


def _prep(proj, bsz, s, gqb, gkb, gqc, gkc):
    n = proj.shape[0]
    tiles_per_seq = s // PREP_T

    def piece(p):
        return pl.BlockSpec((PREP_T, BRANCH_W), lambda i, p=p: (i, p))

    def vec():
        return pl.BlockSpec((1, BRANCH_W), lambda i: (0, 0))

    def tokens():
        return pl.BlockSpec((PREP_T, BRANCH_W), lambda i: (i, 0))

    def residues(d):
        return pl.BlockSpec((1, d, PREP_T // d, BRANCH_W),
                            lambda i: (i // tiles_per_seq, 0, i % tiles_per_seq, 0))

    return pl.pallas_call(
        _prep_kernel,
        grid=(n // PREP_T,),
        in_specs=[piece(P_BQ), piece(P_BK), piece(P_BV), piece(P_CQ), piece(P_CK), piece(P_CV),
                  vec(), vec(), vec(), vec(),
                  pl.BlockSpec((LANES, LANES), lambda i: (0, 0)),
                  pl.BlockSpec((LANES, LANES), lambda i: (0, 0))],
        out_specs=[tokens(), tokens(), tokens(),
                   pl.BlockSpec((1, HEAD_PAIRS, LANES, PREP_T),
                                lambda i: (i // tiles_per_seq, 0, 0, i % tiles_per_seq)),
                   pl.BlockSpec((1, HEAD_PAIRS, 2, V_ROWS, PREP_T),
                                lambda i: (i // tiles_per_seq, 0, 0, 0, i % tiles_per_seq))]
                  + [residues(d) for d in (4, 4, 4, 16, 16, 16)]
                  + [pl.BlockSpec((1, HEAD_PAIRS, 2, PREP_T, 2 * LANES),
                                  lambda i: (i // tiles_per_seq, 0, 0, i % tiles_per_seq, 0)),
                     pl.BlockSpec((1, HEAD_PAIRS, 2, LANES, PREP_T),
                                  lambda i: (i // tiles_per_seq, 0, 0, 0, i % tiles_per_seq))],
        out_shape=[jax.ShapeDtypeStruct((n, BRANCH_W), BF16),
                   jax.ShapeDtypeStruct((n, BRANCH_W), BF16),
                   jax.ShapeDtypeStruct((n, BRANCH_W), BF16),
                   jax.ShapeDtypeStruct((bsz, HEAD_PAIRS, LANES, s), BF16),
                   jax.ShapeDtypeStruct((bsz, HEAD_PAIRS, 2, V_ROWS, s), BF16)]
                  + [jax.ShapeDtypeStruct((bsz, d, s // d, BRANCH_W), BF16) for d in (4, 4, 4, 16, 16, 16)]
                  + [jax.ShapeDtypeStruct((bsz, HEAD_PAIRS, 2, s, 2 * LANES), F8),
                     jax.ShapeDtypeStruct((bsz, HEAD_PAIRS, 2, LANES, s), F8)],
        scratch_shapes=[pltpu.VMEM((HEAD_PAIRS, PREP_T, LANES), F32)] * 3,
        compiler_params=_cparams(("arbitrary",)),
        name="prep",
    )(proj, proj, proj, proj, proj, proj, gqb, gkb, gqc, gkc,
      _block_diag_ones(HEAD_DIM), _block_diag_ones(DIFF_QK_DIM))


def _dilated_kernel(slope_ref, q_ref, k_ref, v_ref, o_ref, lse_ref, *, dilation, seq_sub):
    hp = pl.program_id(2)
    qi = pl.program_id(3)
    tqb = q_ref.shape[1]
    lane = lax.broadcasted_iota(jnp.int32, (DIL_TQ, LANES), 1)
    first_head = lane < HEAD_DIM
    rel0 = (lax.broadcasted_iota(jnp.int32, (DIL_TQ, DIL_WIN), 1)
            - lax.broadcasted_iota(jnp.int32, (DIL_TQ, DIL_WIN), 0))

    def sub_block(sb, carry):
        r0 = pl.multiple_of(sb * DIL_TQ, DIL_TQ)
        i0 = qi * tqb + r0
        ws = pl.multiple_of(jnp.clip(i0 - N_SIDE, 0, seq_sub - DIL_WIN), N_SIDE)
        kw = k_ref[0, pl.ds(ws, DIL_WIN), :]
        vw = v_ref[0, pl.ds(ws, DIL_WIN), :]
        q = q_ref[0, pl.ds(r0, DIL_TQ), :]
        dist = jnp.abs(rel0 + (ws - i0))
        valid = dist <= N_SIDE
        distf = dist.astype(F32)
        outs, lses = [], []
        for hh in range(2):
            slope = slope_ref[2 * hp + hh] * float(dilation)
            qm = jnp.where(first_head if hh == 0 else jnp.logical_not(first_head), q, jnp.zeros_like(q))
            sc = lax.dot_general(qm, kw, (((1,), (1,)), ((), ())), preferred_element_type=F32)
            sc = jnp.where(valid, sc - slope * distf, NEG)
            m = jnp.max(sc, axis=-1, keepdims=True)
            p = jnp.exp2(sc - m)
            den = jnp.sum(p, axis=-1, keepdims=True)
            o = jnp.dot(p.astype(BF16), vw, preferred_element_type=F32) / den
            outs.append(o)
            lses.append(jnp.broadcast_to(m + jnp.log2(den), (DIL_TQ, LANES)))
        o_ref[0, pl.ds(r0, DIL_TQ), :] = jnp.where(first_head, outs[0], outs[1]).astype(o_ref.dtype)
        lse_ref[0, pl.ds(r0, DIL_TQ), :] = jnp.where(first_head, lses[0], lses[1])
        return carry

    lax.fori_loop(0, tqb // DIL_TQ, sub_block, 0)


def _dilated(slopes2, bqn, bkn, proj, bsz, s, dilation):
    seq_sub = s // dilation
    tqb = min(seq_sub, 1024)
    q3 = bqn.reshape(bsz, seq_sub, dilation * BRANCH_W)
    k3 = bkn.reshape(bsz, seq_sub, dilation * BRANCH_W)
    v3 = proj.reshape(bsz, seq_sub, dilation * PROJ_W)
    blocks_q = BRANCH_W // LANES
    blocks_p = PROJ_W // LANES
    v_off = P_BV * blocks_q

    def tile():
        return pl.BlockSpec((1, tqb, LANES), lambda b, r, hp, qi: (b, qi, r * blocks_q + hp))

    o, lse = pl.pallas_call(
        functools.partial(_dilated_kernel, dilation=dilation, seq_sub=seq_sub),
        grid=(bsz, dilation, HEAD_PAIRS, seq_sub // tqb),
        in_specs=[
            pl.BlockSpec(memory_space=pltpu.SMEM),
            tile(),
            pl.BlockSpec((1, seq_sub, LANES), lambda b, r, hp, qi: (b, 0, r * blocks_q + hp)),
            pl.BlockSpec((1, seq_sub, LANES), lambda b, r, hp, qi: (b, 0, r * blocks_p + v_off + hp)),
        ],
        out_specs=[tile(), tile()],
        out_shape=[jax.ShapeDtypeStruct(q3.shape, BF16), jax.ShapeDtypeStruct(q3.shape, F32)],
        compiler_params=_cparams(("arbitrary",) * 4),
        name=f"dilated{dilation}",
    )(slopes2, q3, k3, v3)
    return o.reshape(bsz * s, BRANCH_W), lse.reshape(bsz * s, BRANCH_W)


def _dilated_bounded_kernel(tab_ref, q_ref, k_ref, v_ref, num_ref, den_ref, p_ref, *, seq_sub):
    qi = pl.program_id(3)
    n_res, tqb = q_ref.shape[1], q_ref.shape[2]
    sub_per_res = tqb // DIL_TQ
    first_head_q = lax.broadcasted_iota(jnp.int32, (DIL_TQ, LANES), 1) < HEAD_DIM
    v_lane = lax.broadcasted_iota(jnp.int32, (DIL_WIN, 2 * LANES), 1)
    value_lanes = (v_lane < HEAD_DIM) | (v_lane >= 2 * LANES - HEAD_DIM)

    def window(sb):
        res = sb // sub_per_res
        r0 = pl.multiple_of((sb % sub_per_res) * DIL_TQ, DIL_TQ)
        i0 = qi * tqb + r0
        ws = pl.multiple_of(jnp.clip(i0 - N_SIDE, 0, seq_sub - DIL_WIN), N_SIDE)
        variant = jnp.where(i0 == 0, 0, jnp.where(i0 == seq_sub - DIL_TQ, 2, 1))
        return res, r0, ws, variant

    def probabilities(u):
        for half in range(DIL_GROUP):
            res, r0, ws, variant = window(DIL_GROUP * u + half)
            kw = k_ref[0, res, pl.ds(ws, DIL_WIN), :]
            q = q_ref[0, res, pl.ds(r0, DIL_TQ), :]
            zero = jnp.zeros_like(q)
            q2 = jnp.concatenate([jnp.where(first_head_q, q, zero), jnp.where(first_head_q, zero, q)], axis=0)
            sc = lax.dot_general(q2, kw, (((1,), (1,)), ((), ())), preferred_element_type=F32)
            for hh in range(2):
                p_ref[half, hh * DIL_TQ:(hh + 1) * DIL_TQ, :] = jnp.exp2(
                    sc[hh * DIL_TQ:(hh + 1) * DIL_TQ] + tab_ref[hh, variant]).astype(BF16)

    def outputs(u):
        for half in range(DIL_GROUP):
            res, r0, ws, _ = window(DIL_GROUP * u + half)
            vw = v_ref[0, res, pl.ds(ws, DIL_WIN), :]
            vw2 = jnp.concatenate([vw, vw], axis=1)
            vcat = jnp.where(value_lanes, vw2, jnp.ones_like(vw2))
            a = jnp.dot(p_ref[half], vcat, preferred_element_type=F32)
            a0 = a[0:DIL_TQ, 0:LANES]
            a1 = a[DIL_TQ:2 * DIL_TQ, LANES:2 * LANES]
            num_ref[0, res, pl.ds(r0, DIL_TQ), :] = jnp.where(first_head_q, a0, a1).astype(num_ref.dtype)
            den_ref[0, res, pl.ds(r0, DIL_TQ), :] = pltpu.roll(jnp.where(first_head_q, a1, a0), HEAD_DIM, axis=1)

    n_groups = n_res * sub_per_res // DIL_GROUP
    probabilities(0)

    def step(u, carry):
        outputs(u - 1)
        probabilities(u)
        return carry

    lax.fori_loop(1, n_groups, step, 0)
    outputs(n_groups - 1)


def _dilated_tables(slopes2, bound, dilation):
    rel = ((jnp.arange(DIL_WIN)[None, :] - jnp.arange(DIL_TQ)[:, None])[None]
           - jnp.array([0, N_SIDE, 2 * N_SIDE])[:, None, None])
    dist = jnp.abs(rel).astype(F32)
    bias = -(slopes2 * float(dilation))[:, None, None, None] * dist[None] - bound
    return jnp.where(dist[None] <= N_SIDE, bias, NEG).astype(F32)


def _dilated_bounded(tables, q4d, k4d, v4d, v_block_offset, bsz, s, dilation):
    seq_sub = s // dilation
    tqb = min(seq_sub, DIL_ROWS)
    n_res = min(dilation, DIL_ROWS // tqb)

    def tile():
        return pl.BlockSpec((1, n_res, tqb, LANES), lambda b, r, hp, qi: (b, r, qi, hp))

    out_shape = (bsz, dilation, seq_sub, BRANCH_W)
    return pl.pallas_call(
        functools.partial(_dilated_bounded_kernel, seq_sub=seq_sub),
        grid=(bsz, dilation // n_res, HEAD_PAIRS, seq_sub // tqb),
        in_specs=[
            pl.BlockSpec((2, 3, DIL_TQ, DIL_WIN), lambda b, r, hp, qi: (hp, 0, 0, 0)),
            tile(),
            pl.BlockSpec((1, n_res, seq_sub, LANES), lambda b, r, hp, qi: (b, r, 0, hp)),
            pl.BlockSpec((1, n_res, seq_sub, LANES), lambda b, r, hp, qi: (b, r, 0, v_block_offset + hp)),
        ],
        out_specs=[tile(), tile()],
        out_shape=[jax.ShapeDtypeStruct(out_shape, BF16), jax.ShapeDtypeStruct(out_shape, F32)],
        scratch_shapes=[pltpu.VMEM((DIL_GROUP, 2 * DIL_TQ, DIL_WIN), BF16)],
        compiler_params=_cparams(("arbitrary",) * 4),
        name=f"dilated_bounded{dilation}",
    )(tables, q4d, k4d, v4d)


def _diff_kernel(lam_ref, slope_ref, qt_ref, k_ref, vt_ref, g_ref, o_ref, qm_ref, tm_ref, acc_ref,
                 *, out_scale):
    hp = pl.program_id(1)
    qi = pl.program_id(2)
    tq = qt_ref.shape[3]
    tk = DIFF_TK
    nk = k_ref.shape[1] // tk
    i0 = qi * tq

    qt = qt_ref[0, 0]
    row_group = lax.broadcasted_iota(jnp.int32, (LANES, tq), 0) // DIFF_QK_DIM
    for g in range(4):
        qm_ref[g] = jnp.where(row_group == g, qt, jnp.zeros_like(qt))
    dkq = (lax.broadcasted_iota(jnp.int32, (tk, tq), 0)
           - lax.broadcasted_iota(jnp.int32, (tk, tq), 1)).astype(F32)
    for hh in range(2):
        tm_ref[hh] = slope_ref[2 * hp + hh] * dkq
    acc_ref[...] = jnp.zeros_like(acc_ref)

    def key_tile(j, ms, mode):
        koff = pl.multiple_of(j * tk, tk)
        kt = k_ref[0, pl.ds(koff, tk), :]
        tile_dist = jnp.abs(i0 - koff).astype(F32)
        new_ms = []
        for hh in range(2):
            vt = vt_ref[0, 0, hh, :, pl.ds(koff, tk)]
            shift = slope_ref[2 * hp + hh] * tile_dist
            for c in range(2):
                g = 2 * hh + c
                sc = jnp.dot(kt, qm_ref[g], preferred_element_type=F32)
                if mode < 0:
                    sc = sc + tm_ref[hh]
                elif mode > 0:
                    sc = sc - tm_ref[hh]
                else:
                    sc = sc - jnp.abs(tm_ref[hh])
                m_old = ms[g]
                m_new = jnp.maximum(m_old, jnp.max(sc, axis=0, keepdims=True) - shift)
                p = jnp.exp2(sc - (m_new + shift))
                alpha = jnp.exp2(m_old - m_new)
                acc_ref[g] = alpha * acc_ref[g] + jnp.dot(vt, p.astype(BF16), preferred_element_type=F32)
                new_ms.append(m_new)
        return tuple(new_ms)

    ms = tuple(jnp.full((1, tq), NEG, F32) for _ in range(4))
    ms = lax.fori_loop(0, qi, lambda j, m: key_tile(j, m, -1), ms)
    ms = key_tile(qi, ms, 0)
    lax.fori_loop(qi + 1, nk, lambda j, m: key_tile(j, m, 1), ms)
    _diff_finish(lam_ref, g_ref, o_ref, acc_ref, out_scale)


def _diff_bounded_kernel(lam_ref, slope_ref, qt_ref, k_ref, vt_ref, g_ref, o_ref, qm_ref, tm_ref, acc_ref,
                         p_ref, *, out_scale):
    hp = pl.program_id(1)
    qi = pl.program_id(2)
    tq = qt_ref.shape[4]
    tk = DIFF_TK
    tiles_per_q = tq // tk
    nk = k_ref.shape[3] // tk
    i0 = qi * tq
    centre = 0.5 * lam_ref[1]

    q_hi, q_lo = qt_ref[0, 0, 0], qt_ref[0, 0, 1]
    zeros = jnp.zeros((DIFF_QK_DIM, tq), q_hi.dtype)
    for g in range(4):
        hh, c = divmod(g, 2)
        hi_block, lo_block, hi2_block = ((0, 2, 4), (2, 0, 6))[hh]
        rows = slice(g * DIFF_QK_DIM, (g + 1) * DIFF_QK_DIM)
        blocks = [zeros] * (2 * LANES // DIFF_QK_DIM)
        blocks[hi_block + c], blocks[lo_block + c], blocks[hi2_block + c] = q_hi[rows], q_hi[rows], q_lo[rows]
        qm_ref[g] = jnp.concatenate(blocks, axis=0)
    @pl.when(qi == 0)
    def _():
        dkq = (lax.broadcasted_iota(jnp.int32, (tk, tq), 0)
               - lax.broadcasted_iota(jnp.int32, (tk, tq), 1)).astype(F32)
        for hh in range(2):
            slope = slope_ref[2 * hp + hh]
            tm = slope * dkq
            tm_ref[hh, 0] = tm
            tm_ref[hh, 1] = -tm
            for r in range(tiles_per_q):
                origin = float(r * tk)
                tm_ref[hh, 2 + r] = slope * (origin - jnp.abs(dkq + origin))

    acc_ref[...] = jnp.zeros_like(acc_ref)

    def stage(scores=None, values=None):
        for hh in range(2):
            slope = slope_ref[2 * hp + hh]
            for c in range(2):
                g = 2 * hh + c
                parts = []
                for half in range(2):
                    if values is not None:
                        u, slot = values
                        koff = pl.multiple_of((2 * u + half) * tk, tk)
                        parts.append(jnp.dot(vt_ref[0, 0, hh, :, pl.ds(koff, tk)], p_ref[slot, half, g],
                                             preferred_element_type=F32))
                    if scores is not None:
                        u, slot = scores
                        j = 2 * u + half
                        koff = pl.multiple_of(j * tk, tk)
                        shift = -(slope * jnp.abs(i0 - koff).astype(F32) + centre)
                        first = tiles_per_q * qi
                        table = jnp.where(j < first, 0, jnp.where(j >= first + tiles_per_q, 1, 2 + j - first))
                        sc = jnp.dot(k_ref[0, 0, hh, pl.ds(koff, tk), :], qm_ref[g], preferred_element_type=F32)
                        p_ref[slot, half, g] = jnp.exp2((sc + tm_ref[hh, table] + shift).astype(BF16))
                if parts:
                    acc_ref[g] += parts[0] + parts[1]

    n_pairs = nk // 2
    stage(scores=(0, 0))

    def trip(w, carry):
        stage(scores=(2 * w + 1, 1), values=(2 * w, 0))
        stage(scores=(2 * w + 2, 0), values=(2 * w + 1, 1))
        return carry

    lax.fori_loop(0, n_pairs // 2 - 1, trip, 0)
    stage(scores=(n_pairs - 1, 1), values=(n_pairs - 2, 0))
    stage(values=(n_pairs - 1, 1))

    _diff_finish(lam_ref, g_ref, o_ref, acc_ref, out_scale)


def _diff_finish(lam_ref, g_ref, o_ref, acc_ref, out_scale):
    lam = lam_ref[0]
    heads = []
    for hh in range(2):
        a0 = acc_ref[2 * hh]
        a1 = acc_ref[2 * hh + 1]
        o = (a0[0:HEAD_DIM] / a0[HEAD_DIM:HEAD_DIM + 1]
             - lam * (a1[0:HEAD_DIM] / a1[HEAD_DIM:HEAD_DIM + 1]))
        ms2 = jnp.mean(o * o, axis=0, keepdims=True)
        heads.append(o * lax.rsqrt(ms2 + EPS))
    out = jnp.concatenate(heads, axis=0).T
    o_ref[0] = (out * (g_ref[...] * out_scale)).astype(o_ref.dtype)


def _diff(bounded, lam, score_bound, slopes2, q_t, k, vt, subln_pair, bsz, s, lambda_init):
    scalars = jnp.concatenate([lam, score_bound]).astype(F32)
    if bounded:
        body, name, tq = _diff_bounded_kernel, "diff_bounded", DIFF_TQ_FIXED
        q_spec = pl.BlockSpec((1, 1, 2, LANES, tq), lambda b, hp, qi: (b, hp, 0, 0, qi))
        k_spec = pl.BlockSpec((1, 1, 2, s, 2 * LANES), lambda b, hp, qi: (b, hp, 0, 0, 0))
        rhs_scratch = [pltpu.VMEM((4, 2 * LANES, tq), F8)]
        table_scratch = [pltpu.VMEM((2, 2 + tq // DIFF_TK, DIFF_TK, tq), F32)]
        stage_scratch = [pltpu.VMEM((2, 2, 4, DIFF_TK, tq), BF16)]
    else:
        body, name, tq = _diff_kernel, "diff_online", DIFF_TQ
        k = k.reshape(bsz, s, BRANCH_W)
        q_spec = pl.BlockSpec((1, 1, LANES, tq), lambda b, hp, qi: (b, hp, 0, qi))
        k_spec = pl.BlockSpec((1, s, LANES), lambda b, hp, qi: (b, 0, hp))
        rhs_scratch = [pltpu.VMEM((4, LANES, tq), BF16)]
        table_scratch = [pltpu.VMEM((2, DIFF_TK, tq), F32)]
        stage_scratch = []
    return pl.pallas_call(
        functools.partial(body, out_scale=1.0 - lambda_init),
        grid=(bsz, HEAD_PAIRS, s // tq),
        in_specs=[
            pl.BlockSpec(memory_space=pltpu.SMEM),
            pl.BlockSpec(memory_space=pltpu.SMEM),
            q_spec,
            k_spec,
            pl.BlockSpec((1, 1, 2, V_ROWS, s), lambda b, hp, qi: (b, hp, 0, 0, 0)),
            pl.BlockSpec((1, LANES), lambda b, hp, qi: (0, 0)),
        ],
        out_specs=pl.BlockSpec((1, tq, LANES), lambda b, hp, qi: (b, qi, hp)),
        out_shape=jax.ShapeDtypeStruct((bsz, s, BRANCH_W), BF16),
        scratch_shapes=rhs_scratch + table_scratch + [pltpu.VMEM((4, V_ROWS, tq), F32)] + stage_scratch,
        compiler_params=_cparams(("arbitrary",) * 3),
        name=name,
    )(scalars, slopes2, q_t, k, vt, subln_pair).reshape(bsz * s, BRANCH_W)


def _silu(x):
    return x * jax.nn.sigmoid(x)


def _mix_kernel(x_ref, au_ref, av_ref, ag_ref, bg_ref, cg_ref, din_ref, db_ref, dc_ref, dg_ref,
                din_p_ref, dc_p_ref, din_n_ref, dc_n_ref,
                o1_ref, o4_ref, o16_ref, l1_ref, l4_ref, l16_ref, oc_ref,
                sgu_ref, ws_ref, bs_ref, cw_ref, wout_ref, y_ref, *stage_refs, tiles_per_seq, bounded):
    i = pl.program_id(0)
    t = x_ref.shape[0]
    first_head = lax.broadcasted_iota(jnp.int32, (CHUNK, LANES), 1) < HEAD_DIM

    if bounded:
        for stage, blk, dilation in zip(stage_refs, (o4_ref, o16_ref, l4_ref, l16_ref), (4, 16, 4, 16)):
            for r in range(dilation):
                for cb in range(HEAD_PAIRS):
                    stage[cb, pl.ds(r, t // dilation, stride=dilation), :] = (
                        blk[0, r, :, cb * LANES:(cb + 1) * LANES].astype(F32))

    z = dc_ref[...].astype(F32) * din_ref[...].astype(F32)
    seq_pos = i % tiles_per_seq
    z_prev = (dc_p_ref[...].astype(F32) * din_p_ref[...].astype(F32))[HALO - 1:HALO, :]
    z_next = (dc_n_ref[...].astype(F32) * din_n_ref[...].astype(F32))[0:1, :]
    z_prev = jnp.where(seq_pos == 0, 0.0, z_prev)
    z_next = jnp.where(seq_pos == tiles_per_seq - 1, 0.0, z_next)
    row = lax.broadcasted_iota(jnp.int32, z.shape, 0)
    z_up = jnp.where(row == 0, z_prev, pltpu.roll(z, 1, axis=0))
    z_dn = jnp.where(row == t - 1, z_next, pltpu.roll(z, t - 1, axis=0))
    cw = cw_ref[...]
    out_d = db_ref[...].astype(F32) * (cw[0:1] * z_up + cw[1:2] * z + cw[2:3] * z_dn)

    for sb in range(t // MIX_SUB):
        rows = slice(sb * MIX_SUB, (sb + 1) * MIX_SUB)

        av = av_ref[rows, :].astype(F32)
        vn = (av * lax.rsqrt(jnp.mean(av * av, axis=-1, keepdims=True) + EPS) * sgu_ref[...]).astype(BF16)
        chunks = []
        for c in range(MIX_SUB // CHUNK):
            vc = vn[c * CHUNK:(c + 1) * CHUNK, :]
            pairs = []
            for hp in range(HEAD_PAIRS):
                vp = vc[:, hp * LANES:(hp + 1) * LANES]
                r0 = jnp.dot(ws_ref[2 * hp], vp, preferred_element_type=F32)
                r1 = jnp.dot(ws_ref[2 * hp + 1], vp, preferred_element_type=F32)
                pairs.append(jnp.where(first_head, r0, r1))
            chunks.append(jnp.concatenate(pairs, axis=1) + bs_ref[...])
        out_a = au_ref[rows, :].astype(F32) * jnp.concatenate(chunks, axis=0)

        if bounded:
            def in_order(stage):
                return jnp.concatenate([stage[cb, rows, :] for cb in range(HEAD_PAIRS)], axis=1)

            num = o1_ref[rows, :].astype(F32) + in_order(stage_refs[0]) + in_order(stage_refs[1])
            den = l1_ref[rows, :] + in_order(stage_refs[2]) + in_order(stage_refs[3])
            out_b = num / den
        else:
            l1, l4, l16 = l1_ref[rows, :], l4_ref[rows, :], l16_ref[rows, :]
            lm = jnp.maximum(jnp.maximum(l1, l4), l16)
            w1, w4, w16 = jnp.exp2(l1 - lm), jnp.exp2(l4 - lm), jnp.exp2(l16 - lm)
            out_b = ((w1 * o1_ref[rows, :].astype(F32) + w4 * o4_ref[rows, :].astype(F32)
                      + w16 * o16_ref[rows, :].astype(F32)) / (w1 + w4 + w16))

        mixed = jnp.concatenate([
            (_silu(ag_ref[rows, :].astype(F32)) * out_a).astype(BF16),
            (_silu(bg_ref[rows, :].astype(F32)) * out_b).astype(BF16),
            (_silu(cg_ref[rows, :].astype(F32)) * oc_ref[rows, :].astype(F32)).astype(BF16),
            (_silu(dg_ref[rows, :].astype(F32)) * out_d[rows]).astype(BF16)], axis=1)
        y_ref[rows, :] = x_ref[rows, :] + jnp.dot(mixed, wout_ref[...], preferred_element_type=F32)


def _mix_out(bounded, x2, proj, dil, oc, sgu_g, ws_bf, bs_tab, conv_w, wout_bf, s):
    n = x2.shape[0]
    tiles_per_seq = s // MIX_T
    halo_per_tile = MIX_T // HALO
    n_halo = n // HALO

    def piece(p):
        return pl.BlockSpec((MIX_T, BRANCH_W), lambda i, p=p: (i, p))

    def halo_prev(p):
        return pl.BlockSpec((HALO, BRANCH_W), lambda i, p=p: (jnp.maximum(i * halo_per_tile - 1, 0), p))

    def halo_next(p):
        return pl.BlockSpec((HALO, BRANCH_W),
                            lambda i, p=p: (jnp.minimum((i + 1) * halo_per_tile, n_halo - 1), p))

    def tokens():
        return pl.BlockSpec((MIX_T, BRANCH_W), lambda i: (i, 0))

    def const(shape):
        return pl.BlockSpec(shape, lambda i: (0,) * len(shape))

    def residues(d):
        return pl.BlockSpec((1, d, MIX_T // d, BRANCH_W),
                            lambda i: (i // tiles_per_seq, 0, i % tiles_per_seq, 0))

    (o1, l1), (o4, l4), (o16, l16) = dil
    if bounded:
        dil_specs = [tokens(), residues(4), residues(16)] * 2
        scratch = [pltpu.VMEM((HEAD_PAIRS, MIX_T, LANES), F32)] * 4
    else:
        dil_specs = [tokens()] * 6
        scratch = []
    return pl.pallas_call(
        functools.partial(_mix_kernel, tiles_per_seq=tiles_per_seq, bounded=bounded),
        grid=(n // MIX_T,),
        in_specs=[pl.BlockSpec((MIX_T, D_MODEL), lambda i: (i, 0)),
                  piece(P_AU), piece(P_AV), piece(P_AG), piece(P_BG), piece(P_CG),
                  piece(P_DIN), piece(P_DB), piece(P_DC), piece(P_DG),
                  halo_prev(P_DIN), halo_prev(P_DC), halo_next(P_DIN), halo_next(P_DC),
                  *dil_specs, tokens(),
                  const((1, BRANCH_W)), const((GROUP_HEADS, CHUNK, CHUNK)), const((CHUNK, BRANCH_W)),
                  const((3, BRANCH_W)), const((MIX_W, D_MODEL))],
        out_specs=pl.BlockSpec((MIX_T, D_MODEL), lambda i: (i, 0)),
        out_shape=jax.ShapeDtypeStruct((n, D_MODEL), F32),
        scratch_shapes=scratch,
        compiler_params=_cparams(("arbitrary",)),
        name="mixout_bounded" if bounded else "mixout",
    )(x2, *([proj] * 13), o1, o4, o16, l1, l4, l16, oc,
      sgu_g.reshape(1, BRANCH_W), ws_bf, bs_tab, conv_w, wout_bf)


def _layer(x2, bsz, s, l, norm_g, w_in, sgu_g, w_s, b_s, qn_b, kn_b, qn_c, kn_c,
           lam_q1, lam_k1, lam_q2, lam_k2, subln_g, conv_w, w_out):
    slopes2 = (2.0 ** (-8.0 * jnp.arange(1, GROUP_HEADS + 1, dtype=F32) / GROUP_HEADS)) * LOG2E
    lambda_init = 0.8 - 0.6 * math.exp(-0.3 * l)
    lam = (jnp.exp(jnp.sum(lam_q1 * lam_k1)) - jnp.exp(jnp.sum(lam_q2 * lam_k2)) + lambda_init).reshape(1)

    proj = _inproj(x2, norm_g, w_in.astype(BF16))
    gqb = (jnp.tile(qn_b, GROUP_HEADS) * (LOG2E / math.sqrt(HEAD_DIM))).reshape(1, BRANCH_W)
    gkb = jnp.tile(kn_b, GROUP_HEADS).reshape(1, BRANCH_W)
    gqc = (jnp.tile(qn_c, 2 * GROUP_HEADS) * (LOG2E / math.sqrt(DIFF_QK_DIM))).reshape(1, BRANCH_W)
    gkc = jnp.tile(kn_c, 2 * GROUP_HEADS).reshape(1, BRANCH_W)
    bqn, bkn, kcn, qt, vt, q4, k4, v4, q16, k16, v16, k8, q8 = _prep(proj, bsz, s, gqb, gkb, gqc, gkc)

    bound_b = (jnp.max(jnp.abs(qn_b)) * jnp.max(jnp.abs(kn_b)) * (LOG2E * math.sqrt(HEAD_DIM))).reshape(1)
    bound_c = (jnp.max(jnp.abs(qn_c)) * jnp.max(jnp.abs(kn_c)) * (LOG2E * math.sqrt(DIFF_QK_DIM))).reshape(1)
    subln_pair = jnp.tile(subln_g, 2).reshape(1, LANES)
    bs_tab = jnp.repeat(b_s.T, HEAD_DIM, axis=1)
    ws_bf, wout_bf = w_s.astype(BF16), w_out.astype(BF16)

    def fixed_shift_path():
        operands = {1: (bqn.reshape(bsz, 1, s, BRANCH_W), bkn.reshape(bsz, 1, s, BRANCH_W),
                        proj.reshape(bsz, 1, s, PROJ_W), P_BV * HEAD_PAIRS),
                    4: (q4, k4, v4, 0), 16: (q16, k16, v16, 0)}
        dil = []
        for _, d in DIL_PATTERNS:
            qd, kd, vd, v_off = operands[d]
            num, den = _dilated_bounded(_dilated_tables(slopes2, bound_b[0], d), qd, kd, vd, v_off, bsz, s, d)
            if d == 1:
                num, den = num.reshape(bsz * s, BRANCH_W), den.reshape(bsz * s, BRANCH_W)
            dil.append((num, den))
        oc = _diff(True, lam, bound_c, slopes2, q8, k8, vt, subln_pair, bsz, s, lambda_init)
        return _mix_out(True, x2, proj, dil, oc, sgu_g, ws_bf, bs_tab, conv_w, wout_bf, s)

    def running_max_path():
        dil = [_dilated(slopes2, bqn, bkn, proj, bsz, s, d) for (_, d) in DIL_PATTERNS]
        oc = _diff(False, lam, bound_c, slopes2, qt, kcn, vt, subln_pair, bsz, s, lambda_init)
        return _mix_out(False, x2, proj, dil, oc, sgu_g, ws_bf, bs_tab, conv_w, wout_bf, s)

    use_fixed_shift = jnp.maximum(bound_b[0], bound_c[0]) <= MAX_FIXED_SHIFT
    return lax.cond(use_fixed_shift, fixed_shift_path, running_max_path)


def _trunk(x, params):
    bsz, s, _ = x.shape
    x2 = x.reshape(bsz * s, D_MODEL)
    depth = params[0].shape[0]
    for l in range(depth):
        x2 = _layer(x2, bsz, s, l, *[p[l] for p in params])
    return x2.reshape(bsz, s, D_MODEL)


def kernel(x_prompt, x_sample, norm_g, w_in, sgu_g, w_s, b_s, qn_b, kn_b, qn_c, kn_c,
           lam_q1, lam_k1, lam_q2, lam_k2, subln_g, conv_w, w_out):
    params = (norm_g, w_in, sgu_g, w_s, b_s, qn_b, kn_b, qn_c, kn_c,
              lam_q1, lam_k1, lam_q2, lam_k2, subln_g, conv_w, w_out)
    return (_trunk(x_prompt, params), _trunk(x_sample, params))
```

```python
import functools
import math

import jax
import jax.numpy as jnp
from jax import lax
from jax.experimental import pallas as pl
from jax.experimental.pallas import tpu as pltpu

F32 = jnp.float32
BF16 = jnp.bfloat16
F8 = jnp.float8_e4m3fn

D_MODEL = 1024
HEAD_DIM = 64
GROUP_HEADS = 6
BRANCH_W = GROUP_HEADS * HEAD_DIM
N_PIECES = 15
PROJ_W = N_PIECES * BRANCH_W
MIX_W = 4 * BRANCH_W
CHUNK = 128
DIL_PATTERNS = ((128, 1), (512, 4), (2048, 16))
N_SIDE = 64
DIFF_QK_DIM = HEAD_DIM // 2
EPS = 1e-6
NEG = -1e30
LOG2E = math.log2(math.e)

LANES = 128
HEAD_PAIRS = BRANCH_W // LANES
V_ROWS = 80

P_AU, P_AV, P_AG, P_BQ, P_BK, P_BV, P_BG, P_CQ, P_CK, P_CV, P_CG, P_DIN, P_DB, P_DC, P_DG = range(15)

IN_TM = 1024
IN_TN = 1920
PREP_T = 512
MIX_T = 512
MIX_SUB = 512
HALO = 16
DIL_TQ = 128
DIL_WIN = DIL_TQ + 2 * N_SIDE
DIL_ROWS = 2048
DIL_GROUP = 8
DIFF_TQ = 256
DIFF_TQ_FIXED = 512
DIFF_TK = 256
VMEM_LIMIT = 56 * 1024 * 1024
MAX_FIXED_SHIFT = 40.0


def _cparams(sem):
    return pltpu.CompilerParams(dimension_semantics=sem, vmem_limit_bytes=VMEM_LIMIT)


def _inproj_kernel(x_ref, g_ref, w_ref, o_ref):
    x = x_ref[...]
    ms = jnp.mean(x * x, axis=-1, keepdims=True)
    h = (x * lax.rsqrt(ms + EPS) * g_ref[...]).astype(BF16)
    o_ref[...] = jnp.dot(h, w_ref[...], preferred_element_type=F32).astype(BF16)


def _inproj(x2, norm_g, w_in_bf):
    n = x2.shape[0]
    return pl.pallas_call(
        _inproj_kernel,
        grid=(PROJ_W // IN_TN, n // IN_TM),
        in_specs=[
            pl.BlockSpec((IN_TM, D_MODEL), lambda j, i: (i, 0)),
            pl.BlockSpec((1, D_MODEL), lambda j, i: (0, 0)),
            pl.BlockSpec((D_MODEL, IN_TN), lambda j, i: (0, j)),
        ],
        out_specs=pl.BlockSpec((IN_TM, IN_TN), lambda j, i: (i, j)),
        out_shape=jax.ShapeDtypeStruct((n, PROJ_W), BF16),
        compiler_params=_cparams(("arbitrary", "arbitrary")),
        name="inproj",
    )(x2, norm_g.reshape(1, D_MODEL), w_in_bf)


def _group_rms(x, e_ref, width):
    x2 = x * x
    hi = x2.astype(BF16)
    lo = (x2 - hi.astype(F32)).astype(BF16)
    e = e_ref[...]
    ss = jnp.dot(hi, e, preferred_element_type=F32) + jnp.dot(lo, e, preferred_element_type=F32)
    return x * lax.rsqrt(ss * (1.0 / width) + EPS)


def _prep_kernel(bq_ref, bk_ref, bv_ref, cq_ref, ck_ref, cv_ref, gqb_ref, gkb_ref, gqc_ref, gkc_ref,
                 e64_ref, e32_ref, bqn_ref, bkn_ref, kcn_ref, qt_ref, vt_ref,
                 q4_ref, k4_ref, v4_ref, q16_ref, k16_ref, v16_ref, k8_ref, q8_ref, sq_ref, sk_ref, sv_ref):
    t = bq_ref.shape[0]
    low_lanes = lax.broadcasted_iota(jnp.int32, (t, LANES), 1) < HEAD_DIM
    row = lax.broadcasted_iota(jnp.int32, (V_ROWS - HEAD_DIM, t), 0)
    ones_pad = jnp.where(row == 0, 1.0, 0.0).astype(BF16)
    for cb in range(HEAD_PAIRS):
        sl = slice(cb * LANES, (cb + 1) * LANES)
        sq_ref[cb] = _group_rms(bq_ref[:, sl].astype(F32), e64_ref, HEAD_DIM) * gqb_ref[:, sl]
        sk_ref[cb] = _group_rms(bk_ref[:, sl].astype(F32), e64_ref, HEAD_DIM) * gkb_ref[:, sl]
        sv_ref[cb] = bv_ref[:, sl].astype(F32)
        bqn_ref[:, sl] = sq_ref[cb].astype(BF16)
        bkn_ref[:, sl] = sk_ref[cb].astype(BF16)
        kn = _group_rms(ck_ref[:, sl].astype(F32), e32_ref, DIFF_QK_DIM) * gkc_ref[:, sl]
        kcn_ref[:, sl] = kn.astype(BF16)
        k_hi = kn.astype(F8).astype(F32)
        k_lo_swapped = pltpu.roll((kn - k_hi).astype(F8).astype(F32), HEAD_DIM, axis=1)
        zero = jnp.zeros_like(kn)
        k8_ref[0, cb, 0, :, 0:LANES] = jnp.where(low_lanes, k_hi, k_lo_swapped).astype(F8)
        k8_ref[0, cb, 0, :, LANES:2 * LANES] = jnp.where(low_lanes, k_hi, zero).astype(F8)
        k8_ref[0, cb, 1, :, 0:LANES] = jnp.where(low_lanes, k_lo_swapped, k_hi).astype(F8)
        k8_ref[0, cb, 1, :, LANES:2 * LANES] = jnp.where(low_lanes, zero, k_hi).astype(F8)
        qn_t = (_group_rms(cq_ref[:, sl].astype(F32), e32_ref, DIFF_QK_DIM) * gqc_ref[:, sl]).T
        qt_ref[0, cb] = qn_t.astype(BF16)
        q_hi = qn_t.astype(F8)
        q8_ref[0, cb, 0] = q_hi
        q8_ref[0, cb, 1] = (qn_t - q_hi.astype(F32)).astype(F8)
        vt = cv_ref[:, sl].astype(F32).T.astype(BF16)
        for hh in range(2):
            vt_ref[0, cb, hh, 0:HEAD_DIM, :] = vt[hh * HEAD_DIM:(hh + 1) * HEAD_DIM, :]
            vt_ref[0, cb, hh, HEAD_DIM:V_ROWS, :] = ones_pad
    for stage, out4, out16 in ((sq_ref, q4_ref, q16_ref), (sk_ref, k4_ref, k16_ref), (sv_ref, v4_ref, v16_ref)):
        for dilation, out in ((4, out4), (16, out16)):
            for r in range(dilation):
                for cb in range(HEAD_PAIRS):
                    out[0, r, :, cb * LANES:(cb + 1) * LANES] = (
                        stage[cb, pl.ds(r, t // dilation, stride=dilation), :].astype(BF16))


def _block_diag_ones(group):
    idx = jnp.arange(LANES) // group
    return (idx[:, None] == idx[None, :]).astype(BF16)


def _prep(proj, bsz, s, gqb, gkb, gqc, gkc):
    n = proj.shape[0]
    tiles_per_seq = s // PREP_T

    def piece(p):
        return pl.BlockSpec((PREP_T, BRANCH_W), lambda i, p=p: (i, p))

    def vec():
        return pl.BlockSpec((1, BRANCH_W), lambda i: (0, 0))

    def tokens():
        return pl.BlockSpec((PREP_T, BRANCH_W), lambda i: (i, 0))

    def residues(d):
        return pl.BlockSpec((1, d, PREP_T // d, BRANCH_W),
                            lambda i: (i // tiles_per_seq, 0, i % tiles_per_seq, 0))

    return pl.pallas_call(
        _prep_kernel,
        grid=(n // PREP_T,),
        in_specs=[piece(P_BQ), piece(P_BK), piece(P_BV), piece(P_CQ), piece(P_CK), piece(P_CV),
                  vec(), vec(), vec(), vec(),
                  pl.BlockSpec((LANES, LANES), lambda i: (0, 0)),
                  pl.BlockSpec((LANES, LANES), lambda i: (0, 0))],
        out_specs=[tokens(), tokens(), tokens(),
                   pl.BlockSpec((1, HEAD_PAIRS, LANES, PREP_T),
                                lambda i: (i // tiles_per_seq, 0, 0, i % tiles_per_seq)),
                   pl.BlockSpec((1, HEAD_PAIRS, 2, V_ROWS, PREP_T),
                                lambda i: (i // tiles_per_seq, 0, 0, 0, i % tiles_per_seq))]
                  + [residues(d) for d in (4, 4, 4, 16, 16, 16)]
                  + [pl.BlockSpec((1, HEAD_PAIRS, 2, PREP_T, 2 * LANES),
                                  lambda i: (i // tiles_per_seq, 0, 0, i % tiles_per_seq, 0)),
                     pl.BlockSpec((1, HEAD_PAIRS, 2, LANES, PREP_T),
                                  lambda i: (i // tiles_per_seq, 0, 0, 0, i % tiles_per_seq))],
        out_shape=[jax.ShapeDtypeStruct((n, BRANCH_W), BF16),
                   jax.ShapeDtypeStruct((n, BRANCH_W), BF16),
                   jax.ShapeDtypeStruct((n, BRANCH_W), BF16),
                   jax.ShapeDtypeStruct((bsz, HEAD_PAIRS, LANES, s), BF16),
                   jax.ShapeDtypeStruct((bsz, HEAD_PAIRS, 2, V_ROWS, s), BF16)]
                  + [jax.ShapeDtypeStruct((bsz, d, s // d, BRANCH_W), BF16) for d in (4, 4, 4, 16, 16, 16)]
                  + [jax.ShapeDtypeStruct((bsz, HEAD_PAIRS, 2, s, 2 * LANES), F8),
                     jax.ShapeDtypeStruct((bsz, HEAD_PAIRS, 2, LANES, s), F8)],
        scratch_shapes=[pltpu.VMEM((HEAD_PAIRS, PREP_T, LANES), F32)] * 3,
        compiler_params=_cparams(("arbitrary",)),
        name="prep",
    )(proj, proj, proj, proj, proj, proj, gqb, gkb, gqc, gkc,
      _block_diag_ones(HEAD_DIM), _block_diag_ones(DIFF_QK_DIM))


def _dilated_kernel(slope_ref, q_ref, k_ref, v_ref, o_ref, lse_ref, *, dilation, seq_sub):
    hp = pl.program_id(2)
    qi = pl.program_id(3)
    tqb = q_ref.shape[1]
    lane = lax.broadcasted_iota(jnp.int32, (DIL_TQ, LANES), 1)
    first_head = lane < HEAD_DIM
    rel0 = (lax.broadcasted_iota(jnp.int32, (DIL_TQ, DIL_WIN), 1)
            - lax.broadcasted_iota(jnp.int32, (DIL_TQ, DIL_WIN), 0))

    def sub_block(sb, carry):
        r0 = pl.multiple_of(sb * DIL_TQ, DIL_TQ)
        i0 = qi * tqb + r0
        ws = pl.multiple_of(jnp.clip(i0 - N_SIDE, 0, seq_sub - DIL_WIN), N_SIDE)
        kw = k_ref[0, pl.ds(ws, DIL_WIN), :]
        vw = v_ref[0, pl.ds(ws, DIL_WIN), :]
        q = q_ref[0, pl.ds(r0, DIL_TQ), :]
        dist = jnp.abs(rel0 + (ws - i0))
        valid = dist <= N_SIDE
        distf = dist.astype(F32)
        outs, lses = [], []
        for hh in range(2):
            slope = slope_ref[2 * hp + hh] * float(dilation)
            qm = jnp.where(first_head if hh == 0 else jnp.logical_not(first_head), q, jnp.zeros_like(q))
            sc = lax.dot_general(qm, kw, (((1,), (1,)), ((), ())), preferred_element_type=F32)
            sc = jnp.where(valid, sc - slope * distf, NEG)
            m = jnp.max(sc, axis=-1, keepdims=True)
            p = jnp.exp2(sc - m)
            den = jnp.sum(p, axis=-1, keepdims=True)
            o = jnp.dot(p.astype(BF16), vw, preferred_element_type=F32) / den
            outs.append(o)
            lses.append(jnp.broadcast_to(m + jnp.log2(den), (DIL_TQ, LANES)))
        o_ref[0, pl.ds(r0, DIL_TQ), :] = jnp.where(first_head, outs[0], outs[1]).astype(o_ref.dtype)
        lse_ref[0, pl.ds(r0, DIL_TQ), :] = jnp.where(first_head, lses[0], lses[1])
        return carry

    lax.fori_loop(0, tqb // DIL_TQ, sub_block, 0)


def _dilated(slopes2, bqn, bkn, proj, bsz, s, dilation):
    seq_sub = s // dilation
    tqb = min(seq_sub, 1024)
    q3 = bqn.reshape(bsz, seq_sub, dilation * BRANCH_W)
    k3 = bkn.reshape(bsz, seq_sub, dilation * BRANCH_W)
    v3 = proj.reshape(bsz, seq_sub, dilation * PROJ_W)
    blocks_q = BRANCH_W // LANES
    blocks_p = PROJ_W // LANES
    v_off = P_BV * blocks_q

    def tile():
        return pl.BlockSpec((1, tqb, LANES), lambda b, r, hp, qi: (b, qi, r * blocks_q + hp))

    o, lse = pl.pallas_call(
        functools.partial(_dilated_kernel, dilation=dilation, seq_sub=seq_sub),
        grid=(bsz, dilation, HEAD_PAIRS, seq_sub // tqb),
        in_specs=[
            pl.BlockSpec(memory_space=pltpu.SMEM),
            tile(),
            pl.BlockSpec((1, seq_sub, LANES), lambda b, r, hp, qi: (b, 0, r * blocks_q + hp)),
            pl.BlockSpec((1, seq_sub, LANES), lambda b, r, hp, qi: (b, 0, r * blocks_p + v_off + hp)),
        ],
        out_specs=[tile(), tile()],
        out_shape=[jax.ShapeDtypeStruct(q3.shape, BF16), jax.ShapeDtypeStruct(q3.shape, F32)],
        compiler_params=_cparams(("arbitrary",) * 4),
        name=f"dilated{dilation}",
    )(slopes2, q3, k3, v3)
    return o.reshape(bsz * s, BRANCH_W), lse.reshape(bsz * s, BRANCH_W)


def _dilated_bounded_kernel(tab_ref, q_ref, k_ref, v_ref, num_ref, den_ref, p_ref, *, seq_sub):
    qi = pl.program_id(3)
    n_res, tqb = q_ref.shape[1], q_ref.shape[2]
    sub_per_res = tqb // DIL_TQ
    first_head_q = lax.broadcasted_iota(jnp.int32, (DIL_TQ, LANES), 1) < HEAD_DIM
    v_lane = lax.broadcasted_iota(jnp.int32, (DIL_WIN, 2 * LANES), 1)
    value_lanes = (v_lane < HEAD_DIM) | (v_lane >= 2 * LANES - HEAD_DIM)

    def window(sb):
        res = sb // sub_per_res
        r0 = pl.multiple_of((sb % sub_per_res) * DIL_TQ, DIL_TQ)
        i0 = qi * tqb + r0
        ws = pl.multiple_of(jnp.clip(i0 - N_SIDE, 0, seq_sub - DIL_WIN), N_SIDE)
        variant = jnp.where(i0 == 0, 0, jnp.where(i0 == seq_sub - DIL_TQ, 2, 1))
        return res, r0, ws, variant

    def probabilities(u):
        for half in range(DIL_GROUP):
            res, r0, ws, variant = window(DIL_GROUP * u + half)
            kw = k_ref[0, res, pl.ds(ws, DIL_WIN), :]
            q = q_ref[0, res, pl.ds(r0, DIL_TQ), :]
            zero = jnp.zeros_like(q)
            q2 = jnp.concatenate([jnp.where(first_head_q, q, zero), jnp.where(first_head_q, zero, q)], axis=0)
            sc = lax.dot_general(q2, kw, (((1,), (1,)), ((), ())), preferred_element_type=F32)
            for hh in range(2):
                p_ref[half, hh * DIL_TQ:(hh + 1) * DIL_TQ, :] = jnp.exp2(
                    sc[hh * DIL_TQ:(hh + 1) * DIL_TQ] + tab_ref[hh, variant]).astype(BF16)

    def outputs(u):
        for half in range(DIL_GROUP):
            res, r0, ws, _ = window(DIL_GROUP * u + half)
            vw = v_ref[0, res, pl.ds(ws, DIL_WIN), :]
            vw2 = jnp.concatenate([vw, vw], axis=1)
            vcat = jnp.where(value_lanes, vw2, jnp.ones_like(vw2))
            a = jnp.dot(p_ref[half], vcat, preferred_element_type=F32)
            a0 = a[0:DIL_TQ, 0:LANES]
            a1 = a[DIL_TQ:2 * DIL_TQ, LANES:2 * LANES]
            num_ref[0, res, pl.ds(r0, DIL_TQ), :] = jnp.where(first_head_q, a0, a1).astype(num_ref.dtype)
            den_ref[0, res, pl.ds(r0, DIL_TQ), :] = pltpu.roll(jnp.where(first_head_q, a1, a0), HEAD_DIM, axis=1)

    n_groups = n_res * sub_per_res // DIL_GROUP
    probabilities(0)

    def step(u, carry):
        outputs(u - 1)
        probabilities(u)
        return carry

    lax.fori_loop(1, n_groups, step, 0)
    outputs(n_groups - 1)


def _dilated_tables(slopes2, bound, dilation):
    rel = ((jnp.arange(DIL_WIN)[None, :] - jnp.arange(DIL_TQ)[:, None])[None]
           - jnp.array([0, N_SIDE, 2 * N_SIDE])[:, None, None])
    dist = jnp.abs(rel).astype(F32)
    bias = -(slopes2 * float(dilation))[:, None, None, None] * dist[None] - bound
    return jnp.where(dist[None] <= N_SIDE, bias, NEG).astype(F32)


def _dilated_bounded(tables, q4d, k4d, v4d, v_block_offset, bsz, s, dilation):
    seq_sub = s // dilation
    tqb = min(seq_sub, DIL_ROWS)
    n_res = min(dilation, DIL_ROWS // tqb)

    def tile():
        return pl.BlockSpec((1, n_res, tqb, LANES), lambda b, r, hp, qi: (b, r, qi, hp))

    out_shape = (bsz, dilation, seq_sub, BRANCH_W)
    return pl.pallas_call(
        functools.partial(_dilated_bounded_kernel, seq_sub=seq_sub),
        grid=(bsz, dilation // n_res, HEAD_PAIRS, seq_sub // tqb),
        in_specs=[
            pl.BlockSpec((2, 3, DIL_TQ, DIL_WIN), lambda b, r, hp, qi: (hp, 0, 0, 0)),
            tile(),
            pl.BlockSpec((1, n_res, seq_sub, LANES), lambda b, r, hp, qi: (b, r, 0, hp)),
            pl.BlockSpec((1, n_res, seq_sub, LANES), lambda b, r, hp, qi: (b, r, 0, v_block_offset + hp)),
        ],
        out_specs=[tile(), tile()],
        out_shape=[jax.ShapeDtypeStruct(out_shape, BF16), jax.ShapeDtypeStruct(out_shape, F32)],
        scratch_shapes=[pltpu.VMEM((DIL_GROUP, 2 * DIL_TQ, DIL_WIN), BF16)],
        compiler_params=_cparams(("arbitrary",) * 4),
        name=f"dilated_bounded{dilation}",
    )(tables, q4d, k4d, v4d)


def _diff_kernel(lam_ref, slope_ref, qt_ref, k_ref, vt_ref, g_ref, o_ref, qm_ref, tm_ref, acc_ref,
                 *, out_scale):
    hp = pl.program_id(1)
    qi = pl.program_id(2)
    tq = qt_ref.shape[3]
    tk = DIFF_TK
    nk = k_ref.shape[1] // tk
    i0 = qi * tq

    qt = qt_ref[0, 0]
    row_group = lax.broadcasted_iota(jnp.int32, (LANES, tq), 0) // DIFF_QK_DIM
    for g in range(4):
        qm_ref[g] = jnp.where(row_group == g, qt, jnp.zeros_like(qt))
    dkq = (lax.broadcasted_iota(jnp.int32, (tk, tq), 0)
           - lax.broadcasted_iota(jnp.int32, (tk, tq), 1)).astype(F32)
    for hh in range(2):
        tm_ref[hh] = slope_ref[2 * hp + hh] * dkq
    acc_ref[...] = jnp.zeros_like(acc_ref)

    def key_tile(j, ms, mode):
        koff = pl.multiple_of(j * tk, tk)
        kt = k_ref[0, pl.ds(koff, tk), :]
        tile_dist = jnp.abs(i0 - koff).astype(F32)
        new_ms = []
        for hh in range(2):
            vt = vt_ref[0, 0, hh, :, pl.ds(koff, tk)]
            shift = slope_ref[2 * hp + hh] * tile_dist
            for c in range(2):
                g = 2 * hh + c
                sc = jnp.dot(kt, qm_ref[g], preferred_element_type=F32)
                if mode < 0:
                    sc = sc + tm_ref[hh]
                elif mode > 0:
                    sc = sc - tm_ref[hh]
                else:
                    sc = sc - jnp.abs(tm_ref[hh])
                m_old = ms[g]
                m_new = jnp.maximum(m_old, jnp.max(sc, axis=0, keepdims=True) - shift)
                p = jnp.exp2(sc - (m_new + shift))
                alpha = jnp.exp2(m_old - m_new)
                acc_ref[g] = alpha * acc_ref[g] + jnp.dot(vt, p.astype(BF16), preferred_element_type=F32)
                new_ms.append(m_new)
        return tuple(new_ms)

    ms = tuple(jnp.full((1, tq), NEG, F32) for _ in range(4))
    ms = lax.fori_loop(0, qi, lambda j, m: key_tile(j, m, -1), ms)
    ms = key_tile(qi, ms, 0)
    lax.fori_loop(qi + 1, nk, lambda j, m: key_tile(j, m, 1), ms)
    _diff_finish(lam_ref, g_ref, o_ref, acc_ref, out_scale)


def _diff_bounded_kernel(lam_ref, slope_ref, qt_ref, k_ref, vt_ref, g_ref, o_ref, qm_ref, tm_ref, acc_ref,
                         p_ref, *, out_scale):
    hp = pl.program_id(1)
    qi = pl.program_id(2)
    tq = qt_ref.shape[4]
    tk = DIFF_TK
    tiles_per_q = tq // tk
    nk = k_ref.shape[3] // tk
    i0 = qi * tq
    centre = 0.5 * lam_ref[1]

    q_hi, q_lo = qt_ref[0, 0, 0], qt_ref[0, 0, 1]
    zeros = jnp.zeros((DIFF_QK_DIM, tq), q_hi.dtype)
    for g in range(4):
        hh, c = divmod(g, 2)
        hi_block, lo_block, hi2_block = ((0, 2, 4), (2, 0, 6))[hh]
        rows = slice(g * DIFF_QK_DIM, (g + 1) * DIFF_QK_DIM)
        blocks = [zeros] * (2 * LANES // DIFF_QK_DIM)
        blocks[hi_block + c], blocks[lo_block + c], blocks[hi2_block + c] = q_hi[rows], q_hi[rows], q_lo[rows]
        qm_ref[g] = jnp.concatenate(blocks, axis=0)
    @pl.when(qi == 0)
    def _():
        dkq = (lax.broadcasted_iota(jnp.int32, (tk, tq), 0)
               - lax.broadcasted_iota(jnp.int32, (tk, tq), 1)).astype(F32)
        for hh in range(2):
            slope = slope_ref[2 * hp + hh]
            tm = slope * dkq
            tm_ref[hh, 0] = tm
            tm_ref[hh, 1] = -tm
            for r in range(tiles_per_q):
                origin = float(r * tk)
                tm_ref[hh, 2 + r] = slope * (origin - jnp.abs(dkq + origin))

    acc_ref[...] = jnp.zeros_like(acc_ref)

    def stage(scores=None, values=None):
        for hh in range(2):
            slope = slope_ref[2 * hp + hh]
            for c in range(2):
                g = 2 * hh + c
                if values is not None:
                    u, slot = values
                    koff = pl.multiple_of(2 * u * tk, 2 * tk)
                    acc_ref[g] += jnp.dot(vt_ref[0, 0, hh, :, pl.ds(koff, 2 * tk)], p_ref[slot, g],
                                          preferred_element_type=F32)
                if scores is not None:
                    u, slot = scores
                    koff = pl.multiple_of(2 * u * tk, 2 * tk)
                    sc = jnp.dot(k_ref[0, 0, hh, pl.ds(koff, 2 * tk), :], qm_ref[g], preferred_element_type=F32)
                    first = tiles_per_q * qi
                    for half in range(2):
                        j = 2 * u + half
                        rows = slice(half * tk, (half + 1) * tk)
                        shift = -(slope * jnp.abs(i0 - j * tk).astype(F32) + centre)
                        table = jnp.where(j < first, 0, jnp.where(j >= first + tiles_per_q, 1, 2 + j - first))
                        p_ref[slot, g, rows, :] = jnp.exp2((sc[rows] + tm_ref[hh, table] + shift).astype(BF16))

    n_pairs = nk // 2
    stage(scores=(0, 0))

    def trip(w, carry):
        stage(scores=(2 * w + 1, 1), values=(2 * w, 0))
        stage(scores=(2 * w + 2, 0), values=(2 * w + 1, 1))
        return carry

    lax.fori_loop(0, n_pairs // 2 - 1, trip, 0)
    stage(scores=(n_pairs - 1, 1), values=(n_pairs - 2, 0))
    stage(values=(n_pairs - 1, 1))

    _diff_finish(lam_ref, g_ref, o_ref, acc_ref, out_scale)


def _diff_finish(lam_ref, g_ref, o_ref, acc_ref, out_scale):
    lam = lam_ref[0]
    heads = []
    for hh in range(2):
        a0 = acc_ref[2 * hh]
        a1 = acc_ref[2 * hh + 1]
        o = (a0[0:HEAD_DIM] / a0[HEAD_DIM:HEAD_DIM + 1]
             - lam * (a1[0:HEAD_DIM] / a1[HEAD_DIM:HEAD_DIM + 1]))
        ms2 = jnp.mean(o * o, axis=0, keepdims=True)
        heads.append(o * lax.rsqrt(ms2 + EPS))
    out = jnp.concatenate(heads, axis=0).T
    o_ref[0] = (out * (g_ref[...] * out_scale)).astype(o_ref.dtype)


def _diff(bounded, lam, score_bound, slopes2, q_t, k, vt, subln_pair, bsz, s, lambda_init):
    scalars = jnp.concatenate([lam, score_bound]).astype(F32)
    if bounded:
        body, name, tq = _diff_bounded_kernel, "diff_bounded", DIFF_TQ_FIXED
        q_spec = pl.BlockSpec((1, 1, 2, LANES, tq), lambda b, hp, qi: (b, hp, 0, 0, qi))
        k_spec = pl.BlockSpec((1, 1, 2, s, 2 * LANES), lambda b, hp, qi: (b, hp, 0, 0, 0))
        rhs_scratch = [pltpu.VMEM((4, 2 * LANES, tq), F8)]
        table_scratch = [pltpu.VMEM((2, 2 + tq // DIFF_TK, DIFF_TK, tq), F32)]
        stage_scratch = [pltpu.VMEM((2, 4, 2 * DIFF_TK, tq), BF16)]
    else:
        body, name, tq = _diff_kernel, "diff_online", DIFF_TQ
        k = k.reshape(bsz, s, BRANCH_W)
        q_spec = pl.BlockSpec((1, 1, LANES, tq), lambda b, hp, qi: (b, hp, 0, qi))
        k_spec = pl.BlockSpec((1, s, LANES), lambda b, hp, qi: (b, 0, hp))
        rhs_scratch = [pltpu.VMEM((4, LANES, tq), BF16)]
        table_scratch = [pltpu.VMEM((2, DIFF_TK, tq), F32)]
        stage_scratch = []
    return pl.pallas_call(
        functools.partial(body, out_scale=1.0 - lambda_init),
        grid=(bsz, HEAD_PAIRS, s // tq),
        in_specs=[
            pl.BlockSpec(memory_space=pltpu.SMEM),
            pl.BlockSpec(memory_space=pltpu.SMEM),
            q_spec,
            k_spec,
            pl.BlockSpec((1, 1, 2, V_ROWS, s), lambda b, hp, qi: (b, hp, 0, 0, 0)),
            pl.BlockSpec((1, LANES), lambda b, hp, qi: (0, 0)),
        ],
        out_specs=pl.BlockSpec((1, tq, LANES), lambda b, hp, qi: (b, qi, hp)),
        out_shape=jax.ShapeDtypeStruct((bsz, s, BRANCH_W), BF16),
        scratch_shapes=rhs_scratch + table_scratch + [pltpu.VMEM((4, V_ROWS, tq), F32)] + stage_scratch,
        compiler_params=_cparams(("arbitrary",) * 3),
        name=name,
    )(scalars, slopes2, q_t, k, vt, subln_pair).reshape(bsz * s, BRANCH_W)


def _silu(x):
    return x * jax.nn.sigmoid(x)


def _mix_kernel(x_ref, au_ref, av_ref, ag_ref, bg_ref, cg_ref, din_ref, db_ref, dc_ref, dg_ref,
                din_p_ref, dc_p_ref, din_n_ref, dc_n_ref,
                o1_ref, o4_ref, o16_ref, l1_ref, l4_ref, l16_ref, oc_ref,
                sgu_ref, ws_ref, bs_ref, cw_ref, wout_ref, y_ref, *stage_refs, tiles_per_seq, bounded):
    i = pl.program_id(0)
    t = x_ref.shape[0]
    first_head = lax.broadcasted_iota(jnp.int32, (CHUNK, LANES), 1) < HEAD_DIM

    if bounded:
        for stage, blk, dilation in zip(stage_refs, (o4_ref, o16_ref, l4_ref, l16_ref), (4, 16, 4, 16)):
            for r in range(dilation):
                for cb in range(HEAD_PAIRS):
                    stage[cb, pl.ds(r, t // dilation, stride=dilation), :] = (
                        blk[0, r, :, cb * LANES:(cb + 1) * LANES].astype(F32))

    z = dc_ref[...].astype(F32) * din_ref[...].astype(F32)
    seq_pos = i % tiles_per_seq
    z_prev = (dc_p_ref[...].astype(F32) * din_p_ref[...].astype(F32))[HALO - 1:HALO, :]
    z_next = (dc_n_ref[...].astype(F32) * din_n_ref[...].astype(F32))[0:1, :]
    z_prev = jnp.where(seq_pos == 0, 0.0, z_prev)
    z_next = jnp.where(seq_pos == tiles_per_seq - 1, 0.0, z_next)
    row = lax.broadcasted_iota(jnp.int32, z.shape, 0)
    z_up = jnp.where(row == 0, z_prev, pltpu.roll(z, 1, axis=0))
    z_dn = jnp.where(row == t - 1, z_next, pltpu.roll(z, t - 1, axis=0))
    cw = cw_ref[...]
    out_d = db_ref[...].astype(F32) * (cw[0:1] * z_up + cw[1:2] * z + cw[2:3] * z_dn)

    for sb in range(t // MIX_SUB):
        rows = slice(sb * MIX_SUB, (sb + 1) * MIX_SUB)

        av = av_ref[rows, :].astype(F32)
        vn = (av * lax.rsqrt(jnp.mean(av * av, axis=-1, keepdims=True) + EPS) * sgu_ref[...]).astype(BF16)
        chunks = []
        for c in range(MIX_SUB // CHUNK):
            vc = vn[c * CHUNK:(c + 1) * CHUNK, :]
            pairs = []
            for hp in range(HEAD_PAIRS):
                vp = vc[:, hp * LANES:(hp + 1) * LANES]
                r0 = jnp.dot(ws_ref[2 * hp], vp, preferred_element_type=F32)
                r1 = jnp.dot(ws_ref[2 * hp + 1], vp, preferred_element_type=F32)
                pairs.append(jnp.where(first_head, r0, r1))
            chunks.append(jnp.concatenate(pairs, axis=1) + bs_ref[...])
        out_a = au_ref[rows, :].astype(F32) * jnp.concatenate(chunks, axis=0)

        if bounded:
            def in_order(stage):
                return jnp.concatenate([stage[cb, rows, :] for cb in range(HEAD_PAIRS)], axis=1)

            num = o1_ref[rows, :].astype(F32) + in_order(stage_refs[0]) + in_order(stage_refs[1])
            den = l1_ref[rows, :] + in_order(stage_refs[2]) + in_order(stage_refs[3])
            out_b = num / den
        else:
            l1, l4, l16 = l1_ref[rows, :], l4_ref[rows, :], l16_ref[rows, :]
            lm = jnp.maximum(jnp.maximum(l1, l4), l16)
            w1, w4, w16 = jnp.exp2(l1 - lm), jnp.exp2(l4 - lm), jnp.exp2(l16 - lm)
            out_b = ((w1 * o1_ref[rows, :].astype(F32) + w4 * o4_ref[rows, :].astype(F32)
                      + w16 * o16_ref[rows, :].astype(F32)) / (w1 + w4 + w16))

        mixed = jnp.concatenate([
            (_silu(ag_ref[rows, :].astype(F32)) * out_a).astype(BF16),
            (_silu(bg_ref[rows, :].astype(F32)) * out_b).astype(BF16),
            (_silu(cg_ref[rows, :].astype(F32)) * oc_ref[rows, :].astype(F32)).astype(BF16),
            (_silu(dg_ref[rows, :].astype(F32)) * out_d[rows]).astype(BF16)], axis=1)
        y_ref[rows, :] = x_ref[rows, :] + jnp.dot(mixed, wout_ref[...], preferred_element_type=F32)


def _mix_out(bounded, x2, proj, dil, oc, sgu_g, ws_bf, bs_tab, conv_w, wout_bf, s):
    n = x2.shape[0]
    tiles_per_seq = s // MIX_T
    halo_per_tile = MIX_T // HALO
    n_halo = n // HALO

    def piece(p):
        return pl.BlockSpec((MIX_T, BRANCH_W), lambda i, p=p: (i, p))

    def halo_prev(p):
        return pl.BlockSpec((HALO, BRANCH_W), lambda i, p=p: (jnp.maximum(i * halo_per_tile - 1, 0), p))

    def halo_next(p):
        return pl.BlockSpec((HALO, BRANCH_W),
                            lambda i, p=p: (jnp.minimum((i + 1) * halo_per_tile, n_halo - 1), p))

    def tokens():
        return pl.BlockSpec((MIX_T, BRANCH_W), lambda i: (i, 0))

    def const(shape):
        return pl.BlockSpec(shape, lambda i: (0,) * len(shape))

    def residues(d):
        return pl.BlockSpec((1, d, MIX_T // d, BRANCH_W),
                            lambda i: (i // tiles_per_seq, 0, i % tiles_per_seq, 0))

    (o1, l1), (o4, l4), (o16, l16) = dil
    if bounded:
        dil_specs = [tokens(), residues(4), residues(16)] * 2
        scratch = [pltpu.VMEM((HEAD_PAIRS, MIX_T, LANES), F32)] * 4
    else:
        dil_specs = [tokens()] * 6
        scratch = []
    return pl.pallas_call(
        functools.partial(_mix_kernel, tiles_per_seq=tiles_per_seq, bounded=bounded),
        grid=(n // MIX_T,),
        in_specs=[pl.BlockSpec((MIX_T, D_MODEL), lambda i: (i, 0)),
                  piece(P_AU), piece(P_AV), piece(P_AG), piece(P_BG), piece(P_CG),
                  piece(P_DIN), piece(P_DB), piece(P_DC), piece(P_DG),
                  halo_prev(P_DIN), halo_prev(P_DC), halo_next(P_DIN), halo_next(P_DC),
                  *dil_specs, tokens(),
                  const((1, BRANCH_W)), const((GROUP_HEADS, CHUNK, CHUNK)), const((CHUNK, BRANCH_W)),
                  const((3, BRANCH_W)), const((MIX_W, D_MODEL))],
        out_specs=pl.BlockSpec((MIX_T, D_MODEL), lambda i: (i, 0)),
        out_shape=jax.ShapeDtypeStruct((n, D_MODEL), F32),
        scratch_shapes=scratch,
        compiler_params=_cparams(("arbitrary",)),
        name="mixout_bounded" if bounded else "mixout",
    )(x2, *([proj] * 13), o1, o4, o16, l1, l4, l16, oc,
      sgu_g.reshape(1, BRANCH_W), ws_bf, bs_tab, conv_w, wout_bf)


def _layer(x2, bsz, s, l, norm_g, w_in, sgu_g, w_s, b_s, qn_b, kn_b, qn_c, kn_c,
           lam_q1, lam_k1, lam_q2, lam_k2, subln_g, conv_w, w_out):
    slopes2 = (2.0 ** (-8.0 * jnp.arange(1, GROUP_HEADS + 1, dtype=F32) / GROUP_HEADS)) * LOG2E
    lambda_init = 0.8 - 0.6 * math.exp(-0.3 * l)
    lam = (jnp.exp(jnp.sum(lam_q1 * lam_k1)) - jnp.exp(jnp.sum(lam_q2 * lam_k2)) + lambda_init).reshape(1)

    proj = _inproj(x2, norm_g, w_in.astype(BF16))
    gqb = (jnp.tile(qn_b, GROUP_HEADS) * (LOG2E / math.sqrt(HEAD_DIM))).reshape(1, BRANCH_W)
    gkb = jnp.tile(kn_b, GROUP_HEADS).reshape(1, BRANCH_W)
    gqc = (jnp.tile(qn_c, 2 * GROUP_HEADS) * (LOG2E / math.sqrt(DIFF_QK_DIM))).reshape(1, BRANCH_W)
    gkc = jnp.tile(kn_c, 2 * GROUP_HEADS).reshape(1, BRANCH_W)
    bqn, bkn, kcn, qt, vt, q4, k4, v4, q16, k16, v16, k8, q8 = _prep(proj, bsz, s, gqb, gkb, gqc, gkc)

    bound_b = (jnp.max(jnp.abs(qn_b)) * jnp.max(jnp.abs(kn_b)) * (LOG2E * math.sqrt(HEAD_DIM))).reshape(1)
    bound_c = (jnp.max(jnp.abs(qn_c)) * jnp.max(jnp.abs(kn_c)) * (LOG2E * math.sqrt(DIFF_QK_DIM))).reshape(1)
    subln_pair = jnp.tile(subln_g, 2).reshape(1, LANES)
    bs_tab = jnp.repeat(b_s.T, HEAD_DIM, axis=1)
    ws_bf, wout_bf = w_s.astype(BF16), w_out.astype(BF16)

    def fixed_shift_path():
        operands = {1: (bqn.reshape(bsz, 1, s, BRANCH_W), bkn.reshape(bsz, 1, s, BRANCH_W),
                        proj.reshape(bsz, 1, s, PROJ_W), P_BV * HEAD_PAIRS),
                    4: (q4, k4, v4, 0), 16: (q16, k16, v16, 0)}
        dil = []
        for _, d in DIL_PATTERNS:
            qd, kd, vd, v_off = operands[d]
            num, den = _dilated_bounded(_dilated_tables(slopes2, bound_b[0], d), qd, kd, vd, v_off, bsz, s, d)
            if d == 1:
                num, den = num.reshape(bsz * s, BRANCH_W), den.reshape(bsz * s, BRANCH_W)
            dil.append((num, den))
        oc = _diff(True, lam, bound_c, slopes2, q8, k8, vt, subln_pair, bsz, s, lambda_init)
        return _mix_out(True, x2, proj, dil, oc, sgu_g, ws_bf, bs_tab, conv_w, wout_bf, s)

    def running_max_path():
        dil = [_dilated(slopes2, bqn, bkn, proj, bsz, s, d) for (_, d) in DIL_PATTERNS]
        oc = _diff(False, lam, bound_c, slopes2, qt, kcn, vt, subln_pair, bsz, s, lambda_init)
        return _mix_out(False, x2, proj, dil, oc, sgu_g, ws_bf, bs_tab, conv_w, wout_bf, s)

    use_fixed_shift = jnp.maximum(bound_b[0], bound_c[0]) <= MAX_FIXED_SHIFT
    return lax.cond(use_fixed_shift, fixed_shift_path, running_max_path)


def _trunk(x, params):
    bsz, s, _ = x.shape
    x2 = x.reshape(bsz * s, D_MODEL)
    depth = params[0].shape[0]
    for l in range(depth):
        x2 = _layer(x2, bsz, s, l, *[p[l] for p in params])
    return x2.reshape(bsz, s, D_MODEL)


def kernel(x_prompt, x_sample, norm_g, w_in, sgu_g, w_s, b_s, qn_b, kn_b, qn_c, kn_c,
           lam_q1, lam_k1, lam_q2, lam_k2, subln_g, conv_w, w_out):
    params = (norm_g, w_in, sgu_g, w_s, b_s, qn_b, kn_b, qn_c, kn_c,
              lam_q1, lam_k1, lam_q2, lam_k2, subln_g, conv_w, w_out)
    return (_trunk(x_prompt, params), _trunk(x_sample, params))
```

```python
import functools
import math

import jax
import jax.numpy as jnp
from jax import lax
from jax.experimental import pallas as pl
from jax.experimental.pallas import tpu as pltpu

F32 = jnp.float32
BF16 = jnp.bfloat16
F8 = jnp.float8_e4m3fn

D_MODEL = 1024
HEAD_DIM = 64
GROUP_HEADS = 6
BRANCH_W = GROUP_HEADS * HEAD_DIM
N_PIECES = 15
PROJ_W = N_PIECES * BRANCH_W
MIX_W = 4 * BRANCH_W
CHUNK = 128
DIL_PATTERNS = ((128, 1), (512, 4), (2048, 16))
N_SIDE = 64
DIFF_QK_DIM = HEAD_DIM // 2
EPS = 1e-6
NEG = -1e30
LOG2E = math.log2(math.e)

LANES = 128
HEAD_PAIRS = BRANCH_W // LANES
V_ROWS = 80

P_AU, P_AV, P_AG, P_BQ, P_BK, P_BV, P_BG, P_CQ, P_CK, P_CV, P_CG, P_DIN, P_DB, P_DC, P_DG = range(15)

IN_TM = 1024
IN_TN = 1920
PREP_T = 512
PREP_SUB = 128
MIX_T = 512
MIX_SUB = 512
HALO = 16
DIL_TQ = 128
DIL_WIN = DIL_TQ + 2 * N_SIDE
DIL_ROWS = 2048
DIL_GROUP = 16
DIFF_TQ = 256
DIFF_TQ_FIXED = 512
DIFF_SUBTILES = 2
DIFF_TK = 256
VMEM_LIMIT = 56 * 1024 * 1024
MAX_FIXED_SHIFT = 40.0


def _cparams(sem):
    return pltpu.CompilerParams(dimension_semantics=sem, vmem_limit_bytes=VMEM_LIMIT)


def _inproj_kernel(x_ref, g_ref, w_ref, o_ref):
    x = x_ref[...]
    ms = jnp.mean(x * x, axis=-1, keepdims=True)
    h = (x * lax.rsqrt(ms + EPS) * g_ref[...]).astype(BF16)
    o_ref[...] = jnp.dot(h, w_ref[...], preferred_element_type=F32).astype(BF16)


def _inproj(x2, norm_g, w_in_bf):
    n = x2.shape[0]
    return pl.pallas_call(
        _inproj_kernel,
        grid=(PROJ_W // IN_TN, n // IN_TM),
        in_specs=[
            pl.BlockSpec((IN_TM, D_MODEL), lambda j, i: (i, 0)),
            pl.BlockSpec((1, D_MODEL), lambda j, i: (0, 0)),
            pl.BlockSpec((D_MODEL, IN_TN), lambda j, i: (0, j)),
        ],
        out_specs=pl.BlockSpec((IN_TM, IN_TN), lambda j, i: (i, j)),
        out_shape=jax.ShapeDtypeStruct((n, PROJ_W), BF16),
        compiler_params=_cparams(("arbitrary", "arbitrary")),
        name="inproj",
    )(x2, norm_g.reshape(1, D_MODEL), w_in_bf)


def _group_rms(x, e_ref, width):
    x2 = x * x
    hi = x2.astype(BF16)
    lo = (x2 - hi.astype(F32)).astype(BF16)
    e = e_ref[...]
    ss = jnp.dot(hi, e, preferred_element_type=F32) + jnp.dot(lo, e, preferred_element_type=F32)
    return x * lax.rsqrt(ss * (1.0 / width) + EPS)


def _prep_kernel(bq_ref, bk_ref, bv_ref, cq_ref, ck_ref, cv_ref, gqb_ref, gkb_ref, gqc_ref, gkc_ref,
                 e64_ref, e32_ref, bqn_ref, bkn_ref, kcn_ref, qt_ref, vt_ref,
                 q4_ref, k4_ref, v4_ref, q16_ref, k16_ref, v16_ref, k8_ref, q8_ref, sq_ref, sk_ref, sv_ref):
    t = bq_ref.shape[0]
    low_lanes = lax.broadcasted_iota(jnp.int32, (PREP_SUB, LANES), 1) < HEAD_DIM
    row = lax.broadcasted_iota(jnp.int32, (V_ROWS - HEAD_DIM, PREP_SUB), 0)
    ones_pad = jnp.where(row == 0, 1.0, 0.0).astype(BF16)
    for cb in range(HEAD_PAIRS):
        sl = slice(cb * LANES, (cb + 1) * LANES)
        for r0 in range(0, t, PREP_SUB):
            rs = slice(r0, r0 + PREP_SUB)
            qb = _group_rms(bq_ref[rs, sl].astype(F32), e64_ref, HEAD_DIM) * gqb_ref[:, sl]
            kb = _group_rms(bk_ref[rs, sl].astype(F32), e64_ref, HEAD_DIM) * gkb_ref[:, sl]
            sq_ref[cb, rs, :] = qb
            sk_ref[cb, rs, :] = kb
            sv_ref[cb, rs, :] = bv_ref[rs, sl].astype(F32)
            bqn_ref[rs, sl] = qb.astype(BF16)
            bkn_ref[rs, sl] = kb.astype(BF16)
            kn = _group_rms(ck_ref[rs, sl].astype(F32), e32_ref, DIFF_QK_DIM) * gkc_ref[:, sl]
            kcn_ref[rs, sl] = kn.astype(BF16)
            k_hi = kn.astype(F8).astype(F32)
            k_lo_swapped = pltpu.roll((kn - k_hi).astype(F8).astype(F32), HEAD_DIM, axis=1)
            zero = jnp.zeros_like(kn)
            k8_ref[0, cb, 0, rs, 0:LANES] = jnp.where(low_lanes, k_hi, k_lo_swapped).astype(F8)
            k8_ref[0, cb, 0, rs, LANES:2 * LANES] = jnp.where(low_lanes, k_hi, zero).astype(F8)
            k8_ref[0, cb, 1, rs, 0:LANES] = jnp.where(low_lanes, k_lo_swapped, k_hi).astype(F8)
            k8_ref[0, cb, 1, rs, LANES:2 * LANES] = jnp.where(low_lanes, zero, k_hi).astype(F8)
            qn_t = (_group_rms(cq_ref[rs, sl].astype(F32), e32_ref, DIFF_QK_DIM) * gqc_ref[:, sl]).T
            qt_ref[0, cb, :, rs] = qn_t.astype(BF16)
            q_hi = qn_t.astype(F8)
            q8_ref[0, cb, 0, :, rs] = q_hi
            q8_ref[0, cb, 1, :, rs] = (qn_t - q_hi.astype(F32)).astype(F8)
            vt = cv_ref[rs, sl].astype(F32).T.astype(BF16)
            for hh in range(2):
                vt_ref[0, cb, hh, 0:HEAD_DIM, rs] = vt[hh * HEAD_DIM:(hh + 1) * HEAD_DIM, :]
                vt_ref[0, cb, hh, HEAD_DIM:V_ROWS, rs] = ones_pad
    for stage, out4, out16 in ((sq_ref, q4_ref, q16_ref), (sk_ref, k4_ref, k16_ref), (sv_ref, v4_ref, v16_ref)):
        for dilation, out in ((4, out4), (16, out16)):
            for r in range(dilation):
                for cb in range(HEAD_PAIRS):
                    out[0, r, :, cb * LANES:(cb + 1) * LANES] = (
                        stage[cb, pl.ds(r, t // dilation, stride=dilation), :].astype(BF16))


def _block_diag_ones(group):
    idx = jnp.arange(LANES) // group
    return (idx[:, None] == idx[None, :]).astype(BF16)


def _prep(proj, bsz, s, gqb, gkb, gqc, gkc):
    n = proj.shape[0]
    tiles_per_seq = s // PREP_T

    def piece(p):
        return pl.BlockSpec((PREP_T, BRANCH_W), lambda i, p=p: (i, p))

    def vec():
        return pl.BlockSpec((1, BRANCH_W), lambda i: (0, 0))

    def tokens():
        return pl.BlockSpec((PREP_T, BRANCH_W), lambda i: (i, 0))

    def residues(d):
        return pl.BlockSpec((1, d, PREP_T // d, BRANCH_W),
                            lambda i: (i // tiles_per_seq, 0, i % tiles_per_seq, 0))

    return pl.pallas_call(
        _prep_kernel,
        grid=(n // PREP_T,),
        in_specs=[piece(P_BQ), piece(P_BK), piece(P_BV), piece(P_CQ), piece(P_CK), piece(P_CV),
                  vec(), vec(), vec(), vec(),
                  pl.BlockSpec((LANES, LANES), lambda i: (0, 0)),
                  pl.BlockSpec((LANES, LANES), lambda i: (0, 0))],
        out_specs=[tokens(), tokens(), tokens(),
                   pl.BlockSpec((1, HEAD_PAIRS, LANES, PREP_T),
                                lambda i: (i // tiles_per_seq, 0, 0, i % tiles_per_seq)),
                   pl.BlockSpec((1, HEAD_PAIRS, 2, V_ROWS, PREP_T),
                                lambda i: (i // tiles_per_seq, 0, 0, 0, i % tiles_per_seq))]
                  + [residues(d) for d in (4, 4, 4, 16, 16, 16)]
                  + [pl.BlockSpec((1, HEAD_PAIRS, 2, PREP_T, 2 * LANES),
                                  lambda i: (i // tiles_per_seq, 0, 0, i % tiles_per_seq, 0)),
                     pl.BlockSpec((1, HEAD_PAIRS, 2, LANES, PREP_T),
                                  lambda i: (i // tiles_per_seq, 0, 0, 0, i % tiles_per_seq))],
        out_shape=[jax.ShapeDtypeStruct((n, BRANCH_W), BF16),
                   jax.ShapeDtypeStruct((n, BRANCH_W), BF16),
                   jax.ShapeDtypeStruct((n, BRANCH_W), BF16),
                   jax.ShapeDtypeStruct((bsz, HEAD_PAIRS, LANES, s), BF16),
                   jax.ShapeDtypeStruct((bsz, HEAD_PAIRS, 2, V_ROWS, s), BF16)]
                  + [jax.ShapeDtypeStruct((bsz, d, s // d, BRANCH_W), BF16) for d in (4, 4, 4, 16, 16, 16)]
                  + [jax.ShapeDtypeStruct((bsz, HEAD_PAIRS, 2, s, 2 * LANES), F8),
                     jax.ShapeDtypeStruct((bsz, HEAD_PAIRS, 2, LANES, s), F8)],
        scratch_shapes=[pltpu.VMEM((HEAD_PAIRS, PREP_T, LANES), F32)] * 3,
        compiler_params=_cparams(("arbitrary",)),
        name="prep",
    )(proj, proj, proj, proj, proj, proj, gqb, gkb, gqc, gkc,
      _block_diag_ones(HEAD_DIM), _block_diag_ones(DIFF_QK_DIM))


def _dilated_kernel(slope_ref, q_ref, k_ref, v_ref, o_ref, lse_ref, *, dilation, seq_sub):
    hp = pl.program_id(2)
    qi = pl.program_id(3)
    tqb = q_ref.shape[1]
    lane = lax.broadcasted_iota(jnp.int32, (DIL_TQ, LANES), 1)
    first_head = lane < HEAD_DIM
    rel0 = (lax.broadcasted_iota(jnp.int32, (DIL_TQ, DIL_WIN), 1)
            - lax.broadcasted_iota(jnp.int32, (DIL_TQ, DIL_WIN), 0))

    def sub_block(sb, carry):
        r0 = pl.multiple_of(sb * DIL_TQ, DIL_TQ)
        i0 = qi * tqb + r0
        ws = pl.multiple_of(jnp.clip(i0 - N_SIDE, 0, seq_sub - DIL_WIN), N_SIDE)
        kw = k_ref[0, pl.ds(ws, DIL_WIN), :]
        vw = v_ref[0, pl.ds(ws, DIL_WIN), :]
        q = q_ref[0, pl.ds(r0, DIL_TQ), :]
        dist = jnp.abs(rel0 + (ws - i0))
        valid = dist <= N_SIDE
        distf = dist.astype(F32)
        outs, lses = [], []
        for hh in range(2):
            slope = slope_ref[2 * hp + hh] * float(dilation)
            qm = jnp.where(first_head if hh == 0 else jnp.logical_not(first_head), q, jnp.zeros_like(q))
            sc = lax.dot_general(qm, kw, (((1,), (1,)), ((), ())), preferred_element_type=F32)
            sc = jnp.where(valid, sc - slope * distf, NEG)
            m = jnp.max(sc, axis=-1, keepdims=True)
            p = jnp.exp2(sc - m)
            den = jnp.sum(p, axis=-1, keepdims=True)
            o = jnp.dot(p.astype(BF16), vw, preferred_element_type=F32) / den
            outs.append(o)
            lses.append(jnp.broadcast_to(m + jnp.log2(den), (DIL_TQ, LANES)))
        o_ref[0, pl.ds(r0, DIL_TQ), :] = jnp.where(first_head, outs[0], outs[1]).astype(o_ref.dtype)
        lse_ref[0, pl.ds(r0, DIL_TQ), :] = jnp.where(first_head, lses[0], lses[1])
        return carry

    lax.fori_loop(0, tqb // DIL_TQ, sub_block, 0)


def _dilated(slopes2, bqn, bkn, proj, bsz, s, dilation):
    seq_sub = s // dilation
    tqb = min(seq_sub, 1024)
    q3 = bqn.reshape(bsz, seq_sub, dilation * BRANCH_W)
    k3 = bkn.reshape(bsz, seq_sub, dilation * BRANCH_W)
    v3 = proj.reshape(bsz, seq_sub, dilation * PROJ_W)
    blocks_q = BRANCH_W // LANES
    blocks_p = PROJ_W // LANES
    v_off = P_BV * blocks_q

    def tile():
        return pl.BlockSpec((1, tqb, LANES), lambda b, r, hp, qi: (b, qi, r * blocks_q + hp))

    o, lse = pl.pallas_call(
        functools.partial(_dilated_kernel, dilation=dilation, seq_sub=seq_sub),
        grid=(bsz, dilation, HEAD_PAIRS, seq_sub // tqb),
        in_specs=[
            pl.BlockSpec(memory_space=pltpu.SMEM),
            tile(),
            pl.BlockSpec((1, seq_sub, LANES), lambda b, r, hp, qi: (b, 0, r * blocks_q + hp)),
            pl.BlockSpec((1, seq_sub, LANES), lambda b, r, hp, qi: (b, 0, r * blocks_p + v_off + hp)),
        ],
        out_specs=[tile(), tile()],
        out_shape=[jax.ShapeDtypeStruct(q3.shape, BF16), jax.ShapeDtypeStruct(q3.shape, F32)],
        compiler_params=_cparams(("arbitrary",) * 4),
        name=f"dilated{dilation}",
    )(slopes2, q3, k3, v3)
    return o.reshape(bsz * s, BRANCH_W), lse.reshape(bsz * s, BRANCH_W)


def _dilated_bounded_kernel(tab_ref, q_ref, k_ref, v_ref, num_ref, den_ref, p_ref, *, seq_sub):
    qi = pl.program_id(3)
    n_res, tqb = q_ref.shape[1], q_ref.shape[2]
    sub_per_res = tqb // DIL_TQ
    first_head_q = lax.broadcasted_iota(jnp.int32, (DIL_TQ, LANES), 1) < HEAD_DIM
    v_lane = lax.broadcasted_iota(jnp.int32, (DIL_WIN, 2 * LANES), 1)
    value_lanes = (v_lane < HEAD_DIM) | (v_lane >= 2 * LANES - HEAD_DIM)

    def window(sb):
        res = sb // sub_per_res
        r0 = pl.multiple_of((sb % sub_per_res) * DIL_TQ, DIL_TQ)
        i0 = qi * tqb + r0
        ws = pl.multiple_of(jnp.clip(i0 - N_SIDE, 0, seq_sub - DIL_WIN), N_SIDE)
        variant = jnp.where(i0 == 0, 0, jnp.where(i0 == seq_sub - DIL_TQ, 2, 1))
        return res, r0, ws, variant

    def probabilities(u):
        for half in range(DIL_GROUP):
            res, r0, ws, variant = window(DIL_GROUP * u + half)
            kw = k_ref[0, res, pl.ds(ws, DIL_WIN), :]
            q = q_ref[0, res, pl.ds(r0, DIL_TQ), :]
            zero = jnp.zeros_like(q)
            q2 = jnp.concatenate([jnp.where(first_head_q, q, zero), jnp.where(first_head_q, zero, q)], axis=0)
            sc = lax.dot_general(q2, kw, (((1,), (1,)), ((), ())), preferred_element_type=F32)
            for hh in range(2):
                p_ref[half, hh * DIL_TQ:(hh + 1) * DIL_TQ, :] = jnp.exp2(
                    sc[hh * DIL_TQ:(hh + 1) * DIL_TQ] + tab_ref[hh, variant]).astype(BF16)

    def outputs(u):
        for half in range(DIL_GROUP):
            res, r0, ws, _ = window(DIL_GROUP * u + half)
            vw = v_ref[0, res, pl.ds(ws, DIL_WIN), :]
            vw2 = jnp.concatenate([vw, vw], axis=1)
            vcat = jnp.where(value_lanes, vw2, jnp.ones_like(vw2))
            a = jnp.dot(p_ref[half], vcat, preferred_element_type=F32)
            a0 = a[0:DIL_TQ, 0:LANES]
            a1 = a[DIL_TQ:2 * DIL_TQ, LANES:2 * LANES]
            num_ref[0, res, pl.ds(r0, DIL_TQ), :] = jnp.where(first_head_q, a0, a1).astype(num_ref.dtype)
            den_ref[0, res, pl.ds(r0, DIL_TQ), :] = pltpu.roll(jnp.where(first_head_q, a1, a0), HEAD_DIM, axis=1)

    n_groups = n_res * sub_per_res // DIL_GROUP
    probabilities(0)

    def step(u, carry):
        outputs(u - 1)
        probabilities(u)
        return carry

    lax.fori_loop(1, n_groups, step, 0)
    outputs(n_groups - 1)


def _dilated_tables(slopes2, bound, dilation):
    rel = ((jnp.arange(DIL_WIN)[None, :] - jnp.arange(DIL_TQ)[:, None])[None]
           - jnp.array([0, N_SIDE, 2 * N_SIDE])[:, None, None])
    dist = jnp.abs(rel).astype(F32)
    bias = -(slopes2 * float(dilation))[:, None, None, None] * dist[None] - bound
    return jnp.where(dist[None] <= N_SIDE, bias, NEG).astype(F32)


def _dilated_bounded(tables, q4d, k4d, v4d, v_block_offset, bsz, s, dilation):
    seq_sub = s // dilation
    tqb = min(seq_sub, DIL_ROWS)
    n_res = min(dilation, DIL_ROWS // tqb)

    def tile():
        return pl.BlockSpec((1, n_res, tqb, LANES), lambda b, r, hp, qi: (b, r, qi, hp))

    out_shape = (bsz, dilation, seq_sub, BRANCH_W)
    return pl.pallas_call(
        functools.partial(_dilated_bounded_kernel, seq_sub=seq_sub),
        grid=(bsz, dilation // n_res, HEAD_PAIRS, seq_sub // tqb),
        in_specs=[
            pl.BlockSpec((2, 3, DIL_TQ, DIL_WIN), lambda b, r, hp, qi: (hp, 0, 0, 0)),
            tile(),
            pl.BlockSpec((1, n_res, seq_sub, LANES), lambda b, r, hp, qi: (b, r, 0, hp)),
            pl.BlockSpec((1, n_res, seq_sub, LANES), lambda b, r, hp, qi: (b, r, 0, v_block_offset + hp)),
        ],
        out_specs=[tile(), tile()],
        out_shape=[jax.ShapeDtypeStruct(out_shape, BF16), jax.ShapeDtypeStruct(out_shape, F32)],
        scratch_shapes=[pltpu.VMEM((DIL_GROUP, 2 * DIL_TQ, DIL_WIN), BF16)],
        compiler_params=_cparams(("arbitrary",) * 4),
        name=f"dilated_bounded{dilation}",
    )(tables, q4d, k4d, v4d)


def _diff_kernel(lam_ref, slope_ref, qt_ref, k_ref, vt_ref, g_ref, o_ref, qm_ref, tm_ref, acc_ref,
                 *, out_scale):
    hp = pl.program_id(1)
    qi = pl.program_id(2)
    tq = qt_ref.shape[3]
    tk = DIFF_TK
    nk = k_ref.shape[1] // tk
    i0 = qi * tq

    qt = qt_ref[0, 0]
    row_group = lax.broadcasted_iota(jnp.int32, (LANES, tq), 0) // DIFF_QK_DIM
    for g in range(4):
        qm_ref[g] = jnp.where(row_group == g, qt, jnp.zeros_like(qt))
    dkq = (lax.broadcasted_iota(jnp.int32, (tk, tq), 0)
           - lax.broadcasted_iota(jnp.int32, (tk, tq), 1)).astype(F32)
    for hh in range(2):
        tm_ref[hh] = slope_ref[2 * hp + hh] * dkq
    acc_ref[...] = jnp.zeros_like(acc_ref)

    def key_tile(j, ms, mode):
        koff = pl.multiple_of(j * tk, tk)
        kt = k_ref[0, pl.ds(koff, tk), :]
        tile_dist = jnp.abs(i0 - koff).astype(F32)
        new_ms = []
        for hh in range(2):
            vt = vt_ref[0, 0, hh, :, pl.ds(koff, tk)]
            shift = slope_ref[2 * hp + hh] * tile_dist
            for c in range(2):
                g = 2 * hh + c
                sc = jnp.dot(kt, qm_ref[g], preferred_element_type=F32)
                if mode < 0:
                    sc = sc + tm_ref[hh]
                elif mode > 0:
                    sc = sc - tm_ref[hh]
                else:
                    sc = sc - jnp.abs(tm_ref[hh])
                m_old = ms[g]
                m_new = jnp.maximum(m_old, jnp.max(sc, axis=0, keepdims=True) - shift)
                p = jnp.exp2(sc - (m_new + shift))
                alpha = jnp.exp2(m_old - m_new)
                acc_ref[g] = alpha * acc_ref[g] + jnp.dot(vt, p.astype(BF16), preferred_element_type=F32)
                new_ms.append(m_new)
        return tuple(new_ms)

    ms = tuple(jnp.full((1, tq), NEG, F32) for _ in range(4))
    ms = lax.fori_loop(0, qi, lambda j, m: key_tile(j, m, -1), ms)
    ms = key_tile(qi, ms, 0)
    lax.fori_loop(qi + 1, nk, lambda j, m: key_tile(j, m, 1), ms)
    _diff_finish(lam_ref, g_ref, o_ref, acc_ref, out_scale)


def _diff_bounded_kernel(lam_ref, slope_ref, qt_ref, k_ref, vt_ref, g_ref, o_ref, qm_ref, tm_ref, acc_ref,
                         p_ref, *, out_scale):
    hp = pl.program_id(1)
    qi = pl.program_id(2)
    n_sub = DIFF_SUBTILES
    tq = qt_ref.shape[4] // n_sub
    tk = DIFF_TK
    tiles_per_q = tq // tk
    nk = k_ref.shape[3] // tk
    centre = 0.5 * lam_ref[1]

    zeros = jnp.zeros((DIFF_QK_DIM, tq), qt_ref.dtype)
    for t in range(n_sub):
        q_hi = qt_ref[0, 0, 0, :, t * tq:(t + 1) * tq]
        q_lo = qt_ref[0, 0, 1, :, t * tq:(t + 1) * tq]
        for g in range(4):
            hh, c = divmod(g, 2)
            hi_block, lo_block, hi2_block = ((0, 2, 4), (2, 0, 6))[hh]
            rows = slice(g * DIFF_QK_DIM, (g + 1) * DIFF_QK_DIM)
            blocks = [zeros] * (2 * LANES // DIFF_QK_DIM)
            blocks[hi_block + c], blocks[lo_block + c], blocks[hi2_block + c] = q_hi[rows], q_hi[rows], q_lo[rows]
            qm_ref[t, g] = jnp.concatenate(blocks, axis=0)
    @pl.when(qi == 0)
    def _():
        dkq = (lax.broadcasted_iota(jnp.int32, (tk, tq), 0)
               - lax.broadcasted_iota(jnp.int32, (tk, tq), 1)).astype(F32)
        for hh in range(2):
            slope = slope_ref[2 * hp + hh]
            tm = slope * dkq
            tm_ref[hh, 0] = tm
            tm_ref[hh, 1] = -tm
            for r in range(tiles_per_q):
                origin = float(r * tk)
                tm_ref[hh, 2 + r] = slope * (origin - jnp.abs(dkq + origin))

    acc_ref[...] = jnp.zeros_like(acc_ref)

    def stage(scores=None, values=None):
        for hh in range(2):
            slope = slope_ref[2 * hp + hh]
            for c in range(2):
                g = 2 * hh + c
                parts = []
                for half in range(2):
                    if values is not None:
                        t, u, slot = values
                        koff = pl.multiple_of((2 * u + half) * tk, tk)
                        parts.append(jnp.dot(vt_ref[0, 0, hh, :, pl.ds(koff, tk)], p_ref[slot, half, g],
                                             preferred_element_type=F32))
                    if scores is not None:
                        t, u, slot = scores
                        j = 2 * u + half
                        koff = pl.multiple_of(j * tk, tk)
                        q_tile = qi * n_sub + t
                        shift = -(slope * jnp.abs(q_tile * tq - koff).astype(F32) + centre)
                        first = tiles_per_q * q_tile
                        table = jnp.where(j < first, 0, jnp.where(j >= first + tiles_per_q, 1, 2 + j - first))
                        sc = jnp.dot(k_ref[0, 0, hh, pl.ds(koff, tk), :], qm_ref[t, g],
                                     preferred_element_type=F32)
                        p_ref[slot, half, g] = jnp.exp2((sc + tm_ref[hh, table] + shift).astype(BF16))
                if parts:
                    acc_ref[values[0], g] += parts[0] + parts[1]

    n_pairs = nk // 2
    stage(scores=(0, 0, 0))
    for t in range(n_sub):
        def trip(w, carry, t=t):
            stage(scores=(t, 2 * w + 1, 1), values=(t, 2 * w, 0))
            stage(scores=(t, 2 * w + 2, 0), values=(t, 2 * w + 1, 1))
            return carry

        lax.fori_loop(0, n_pairs // 2 - 1, trip, 0)
        stage(scores=(t, n_pairs - 1, 1), values=(t, n_pairs - 2, 0))
        stage(scores=(t + 1, 0, 0) if t + 1 < n_sub else None, values=(t, n_pairs - 1, 1))
        _diff_finish(lam_ref, g_ref, o_ref, acc_ref.at[t], out_scale, slice(t * tq, (t + 1) * tq))


def _diff_finish(lam_ref, g_ref, o_ref, acc_ref, out_scale, out_rows=slice(None)):
    lam = lam_ref[0]
    heads = []
    for hh in range(2):
        a0 = acc_ref[2 * hh]
        a1 = acc_ref[2 * hh + 1]
        o = (a0[0:HEAD_DIM] / a0[HEAD_DIM:HEAD_DIM + 1]
             - lam * (a1[0:HEAD_DIM] / a1[HEAD_DIM:HEAD_DIM + 1]))
        ms2 = jnp.mean(o * o, axis=0, keepdims=True)
        heads.append(o * lax.rsqrt(ms2 + EPS))
    out = jnp.concatenate(heads, axis=0).T
    o_ref[0, out_rows, :] = (out * (g_ref[...] * out_scale)).astype(o_ref.dtype)


def _diff(bounded, lam, score_bound, slopes2, q_t, k, vt, subln_pair, bsz, s, lambda_init):
    scalars = jnp.concatenate([lam, score_bound]).astype(F32)
    if bounded:
        body, name = _diff_bounded_kernel, "diff_bounded"
        sub = DIFF_TQ_FIXED
        tq = DIFF_SUBTILES * sub
        q_spec = pl.BlockSpec((1, 1, 2, LANES, tq), lambda b, hp, qi: (b, hp, 0, 0, qi))
        k_spec = pl.BlockSpec((1, 1, 2, s, 2 * LANES), lambda b, hp, qi: (b, hp, 0, 0, 0))
        scratch = [pltpu.VMEM((DIFF_SUBTILES, 4, 2 * LANES, sub), F8),
                   pltpu.VMEM((2, 2 + sub // DIFF_TK, DIFF_TK, sub), F32),
                   pltpu.VMEM((DIFF_SUBTILES, 4, V_ROWS, sub), F32),
                   pltpu.VMEM((2, 2, 4, DIFF_TK, sub), BF16)]
    else:
        body, name, tq = _diff_kernel, "diff_online", DIFF_TQ
        k = k.reshape(bsz, s, BRANCH_W)
        q_spec = pl.BlockSpec((1, 1, LANES, tq), lambda b, hp, qi: (b, hp, 0, qi))
        k_spec = pl.BlockSpec((1, s, LANES), lambda b, hp, qi: (b, 0, hp))
        scratch = [pltpu.VMEM((4, LANES, tq), BF16), pltpu.VMEM((2, DIFF_TK, tq), F32),
                   pltpu.VMEM((4, V_ROWS, tq), F32)]
    return pl.pallas_call(
        functools.partial(body, out_scale=1.0 - lambda_init),
        grid=(bsz, HEAD_PAIRS, s // tq),
        in_specs=[
            pl.BlockSpec(memory_space=pltpu.SMEM),
            pl.BlockSpec(memory_space=pltpu.SMEM),
            q_spec,
            k_spec,
            pl.BlockSpec((1, 1, 2, V_ROWS, s), lambda b, hp, qi: (b, hp, 0, 0, 0)),
            pl.BlockSpec((1, LANES), lambda b, hp, qi: (0, 0)),
        ],
        out_specs=pl.BlockSpec((1, tq, LANES), lambda b, hp, qi: (b, qi, hp)),
        out_shape=jax.ShapeDtypeStruct((bsz, s, BRANCH_W), BF16),
        scratch_shapes=scratch,
        compiler_params=_cparams(("arbitrary",) * 3),
        name=name,
    )(scalars, slopes2, q_t, k, vt, subln_pair).reshape(bsz * s, BRANCH_W)


def _silu(x):
    return x * jax.nn.sigmoid(x)


def _mix_kernel(x_ref, au_ref, av_ref, ag_ref, bg_ref, cg_ref, din_ref, db_ref, dc_ref, dg_ref,
                din_p_ref, dc_p_ref, din_n_ref, dc_n_ref,
                o1_ref, o4_ref, o16_ref, l1_ref, l4_ref, l16_ref, oc_ref,
                sgu_ref, ws_ref, bs_ref, cw_ref, wout_ref, y_ref, *stage_refs, tiles_per_seq, bounded):
    i = pl.program_id(0)
    t = x_ref.shape[0]
    first_head = lax.broadcasted_iota(jnp.int32, (CHUNK, LANES), 1) < HEAD_DIM

    if bounded:
        for stage, blk, dilation in zip(stage_refs, (o4_ref, o16_ref, l4_ref, l16_ref), (4, 16, 4, 16)):
            for r in range(dilation):
                for cb in range(HEAD_PAIRS):
                    stage[cb, pl.ds(r, t // dilation, stride=dilation), :] = (
                        blk[0, r, :, cb * LANES:(cb + 1) * LANES].astype(F32))

    z = dc_ref[...].astype(F32) * din_ref[...].astype(F32)
    seq_pos = i % tiles_per_seq
    z_prev = (dc_p_ref[...].astype(F32) * din_p_ref[...].astype(F32))[HALO - 1:HALO, :]
    z_next = (dc_n_ref[...].astype(F32) * din_n_ref[...].astype(F32))[0:1, :]
    z_prev = jnp.where(seq_pos == 0, 0.0, z_prev)
    z_next = jnp.where(seq_pos == tiles_per_seq - 1, 0.0, z_next)
    row = lax.broadcasted_iota(jnp.int32, z.shape, 0)
    z_up = jnp.where(row == 0, z_prev, pltpu.roll(z, 1, axis=0))
    z_dn = jnp.where(row == t - 1, z_next, pltpu.roll(z, t - 1, axis=0))
    cw = cw_ref[...]
    out_d = db_ref[...].astype(F32) * (cw[0:1] * z_up + cw[1:2] * z + cw[2:3] * z_dn)

    for sb in range(t // MIX_SUB):
        rows = slice(sb * MIX_SUB, (sb + 1) * MIX_SUB)

        av = av_ref[rows, :].astype(F32)
        vn = (av * lax.rsqrt(jnp.mean(av * av, axis=-1, keepdims=True) + EPS) * sgu_ref[...]).astype(BF16)
        chunks = []
        for c in range(MIX_SUB // CHUNK):
            vc = vn[c * CHUNK:(c + 1) * CHUNK, :]
            pairs = []
            for hp in range(HEAD_PAIRS):
                vp = vc[:, hp * LANES:(hp + 1) * LANES]
                r0 = jnp.dot(ws_ref[2 * hp], vp, preferred_element_type=F32)
                r1 = jnp.dot(ws_ref[2 * hp + 1], vp, preferred_element_type=F32)
                pairs.append(jnp.where(first_head, r0, r1))
            chunks.append(jnp.concatenate(pairs, axis=1) + bs_ref[...])
        out_a = au_ref[rows, :].astype(F32) * jnp.concatenate(chunks, axis=0)

        if bounded:
            def in_order(stage):
                return jnp.concatenate([stage[cb, rows, :] for cb in range(HEAD_PAIRS)], axis=1)

            num = o1_ref[rows, :].astype(F32) + in_order(stage_refs[0]) + in_order(stage_refs[1])
            den = l1_ref[rows, :] + in_order(stage_refs[2]) + in_order(stage_refs[3])
            out_b = num / den
        else:
            l1, l4, l16 = l1_ref[rows, :], l4_ref[rows, :], l16_ref[rows, :]
            lm = jnp.maximum(jnp.maximum(l1, l4), l16)
            w1, w4, w16 = jnp.exp2(l1 - lm), jnp.exp2(l4 - lm), jnp.exp2(l16 - lm)
            out_b = ((w1 * o1_ref[rows, :].astype(F32) + w4 * o4_ref[rows, :].astype(F32)
                      + w16 * o16_ref[rows, :].astype(F32)) / (w1 + w4 + w16))

        mixed = jnp.concatenate([
            (_silu(ag_ref[rows, :].astype(F32)) * out_a).astype(BF16),
            (_silu(bg_ref[rows, :].astype(F32)) * out_b).astype(BF16),
            (_silu(cg_ref[rows, :].astype(F32)) * oc_ref[rows, :].astype(F32)).astype(BF16),
            (_silu(dg_ref[rows, :].astype(F32)) * out_d[rows]).astype(BF16)], axis=1)
        y_ref[rows, :] = x_ref[rows, :] + jnp.dot(mixed, wout_ref[...], preferred_element_type=F32)


def _mix_out(bounded, x2, proj, dil, oc, sgu_g, ws_bf, bs_tab, conv_w, wout_bf, s):
    n = x2.shape[0]
    tiles_per_seq = s // MIX_T
    halo_per_tile = MIX_T // HALO
    n_halo = n // HALO

    def piece(p):
        return pl.BlockSpec((MIX_T, BRANCH_W), lambda i, p=p: (i, p))

    def halo_prev(p):
        return pl.BlockSpec((HALO, BRANCH_W), lambda i, p=p: (jnp.maximum(i * halo_per_tile - 1, 0), p))

    def halo_next(p):
        return pl.BlockSpec((HALO, BRANCH_W),
                            lambda i, p=p: (jnp.minimum((i + 1) * halo_per_tile, n_halo - 1), p))

    def tokens():
        return pl.BlockSpec((MIX_T, BRANCH_W), lambda i: (i, 0))

    def const(shape):
        return pl.BlockSpec(shape, lambda i: (0,) * len(shape))

    def residues(d):
        return pl.BlockSpec((1, d, MIX_T // d, BRANCH_W),
                            lambda i: (i // tiles_per_seq, 0, i % tiles_per_seq, 0))

    (o1, l1), (o4, l4), (o16, l16) = dil
    if bounded:
        dil_specs = [tokens(), residues(4), residues(16)] * 2
        scratch = [pltpu.VMEM((HEAD_PAIRS, MIX_T, LANES), F32)] * 4
    else:
        dil_specs = [tokens()] * 6
        scratch = []
    return pl.pallas_call(
        functools.partial(_mix_kernel, tiles_per_seq=tiles_per_seq, bounded=bounded),
        grid=(n // MIX_T,),
        in_specs=[pl.BlockSpec((MIX_T, D_MODEL), lambda i: (i, 0)),
                  piece(P_AU), piece(P_AV), piece(P_AG), piece(P_BG), piece(P_CG),
                  piece(P_DIN), piece(P_DB), piece(P_DC), piece(P_DG),
                  halo_prev(P_DIN), halo_prev(P_DC), halo_next(P_DIN), halo_next(P_DC),
                  *dil_specs, tokens(),
                  const((1, BRANCH_W)), const((GROUP_HEADS, CHUNK, CHUNK)), const((CHUNK, BRANCH_W)),
                  const((3, BRANCH_W)), const((MIX_W, D_MODEL))],
        out_specs=pl.BlockSpec((MIX_T, D_MODEL), lambda i: (i, 0)),
        out_shape=jax.ShapeDtypeStruct((n, D_MODEL), F32),
        scratch_shapes=scratch,
        compiler_params=_cparams(("arbitrary",)),
        name="mixout_bounded" if bounded else "mixout",
    )(x2, *([proj] * 13), o1, o4, o16, l1, l4, l16, oc,
      sgu_g.reshape(1, BRANCH_W), ws_bf, bs_tab, conv_w, wout_bf)


def _layer(x2, bsz, s, l, norm_g, w_in, sgu_g, w_s, b_s, qn_b, kn_b, qn_c, kn_c,
           lam_q1, lam_k1, lam_q2, lam_k2, subln_g, conv_w, w_out):
    slopes2 = (2.0 ** (-8.0 * jnp.arange(1, GROUP_HEADS + 1, dtype=F32) / GROUP_HEADS)) * LOG2E
    lambda_init = 0.8 - 0.6 * math.exp(-0.3 * l)
    lam = (jnp.exp(jnp.sum(lam_q1 * lam_k1)) - jnp.exp(jnp.sum(lam_q2 * lam_k2)) + lambda_init).reshape(1)

    proj = _inproj(x2, norm_g, w_in.astype(BF16))
    gqb = (jnp.tile(qn_b, GROUP_HEADS) * (LOG2E / math.sqrt(HEAD_DIM))).reshape(1, BRANCH_W)
    gkb = jnp.tile(kn_b, GROUP_HEADS).reshape(1, BRANCH_W)
    gqc = (jnp.tile(qn_c, 2 * GROUP_HEADS) * (LOG2E / math.sqrt(DIFF_QK_DIM))).reshape(1, BRANCH_W)
    gkc = jnp.tile(kn_c, 2 * GROUP_HEADS).reshape(1, BRANCH_W)
    bqn, bkn, kcn, qt, vt, q4, k4, v4, q16, k16, v16, k8, q8 = _prep(proj, bsz, s, gqb, gkb, gqc, gkc)

    bound_b = (jnp.max(jnp.abs(qn_b)) * jnp.max(jnp.abs(kn_b)) * (LOG2E * math.sqrt(HEAD_DIM))).reshape(1)
    bound_c = (jnp.max(jnp.abs(qn_c)) * jnp.max(jnp.abs(kn_c)) * (LOG2E * math.sqrt(DIFF_QK_DIM))).reshape(1)
    subln_pair = jnp.tile(subln_g, 2).reshape(1, LANES)
    bs_tab = jnp.repeat(b_s.T, HEAD_DIM, axis=1)
    ws_bf, wout_bf = w_s.astype(BF16), w_out.astype(BF16)

    def fixed_shift_path():
        operands = {1: (bqn.reshape(bsz, 1, s, BRANCH_W), bkn.reshape(bsz, 1, s, BRANCH_W),
                        proj.reshape(bsz, 1, s, PROJ_W), P_BV * HEAD_PAIRS),
                    4: (q4, k4, v4, 0), 16: (q16, k16, v16, 0)}
        dil = []
        for _, d in DIL_PATTERNS:
            qd, kd, vd, v_off = operands[d]
            num, den = _dilated_bounded(_dilated_tables(slopes2, bound_b[0], d), qd, kd, vd, v_off, bsz, s, d)
            if d == 1:
                num, den = num.reshape(bsz * s, BRANCH_W), den.reshape(bsz * s, BRANCH_W)
            dil.append((num, den))
        oc = _diff(True, lam, bound_c, slopes2, q8, k8, vt, subln_pair, bsz, s, lambda_init)
        return _mix_out(True, x2, proj, dil, oc, sgu_g, ws_bf, bs_tab, conv_w, wout_bf, s)

    def running_max_path():
        dil = [_dilated(slopes2, bqn, bkn, proj, bsz, s, d) for (_, d) in DIL_PATTERNS]
        oc = _diff(False, lam, bound_c, slopes2, qt, kcn, vt, subln_pair, bsz, s, lambda_init)
        return _mix_out(False, x2, proj, dil, oc, sgu_g, ws_bf, bs_tab, conv_w, wout_bf, s)

    use_fixed_shift = jnp.maximum(bound_b[0], bound_c[0]) <= MAX_FIXED_SHIFT
    return lax.cond(use_fixed_shift, fixed_shift_path, running_max_path)


def _trunk(x, params):
    bsz, s, _ = x.shape
    x2 = x.reshape(bsz * s, D_MODEL)
    depth = params[0].shape[0]
    for l in range(depth):
        x2 = _layer(x2, bsz, s, l, *[p[l] for p in params])
    return x2.reshape(bsz, s, D_MODEL)


def kernel(x_prompt, x_sample, norm_g, w_in, sgu_g, w_s, b_s, qn_b, kn_b, qn_c, kn_c,
           lam_q1, lam_k1, lam_q2, lam_k2, subln_g, conv_w, w_out):
    params = (norm_g, w_in, sgu_g, w_s, b_s, qn_b, kn_b, qn_c, kn_c,
              lam_q1, lam_k1, lam_q2, lam_k2, subln_g, conv_w, w_out)
    return (_trunk(x_prompt, params), _trunk(x_sample, params))
```

```python
import functools
import math

import jax
import jax.numpy as jnp
from jax import lax
from jax.experimental import pallas as pl
from jax.experimental.pallas import tpu as pltpu

F32 = jnp.float32
BF16 = jnp.bfloat16
F8 = jnp.float8_e4m3fn

D_MODEL = 1024
HEAD_DIM = 64
GROUP_HEADS = 6
BRANCH_W = GROUP_HEADS * HEAD_DIM
N_PIECES = 15
PROJ_W = N_PIECES * BRANCH_W
MIX_W = 4 * BRANCH_W
CHUNK = 128
DIL_PATTERNS = ((128, 1), (512, 4), (2048, 16))
N_SIDE = 64
DIFF_QK_DIM = HEAD_DIM // 2
EPS = 1e-6
NEG = -1e30
LOG2E = math.log2(math.e)

LANES = 128
HEAD_PAIRS = BRANCH_W // LANES
V_ROWS = 80

P_AU, P_AV, P_AG, P_BQ, P_BK, P_BV, P_BG, P_CQ, P_CK, P_CV, P_CG, P_DIN, P_DB, P_DC, P_DG = range(15)

IN_TM = 1024
IN_TN = 1920
PREP_T = 512
PREP_SUB = 128
MIX_T = 512
MIX_SUB = 512
HALO = 16
DIL_TQ = 128
DIL_WIN = DIL_TQ + 2 * N_SIDE
DIL_ROWS = 2048
DIL_GROUP = 16
DIFF_TQ = 256
DIFF_TQ_FIXED = 512
DIFF_SUBTILES = 2
DIFF_TRIP = 4
DIFF_TK = 256
VMEM_LIMIT = 56 * 1024 * 1024
MAX_FIXED_SHIFT = 40.0


def _cparams(sem):
    return pltpu.CompilerParams(dimension_semantics=sem, vmem_limit_bytes=VMEM_LIMIT)


def _inproj_kernel(x_ref, g_ref, w_ref, o_ref):
    x = x_ref[...]
    ms = jnp.mean(x * x, axis=-1, keepdims=True)
    h = (x * lax.rsqrt(ms + EPS) * g_ref[...]).astype(BF16)
    o_ref[...] = jnp.dot(h, w_ref[...], preferred_element_type=F32).astype(BF16)


def _inproj(x2, norm_g, w_in_bf):
    n = x2.shape[0]
    return pl.pallas_call(
        _inproj_kernel,
        grid=(PROJ_W // IN_TN, n // IN_TM),
        in_specs=[
            pl.BlockSpec((IN_TM, D_MODEL), lambda j, i: (i, 0)),
            pl.BlockSpec((1, D_MODEL), lambda j, i: (0, 0)),
            pl.BlockSpec((D_MODEL, IN_TN), lambda j, i: (0, j)),
        ],
        out_specs=pl.BlockSpec((IN_TM, IN_TN), lambda j, i: (i, j)),
        out_shape=jax.ShapeDtypeStruct((n, PROJ_W), BF16),
        compiler_params=_cparams(("arbitrary", "arbitrary")),
        name="inproj",
    )(x2, norm_g.reshape(1, D_MODEL), w_in_bf)


def _group_rms(x, e_ref, width):
    x2 = x * x
    hi = x2.astype(BF16)
    lo = (x2 - hi.astype(F32)).astype(BF16)
    e = e_ref[...]
    ss = jnp.dot(hi, e, preferred_element_type=F32) + jnp.dot(lo, e, preferred_element_type=F32)
    return x * lax.rsqrt(ss * (1.0 / width) + EPS)


def _prep_kernel(bq_ref, bk_ref, bv_ref, cq_ref, ck_ref, cv_ref, gqb_ref, gkb_ref, gqc_ref, gkc_ref,
                 e64_ref, e32_ref, bqn_ref, bkn_ref, kcn_ref, qt_ref, vt_ref,
                 q4_ref, k4_ref, v4_ref, q16_ref, k16_ref, v16_ref, k8_ref, q8_ref, sq_ref, sk_ref, sv_ref):
    t = bq_ref.shape[0]
    low_lanes = lax.broadcasted_iota(jnp.int32, (PREP_SUB, LANES), 1) < HEAD_DIM
    row = lax.broadcasted_iota(jnp.int32, (V_ROWS - HEAD_DIM, PREP_SUB), 0)
    ones_pad = jnp.where(row == 0, 1.0, 0.0).astype(BF16)
    for cb in range(HEAD_PAIRS):
        sl = slice(cb * LANES, (cb + 1) * LANES)
        for r0 in range(0, t, PREP_SUB):
            rs = slice(r0, r0 + PREP_SUB)
            qb = _group_rms(bq_ref[rs, sl].astype(F32), e64_ref, HEAD_DIM) * gqb_ref[:, sl]
            kb = _group_rms(bk_ref[rs, sl].astype(F32), e64_ref, HEAD_DIM) * gkb_ref[:, sl]
            sq_ref[cb, rs, :] = qb
            sk_ref[cb, rs, :] = kb
            sv_ref[cb, rs, :] = bv_ref[rs, sl].astype(F32)
            bqn_ref[rs, sl] = qb.astype(BF16)
            bkn_ref[rs, sl] = kb.astype(BF16)
            kn = _group_rms(ck_ref[rs, sl].astype(F32), e32_ref, DIFF_QK_DIM) * gkc_ref[:, sl]
            kcn_ref[rs, sl] = kn.astype(BF16)
            k_hi = kn.astype(F8).astype(F32)
            k_lo_swapped = pltpu.roll((kn - k_hi).astype(F8).astype(F32), HEAD_DIM, axis=1)
            zero = jnp.zeros_like(kn)
            k8_ref[0, cb, 0, rs, 0:LANES] = jnp.where(low_lanes, k_hi, k_lo_swapped).astype(F8)
            k8_ref[0, cb, 0, rs, LANES:2 * LANES] = jnp.where(low_lanes, k_hi, zero).astype(F8)
            k8_ref[0, cb, 1, rs, 0:LANES] = jnp.where(low_lanes, k_lo_swapped, k_hi).astype(F8)
            k8_ref[0, cb, 1, rs, LANES:2 * LANES] = jnp.where(low_lanes, zero, k_hi).astype(F8)
            qn_t = (_group_rms(cq_ref[rs, sl].astype(F32), e32_ref, DIFF_QK_DIM) * gqc_ref[:, sl]).T
            qt_ref[0, cb, :, rs] = qn_t.astype(BF16)
            q_hi = qn_t.astype(F8)
            q8_ref[0, cb, 0, :, rs] = q_hi
            q8_ref[0, cb, 1, :, rs] = (qn_t - q_hi.astype(F32)).astype(F8)
            vt = cv_ref[rs, sl].astype(F32).T.astype(BF16)
            for hh in range(2):
                vt_ref[0, cb, hh, 0:HEAD_DIM, rs] = vt[hh * HEAD_DIM:(hh + 1) * HEAD_DIM, :]
                vt_ref[0, cb, hh, HEAD_DIM:V_ROWS, rs] = ones_pad
    for stage, out4, out16 in ((sq_ref, q4_ref, q16_ref), (sk_ref, k4_ref, k16_ref), (sv_ref, v4_ref, v16_ref)):
        for dilation, out in ((4, out4), (16, out16)):
            for r in range(dilation):
                for cb in range(HEAD_PAIRS):
                    out[0, r, :, cb * LANES:(cb + 1) * LANES] = (
                        stage[cb, pl.ds(r, t // dilation, stride=dilation), :].astype(BF16))


def _block_diag_ones(group):
    idx = jnp.arange(LANES) // group
    return (idx[:, None] == idx[None, :]).astype(BF16)


def _prep(proj, bsz, s, gqb, gkb, gqc, gkc):
    n = proj.shape[0]
    tiles_per_seq = s // PREP_T

    def piece(p):
        return pl.BlockSpec((PREP_T, BRANCH_W), lambda i, p=p: (i, p))

    def vec():
        return pl.BlockSpec((1, BRANCH_W), lambda i: (0, 0))

    def tokens():
        return pl.BlockSpec((PREP_T, BRANCH_W), lambda i: (i, 0))

    def residues(d):
        return pl.BlockSpec((1, d, PREP_T // d, BRANCH_W),
                            lambda i: (i // tiles_per_seq, 0, i % tiles_per_seq, 0))

    return pl.pallas_call(
        _prep_kernel,
        grid=(n // PREP_T,),
        in_specs=[piece(P_BQ), piece(P_BK), piece(P_BV), piece(P_CQ), piece(P_CK), piece(P_CV),
                  vec(), vec(), vec(), vec(),
                  pl.BlockSpec((LANES, LANES), lambda i: (0, 0)),
                  pl.BlockSpec((LANES, LANES), lambda i: (0, 0))],
        out_specs=[tokens(), tokens(), tokens(),
                   pl.BlockSpec((1, HEAD_PAIRS, LANES, PREP_T),
                                lambda i: (i // tiles_per_seq, 0, 0, i % tiles_per_seq)),
                   pl.BlockSpec((1, HEAD_PAIRS, 2, V_ROWS, PREP_T),
                                lambda i: (i // tiles_per_seq, 0, 0, 0, i % tiles_per_seq))]
                  + [residues(d) for d in (4, 4, 4, 16, 16, 16)]
                  + [pl.BlockSpec((1, HEAD_PAIRS, 2, PREP_T, 2 * LANES),
                                  lambda i: (i // tiles_per_seq, 0, 0, i % tiles_per_seq, 0)),
                     pl.BlockSpec((1, HEAD_PAIRS, 2, LANES, PREP_T),
                                  lambda i: (i // tiles_per_seq, 0, 0, 0, i % tiles_per_seq))],
        out_shape=[jax.ShapeDtypeStruct((n, BRANCH_W), BF16),
                   jax.ShapeDtypeStruct((n, BRANCH_W), BF16),
                   jax.ShapeDtypeStruct((n, BRANCH_W), BF16),
                   jax.ShapeDtypeStruct((bsz, HEAD_PAIRS, LANES, s), BF16),
                   jax.ShapeDtypeStruct((bsz, HEAD_PAIRS, 2, V_ROWS, s), BF16)]
                  + [jax.ShapeDtypeStruct((bsz, d, s // d, BRANCH_W), BF16) for d in (4, 4, 4, 16, 16, 16)]
                  + [jax.ShapeDtypeStruct((bsz, HEAD_PAIRS, 2, s, 2 * LANES), F8),
                     jax.ShapeDtypeStruct((bsz, HEAD_PAIRS, 2, LANES, s), F8)],
        scratch_shapes=[pltpu.VMEM((HEAD_PAIRS, PREP_T, LANES), F32)] * 3,
        compiler_params=_cparams(("arbitrary",)),
        name="prep",
    )(proj, proj, proj, proj, proj, proj, gqb, gkb, gqc, gkc,
      _block_diag_ones(HEAD_DIM), _block_diag_ones(DIFF_QK_DIM))


def _dilated_kernel(slope_ref, q_ref, k_ref, v_ref, o_ref, lse_ref, *, dilation, seq_sub):
    hp = pl.program_id(2)
    qi = pl.program_id(3)
    tqb = q_ref.shape[1]
    lane = lax.broadcasted_iota(jnp.int32, (DIL_TQ, LANES), 1)
    first_head = lane < HEAD_DIM
    rel0 = (lax.broadcasted_iota(jnp.int32, (DIL_TQ, DIL_WIN), 1)
            - lax.broadcasted_iota(jnp.int32, (DIL_TQ, DIL_WIN), 0))

    def sub_block(sb, carry):
        r0 = pl.multiple_of(sb * DIL_TQ, DIL_TQ)
        i0 = qi * tqb + r0
        ws = pl.multiple_of(jnp.clip(i0 - N_SIDE, 0, seq_sub - DIL_WIN), N_SIDE)
        kw = k_ref[0, pl.ds(ws, DIL_WIN), :]
        vw = v_ref[0, pl.ds(ws, DIL_WIN), :]
        q = q_ref[0, pl.ds(r0, DIL_TQ), :]
        dist = jnp.abs(rel0 + (ws - i0))
        valid = dist <= N_SIDE
        distf = dist.astype(F32)
        outs, lses = [], []
        for hh in range(2):
            slope = slope_ref[2 * hp + hh] * float(dilation)
            qm = jnp.where(first_head if hh == 0 else jnp.logical_not(first_head), q, jnp.zeros_like(q))
            sc = lax.dot_general(qm, kw, (((1,), (1,)), ((), ())), preferred_element_type=F32)
            sc = jnp.where(valid, sc - slope * distf, NEG)
            m = jnp.max(sc, axis=-1, keepdims=True)
            p = jnp.exp2(sc - m)
            den = jnp.sum(p, axis=-1, keepdims=True)
            o = jnp.dot(p.astype(BF16), vw, preferred_element_type=F32) / den
            outs.append(o)
            lses.append(jnp.broadcast_to(m + jnp.log2(den), (DIL_TQ, LANES)))
        o_ref[0, pl.ds(r0, DIL_TQ), :] = jnp.where(first_head, outs[0], outs[1]).astype(o_ref.dtype)
        lse_ref[0, pl.ds(r0, DIL_TQ), :] = jnp.where(first_head, lses[0], lses[1])
        return carry

    lax.fori_loop(0, tqb // DIL_TQ, sub_block, 0)


def _dilated(slopes2, bqn, bkn, proj, bsz, s, dilation):
    seq_sub = s // dilation
    tqb = min(seq_sub, 1024)
    q3 = bqn.reshape(bsz, seq_sub, dilation * BRANCH_W)
    k3 = bkn.reshape(bsz, seq_sub, dilation * BRANCH_W)
    v3 = proj.reshape(bsz, seq_sub, dilation * PROJ_W)
    blocks_q = BRANCH_W // LANES
    blocks_p = PROJ_W // LANES
    v_off = P_BV * blocks_q

    def tile():
        return pl.BlockSpec((1, tqb, LANES), lambda b, r, hp, qi: (b, qi, r * blocks_q + hp))

    o, lse = pl.pallas_call(
        functools.partial(_dilated_kernel, dilation=dilation, seq_sub=seq_sub),
        grid=(bsz, dilation, HEAD_PAIRS, seq_sub // tqb),
        in_specs=[
            pl.BlockSpec(memory_space=pltpu.SMEM),
            tile(),
            pl.BlockSpec((1, seq_sub, LANES), lambda b, r, hp, qi: (b, 0, r * blocks_q + hp)),
            pl.BlockSpec((1, seq_sub, LANES), lambda b, r, hp, qi: (b, 0, r * blocks_p + v_off + hp)),
        ],
        out_specs=[tile(), tile()],
        out_shape=[jax.ShapeDtypeStruct(q3.shape, BF16), jax.ShapeDtypeStruct(q3.shape, F32)],
        compiler_params=_cparams(("arbitrary",) * 4),
        name=f"dilated{dilation}",
    )(slopes2, q3, k3, v3)
    return o.reshape(bsz * s, BRANCH_W), lse.reshape(bsz * s, BRANCH_W)


def _dilated_bounded_kernel(tab_ref, q_ref, k_ref, v_ref, num_ref, den_ref, p_ref, *, seq_sub):
    qi = pl.program_id(3)
    n_res, tqb = q_ref.shape[1], q_ref.shape[2]
    sub_per_res = tqb // DIL_TQ
    first_head_q = lax.broadcasted_iota(jnp.int32, (DIL_TQ, LANES), 1) < HEAD_DIM
    v_lane = lax.broadcasted_iota(jnp.int32, (DIL_WIN, 2 * LANES), 1)
    value_lanes = (v_lane < HEAD_DIM) | (v_lane >= 2 * LANES - HEAD_DIM)

    def window(sb):
        res = sb // sub_per_res
        r0 = pl.multiple_of((sb % sub_per_res) * DIL_TQ, DIL_TQ)
        i0 = qi * tqb + r0
        ws = pl.multiple_of(jnp.clip(i0 - N_SIDE, 0, seq_sub - DIL_WIN), N_SIDE)
        variant = jnp.where(i0 == 0, 0, jnp.where(i0 == seq_sub - DIL_TQ, 2, 1))
        return res, r0, ws, variant

    def probabilities(u):
        for half in range(DIL_GROUP):
            res, r0, ws, variant = window(DIL_GROUP * u + half)
            kw = k_ref[0, res, pl.ds(ws, DIL_WIN), :]
            q = q_ref[0, res, pl.ds(r0, DIL_TQ), :]
            zero = jnp.zeros_like(q)
            q2 = jnp.concatenate([jnp.where(first_head_q, q, zero), jnp.where(first_head_q, zero, q)], axis=0)
            sc = lax.dot_general(q2, kw, (((1,), (1,)), ((), ())), preferred_element_type=F32)
            for hh in range(2):
                p_ref[half, hh * DIL_TQ:(hh + 1) * DIL_TQ, :] = jnp.exp2(
                    sc[hh * DIL_TQ:(hh + 1) * DIL_TQ] + tab_ref[hh, variant]).astype(BF16)

    def outputs(u):
        for half in range(DIL_GROUP):
            res, r0, ws, _ = window(DIL_GROUP * u + half)
            vw = v_ref[0, res, pl.ds(ws, DIL_WIN), :]
            vw2 = jnp.concatenate([vw, vw], axis=1)
            vcat = jnp.where(value_lanes, vw2, jnp.ones_like(vw2))
            a = jnp.dot(p_ref[half], vcat, preferred_element_type=F32)
            a0 = a[0:DIL_TQ, 0:LANES]
            a1 = a[DIL_TQ:2 * DIL_TQ, LANES:2 * LANES]
            num_ref[0, res, pl.ds(r0, DIL_TQ), :] = jnp.where(first_head_q, a0, a1).astype(num_ref.dtype)
            den_ref[0, res, pl.ds(r0, DIL_TQ), :] = pltpu.roll(jnp.where(first_head_q, a1, a0), HEAD_DIM, axis=1)

    n_groups = n_res * sub_per_res // DIL_GROUP
    probabilities(0)

    def step(u, carry):
        outputs(u - 1)
        probabilities(u)
        return carry

    lax.fori_loop(1, n_groups, step, 0)
    outputs(n_groups - 1)


def _dilated_tables(slopes2, bound, dilation):
    rel = ((jnp.arange(DIL_WIN)[None, :] - jnp.arange(DIL_TQ)[:, None])[None]
           - jnp.array([0, N_SIDE, 2 * N_SIDE])[:, None, None])
    dist = jnp.abs(rel).astype(F32)
    bias = -(slopes2 * float(dilation))[:, None, None, None] * dist[None] - bound
    return jnp.where(dist[None] <= N_SIDE, bias, NEG).astype(F32)


def _dilated_bounded(tables, q4d, k4d, v4d, v_block_offset, bsz, s, dilation):
    seq_sub = s // dilation
    tqb = min(seq_sub, DIL_ROWS)
    n_res = min(dilation, DIL_ROWS // tqb)

    def tile():
        return pl.BlockSpec((1, n_res, tqb, LANES), lambda b, r, hp, qi: (b, r, qi, hp))

    out_shape = (bsz, dilation, seq_sub, BRANCH_W)
    return pl.pallas_call(
        functools.partial(_dilated_bounded_kernel, seq_sub=seq_sub),
        grid=(bsz, dilation // n_res, HEAD_PAIRS, seq_sub // tqb),
        in_specs=[
            pl.BlockSpec((2, 3, DIL_TQ, DIL_WIN), lambda b, r, hp, qi: (hp, 0, 0, 0)),
            tile(),
            pl.BlockSpec((1, n_res, seq_sub, LANES), lambda b, r, hp, qi: (b, r, 0, hp)),
            pl.BlockSpec((1, n_res, seq_sub, LANES), lambda b, r, hp, qi: (b, r, 0, v_block_offset + hp)),
        ],
        out_specs=[tile(), tile()],
        out_shape=[jax.ShapeDtypeStruct(out_shape, BF16), jax.ShapeDtypeStruct(out_shape, F32)],
        scratch_shapes=[pltpu.VMEM((DIL_GROUP, 2 * DIL_TQ, DIL_WIN), BF16)],
        compiler_params=_cparams(("arbitrary",) * 4),
        name=f"dilated_bounded{dilation}",
    )(tables, q4d, k4d, v4d)


def _diff_kernel(lam_ref, slope_ref, qt_ref, k_ref, vt_ref, g_ref, o_ref, qm_ref, tm_ref, acc_ref,
                 *, out_scale):
    hp = pl.program_id(1)
    qi = pl.program_id(2)
    tq = qt_ref.shape[3]
    tk = DIFF_TK
    nk = k_ref.shape[1] // tk
    i0 = qi * tq

    qt = qt_ref[0, 0]
    row_group = lax.broadcasted_iota(jnp.int32, (LANES, tq), 0) // DIFF_QK_DIM
    for g in range(4):
        qm_ref[g] = jnp.where(row_group == g, qt, jnp.zeros_like(qt))
    dkq = (lax.broadcasted_iota(jnp.int32, (tk, tq), 0)
           - lax.broadcasted_iota(jnp.int32, (tk, tq), 1)).astype(F32)
    for hh in range(2):
        tm_ref[hh] = slope_ref[2 * hp + hh] * dkq
    acc_ref[...] = jnp.zeros_like(acc_ref)

    def key_tile(j, ms, mode):
        koff = pl.multiple_of(j * tk, tk)
        kt = k_ref[0, pl.ds(koff, tk), :]
        tile_dist = jnp.abs(i0 - koff).astype(F32)
        new_ms = []
        for hh in range(2):
            vt = vt_ref[0, 0, hh, :, pl.ds(koff, tk)]
            shift = slope_ref[2 * hp + hh] * tile_dist
            for c in range(2):
                g = 2 * hh + c
                sc = jnp.dot(kt, qm_ref[g], preferred_element_type=F32)
                if mode < 0:
                    sc = sc + tm_ref[hh]
                elif mode > 0:
                    sc = sc - tm_ref[hh]
                else:
                    sc = sc - jnp.abs(tm_ref[hh])
                m_old = ms[g]
                m_new = jnp.maximum(m_old, jnp.max(sc, axis=0, keepdims=True) - shift)
                p = jnp.exp2(sc - (m_new + shift))
                alpha = jnp.exp2(m_old - m_new)
                acc_ref[g] = alpha * acc_ref[g] + jnp.dot(vt, p.astype(BF16), preferred_element_type=F32)
                new_ms.append(m_new)
        return tuple(new_ms)

    ms = tuple(jnp.full((1, tq), NEG, F32) for _ in range(4))
    ms = lax.fori_loop(0, qi, lambda j, m: key_tile(j, m, -1), ms)
    ms = key_tile(qi, ms, 0)
    lax.fori_loop(qi + 1, nk, lambda j, m: key_tile(j, m, 1), ms)
    _diff_finish(lam_ref, g_ref, o_ref, acc_ref, out_scale)


def _diff_bounded_kernel(lam_ref, slope_ref, qt_ref, k_ref, vt_ref, g_ref, o_ref, qm_ref, tm_ref, acc_ref,
                         p_ref, *, out_scale):
    hp = pl.program_id(1)
    qi = pl.program_id(2)
    n_sub = DIFF_SUBTILES
    tq = qt_ref.shape[4] // n_sub
    tk = DIFF_TK
    tiles_per_q = tq // tk
    nk = k_ref.shape[3] // tk
    centre = 0.5 * lam_ref[1]

    zeros = jnp.zeros((DIFF_QK_DIM, tq), qt_ref.dtype)
    for t in range(n_sub):
        q_hi = qt_ref[0, 0, 0, :, t * tq:(t + 1) * tq]
        q_lo = qt_ref[0, 0, 1, :, t * tq:(t + 1) * tq]
        for g in range(4):
            hh, c = divmod(g, 2)
            hi_block, lo_block, hi2_block = ((0, 2, 4), (2, 0, 6))[hh]
            rows = slice(g * DIFF_QK_DIM, (g + 1) * DIFF_QK_DIM)
            blocks = [zeros] * (2 * LANES // DIFF_QK_DIM)
            blocks[hi_block + c], blocks[lo_block + c], blocks[hi2_block + c] = q_hi[rows], q_hi[rows], q_lo[rows]
            qm_ref[t, g] = jnp.concatenate(blocks, axis=0)
    @pl.when(qi == 0)
    def _():
        dkq = (lax.broadcasted_iota(jnp.int32, (tk, tq), 0)
               - lax.broadcasted_iota(jnp.int32, (tk, tq), 1)).astype(F32)
        for hh in range(2):
            slope = slope_ref[2 * hp + hh]
            tm = slope * dkq
            tm_ref[hh, 0] = tm
            tm_ref[hh, 1] = -tm
            for r in range(tiles_per_q):
                origin = float(r * tk)
                tm_ref[hh, 2 + r] = slope * (origin - jnp.abs(dkq + origin))

    acc_ref[...] = jnp.zeros_like(acc_ref)

    def stage(scores=None, values=None):
        for hh in range(2):
            slope = slope_ref[2 * hp + hh]
            for c in range(2):
                g = 2 * hh + c
                parts = []
                for half in range(2):
                    if values is not None:
                        t, u, slot = values
                        koff = pl.multiple_of((2 * u + half) * tk, tk)
                        parts.append(jnp.dot(vt_ref[0, 0, hh, :, pl.ds(koff, tk)], p_ref[slot, half, g],
                                             preferred_element_type=F32))
                    if scores is not None:
                        t, u, slot = scores
                        j = 2 * u + half
                        koff = pl.multiple_of(j * tk, tk)
                        q_tile = qi * n_sub + t
                        shift = -(slope * jnp.abs(q_tile * tq - koff).astype(F32) + centre)
                        first = tiles_per_q * q_tile
                        table = jnp.where(j < first, 0, jnp.where(j >= first + tiles_per_q, 1, 2 + j - first))
                        sc = jnp.dot(k_ref[0, 0, hh, pl.ds(koff, tk), :], qm_ref[t, g],
                                     preferred_element_type=F32)
                        p_ref[slot, half, g] = jnp.exp2((sc + tm_ref[hh, table] + shift).astype(BF16))
                if parts:
                    acc_ref[values[0], g] += parts[0] + parts[1]

    n_pairs = nk // 2
    stage(scores=(0, 0, 0))
    for t in range(n_sub):
        def steps(first, count, t=t):
            for k in range(count):
                stage(scores=(t, first + k, (1 + k) % 2), values=(t, first + k - 1, k % 2))

        trip_steps = DIFF_TRIP if (n_pairs - 2) // DIFF_TRIP >= 2 else 2

        def trip(w, carry):
            steps(trip_steps * w + 1, trip_steps)
            return carry

        n_trips = (n_pairs - 2) // trip_steps
        lax.fori_loop(0, n_trips, trip, 0)
        steps(trip_steps * n_trips + 1, n_pairs - 2 - trip_steps * n_trips)
        stage(scores=(t, n_pairs - 1, 1), values=(t, n_pairs - 2, 0))
        stage(scores=(t + 1, 0, 0) if t + 1 < n_sub else None, values=(t, n_pairs - 1, 1))
        _diff_finish(lam_ref, g_ref, o_ref, acc_ref.at[t], out_scale, slice(t * tq, (t + 1) * tq))


def _diff_finish(lam_ref, g_ref, o_ref, acc_ref, out_scale, out_rows=slice(None)):
    lam = lam_ref[0]
    heads = []
    for hh in range(2):
        a0 = acc_ref[2 * hh]
        a1 = acc_ref[2 * hh + 1]
        o = (a0[0:HEAD_DIM] / a0[HEAD_DIM:HEAD_DIM + 1]
             - lam * (a1[0:HEAD_DIM] / a1[HEAD_DIM:HEAD_DIM + 1]))
        ms2 = jnp.mean(o * o, axis=0, keepdims=True)
        heads.append(o * lax.rsqrt(ms2 + EPS))
    out = jnp.concatenate(heads, axis=0).T
    o_ref[0, out_rows, :] = (out * (g_ref[...] * out_scale)).astype(o_ref.dtype)


def _diff(bounded, lam, score_bound, slopes2, q_t, k, vt, subln_pair, bsz, s, lambda_init):
    scalars = jnp.concatenate([lam, score_bound]).astype(F32)
    if bounded:
        body, name = _diff_bounded_kernel, "diff_bounded"
        sub = DIFF_TQ_FIXED
        tq = DIFF_SUBTILES * sub
        q_spec = pl.BlockSpec((1, 1, 2, LANES, tq), lambda b, hp, qi: (b, hp, 0, 0, qi))
        k_spec = pl.BlockSpec((1, 1, 2, s, 2 * LANES), lambda b, hp, qi: (b, hp, 0, 0, 0))
        scratch = [pltpu.VMEM((DIFF_SUBTILES, 4, 2 * LANES, sub), F8),
                   pltpu.VMEM((2, 2 + sub // DIFF_TK, DIFF_TK, sub), F32),
                   pltpu.VMEM((DIFF_SUBTILES, 4, V_ROWS, sub), F32),
                   pltpu.VMEM((2, 2, 4, DIFF_TK, sub), BF16)]
    else:
        body, name, tq = _diff_kernel, "diff_online", DIFF_TQ
        k = k.reshape(bsz, s, BRANCH_W)
        q_spec = pl.BlockSpec((1, 1, LANES, tq), lambda b, hp, qi: (b, hp, 0, qi))
        k_spec = pl.BlockSpec((1, s, LANES), lambda b, hp, qi: (b, 0, hp))
        scratch = [pltpu.VMEM((4, LANES, tq), BF16), pltpu.VMEM((2, DIFF_TK, tq), F32),
                   pltpu.VMEM((4, V_ROWS, tq), F32)]
    return pl.pallas_call(
        functools.partial(body, out_scale=1.0 - lambda_init),
        grid=(bsz, HEAD_PAIRS, s // tq),
        in_specs=[
            pl.BlockSpec(memory_space=pltpu.SMEM),
            pl.BlockSpec(memory_space=pltpu.SMEM),
            q_spec,
            k_spec,
            pl.BlockSpec((1, 1, 2, V_ROWS, s), lambda b, hp, qi: (b, hp, 0, 0, 0)),
            pl.BlockSpec((1, LANES), lambda b, hp, qi: (0, 0)),
        ],
        out_specs=pl.BlockSpec((1, tq, LANES), lambda b, hp, qi: (b, qi, hp)),
        out_shape=jax.ShapeDtypeStruct((bsz, s, BRANCH_W), BF16),
        scratch_shapes=scratch,
        compiler_params=_cparams(("arbitrary",) * 3),
        name=name,
    )(scalars, slopes2, q_t, k, vt, subln_pair).reshape(bsz * s, BRANCH_W)


def _silu(x):
    return x * jax.nn.sigmoid(x)


def _mix_kernel(x_ref, au_ref, av_ref, ag_ref, bg_ref, cg_ref, din_ref, db_ref, dc_ref, dg_ref,
                din_p_ref, dc_p_ref, din_n_ref, dc_n_ref,
                o1_ref, o4_ref, o16_ref, l1_ref, l4_ref, l16_ref, oc_ref,
                sgu_ref, ws_ref, bs_ref, cw_ref, wout_ref, y_ref, *stage_refs, tiles_per_seq, bounded):
    i = pl.program_id(0)
    t = x_ref.shape[0]
    first_head = lax.broadcasted_iota(jnp.int32, (CHUNK, LANES), 1) < HEAD_DIM

    if bounded:
        for stage, blk, dilation in zip(stage_refs, (o4_ref, o16_ref, l4_ref, l16_ref), (4, 16, 4, 16)):
            for r in range(dilation):
                for cb in range(HEAD_PAIRS):
                    stage[cb, pl.ds(r, t // dilation, stride=dilation), :] = (
                        blk[0, r, :, cb * LANES:(cb + 1) * LANES].astype(F32))

    z = dc_ref[...].astype(F32) * din_ref[...].astype(F32)
    seq_pos = i % tiles_per_seq
    z_prev = (dc_p_ref[...].astype(F32) * din_p_ref[...].astype(F32))[HALO - 1:HALO, :]
    z_next = (dc_n_ref[...].astype(F32) * din_n_ref[...].astype(F32))[0:1, :]
    z_prev = jnp.where(seq_pos == 0, 0.0, z_prev)
    z_next = jnp.where(seq_pos == tiles_per_seq - 1, 0.0, z_next)
    row = lax.broadcasted_iota(jnp.int32, z.shape, 0)
    z_up = jnp.where(row == 0, z_prev, pltpu.roll(z, 1, axis=0))
    z_dn = jnp.where(row == t - 1, z_next, pltpu.roll(z, t - 1, axis=0))
    cw = cw_ref[...]
    out_d = db_ref[...].astype(F32) * (cw[0:1] * z_up + cw[1:2] * z + cw[2:3] * z_dn)

    for sb in range(t // MIX_SUB):
        rows = slice(sb * MIX_SUB, (sb + 1) * MIX_SUB)

        av = av_ref[rows, :].astype(F32)
        vn = (av * lax.rsqrt(jnp.mean(av * av, axis=-1, keepdims=True) + EPS) * sgu_ref[...]).astype(BF16)
        chunks = []
        for c in range(MIX_SUB // CHUNK):
            vc = vn[c * CHUNK:(c + 1) * CHUNK, :]
            pairs = []
            for hp in range(HEAD_PAIRS):
                vp = vc[:, hp * LANES:(hp + 1) * LANES]
                r0 = jnp.dot(ws_ref[2 * hp], vp, preferred_element_type=F32)
                r1 = jnp.dot(ws_ref[2 * hp + 1], vp, preferred_element_type=F32)
                pairs.append(jnp.where(first_head, r0, r1))
            chunks.append(jnp.concatenate(pairs, axis=1) + bs_ref[...])
        out_a = au_ref[rows, :].astype(F32) * jnp.concatenate(chunks, axis=0)

        if bounded:
            def in_order(stage):
                return jnp.concatenate([stage[cb, rows, :] for cb in range(HEAD_PAIRS)], axis=1)

            num = o1_ref[rows, :].astype(F32) + in_order(stage_refs[0]) + in_order(stage_refs[1])
            den = l1_ref[rows, :] + in_order(stage_refs[2]) + in_order(stage_refs[3])
            out_b = num / den
        else:
            l1, l4, l16 = l1_ref[rows, :], l4_ref[rows, :], l16_ref[rows, :]
            lm = jnp.maximum(jnp.maximum(l1, l4), l16)
            w1, w4, w16 = jnp.exp2(l1 - lm), jnp.exp2(l4 - lm), jnp.exp2(l16 - lm)
            out_b = ((w1 * o1_ref[rows, :].astype(F32) + w4 * o4_ref[rows, :].astype(F32)
                      + w16 * o16_ref[rows, :].astype(F32)) / (w1 + w4 + w16))

        mixed = jnp.concatenate([
            (_silu(ag_ref[rows, :].astype(F32)) * out_a).astype(BF16),
            (_silu(bg_ref[rows, :].astype(F32)) * out_b).astype(BF16),
            (_silu(cg_ref[rows, :].astype(F32)) * oc_ref[rows, :].astype(F32)).astype(BF16),
            (_silu(dg_ref[rows, :].astype(F32)) * out_d[rows]).astype(BF16)], axis=1)
        y_ref[rows, :] = x_ref[rows, :] + jnp.dot(mixed, wout_ref[...], preferred_element_type=F32)


def _mix_out(bounded, x2, proj, dil, oc, sgu_g, ws_bf, bs_tab, conv_w, wout_bf, s):
    n = x2.shape[0]
    tiles_per_seq = s // MIX_T
    halo_per_tile = MIX_T // HALO
    n_halo = n // HALO

    def piece(p):
        return pl.BlockSpec((MIX_T, BRANCH_W), lambda i, p=p: (i, p))

    def halo_prev(p):
        return pl.BlockSpec((HALO, BRANCH_W), lambda i, p=p: (jnp.maximum(i * halo_per_tile - 1, 0), p))

    def halo_next(p):
        return pl.BlockSpec((HALO, BRANCH_W),
                            lambda i, p=p: (jnp.minimum((i + 1) * halo_per_tile, n_halo - 1), p))

    def tokens():
        return pl.BlockSpec((MIX_T, BRANCH_W), lambda i: (i, 0))

    def const(shape):
        return pl.BlockSpec(shape, lambda i: (0,) * len(shape))

    def residues(d):
        return pl.BlockSpec((1, d, MIX_T // d, BRANCH_W),
                            lambda i: (i // tiles_per_seq, 0, i % tiles_per_seq, 0))

    (o1, l1), (o4, l4), (o16, l16) = dil
    if bounded:
        dil_specs = [tokens(), residues(4), residues(16)] * 2
        scratch = [pltpu.VMEM((HEAD_PAIRS, MIX_T, LANES), F32)] * 4
    else:
        dil_specs = [tokens()] * 6
        scratch = []
    return pl.pallas_call(
        functools.partial(_mix_kernel, tiles_per_seq=tiles_per_seq, bounded=bounded),
        grid=(n // MIX_T,),
        in_specs=[pl.BlockSpec((MIX_T, D_MODEL), lambda i: (i, 0)),
                  piece(P_AU), piece(P_AV), piece(P_AG), piece(P_BG), piece(P_CG),
                  piece(P_DIN), piece(P_DB), piece(P_DC), piece(P_DG),
                  halo_prev(P_DIN), halo_prev(P_DC), halo_next(P_DIN), halo_next(P_DC),
                  *dil_specs, tokens(),
                  const((1, BRANCH_W)), const((GROUP_HEADS, CHUNK, CHUNK)), const((CHUNK, BRANCH_W)),
                  const((3, BRANCH_W)), const((MIX_W, D_MODEL))],
        out_specs=pl.BlockSpec((MIX_T, D_MODEL), lambda i: (i, 0)),
        out_shape=jax.ShapeDtypeStruct((n, D_MODEL), F32),
        scratch_shapes=scratch,
        compiler_params=_cparams(("arbitrary",)),
        name="mixout_bounded" if bounded else "mixout",
    )(x2, *([proj] * 13), o1, o4, o16, l1, l4, l16, oc,
      sgu_g.reshape(1, BRANCH_W), ws_bf, bs_tab, conv_w, wout_bf)


def _layer(x2, bsz, s, l, norm_g, w_in, sgu_g, w_s, b_s, qn_b, kn_b, qn_c, kn_c,
           lam_q1, lam_k1, lam_q2, lam_k2, subln_g, conv_w, w_out):
    slopes2 = (2.0 ** (-8.0 * jnp.arange(1, GROUP_HEADS + 1, dtype=F32) / GROUP_HEADS)) * LOG2E
    lambda_init = 0.8 - 0.6 * math.exp(-0.3 * l)
    lam = (jnp.exp(jnp.sum(lam_q1 * lam_k1)) - jnp.exp(jnp.sum(lam_q2 * lam_k2)) + lambda_init).reshape(1)

    proj = _inproj(x2, norm_g, w_in.astype(BF16))
    gqb = (jnp.tile(qn_b, GROUP_HEADS) * (LOG2E / math.sqrt(HEAD_DIM))).reshape(1, BRANCH_W)
    gkb = jnp.tile(kn_b, GROUP_HEADS).reshape(1, BRANCH_W)
    gqc = (jnp.tile(qn_c, 2 * GROUP_HEADS) * (LOG2E / math.sqrt(DIFF_QK_DIM))).reshape(1, BRANCH_W)
    gkc = jnp.tile(kn_c, 2 * GROUP_HEADS).reshape(1, BRANCH_W)
    bqn, bkn, kcn, qt, vt, q4, k4, v4, q16, k16, v16, k8, q8 = _prep(proj, bsz, s, gqb, gkb, gqc, gkc)

    bound_b = (jnp.max(jnp.abs(qn_b)) * jnp.max(jnp.abs(kn_b)) * (LOG2E * math.sqrt(HEAD_DIM))).reshape(1)
    bound_c = (jnp.max(jnp.abs(qn_c)) * jnp.max(jnp.abs(kn_c)) * (LOG2E * math.sqrt(DIFF_QK_DIM))).reshape(1)
    subln_pair = jnp.tile(subln_g, 2).reshape(1, LANES)
    bs_tab = jnp.repeat(b_s.T, HEAD_DIM, axis=1)
    ws_bf, wout_bf = w_s.astype(BF16), w_out.astype(BF16)

    def fixed_shift_path():
        operands = {1: (bqn.reshape(bsz, 1, s, BRANCH_W), bkn.reshape(bsz, 1, s, BRANCH_W),
                        proj.reshape(bsz, 1, s, PROJ_W), P_BV * HEAD_PAIRS),
                    4: (q4, k4, v4, 0), 16: (q16, k16, v16, 0)}
        dil = []
        for _, d in DIL_PATTERNS:
            qd, kd, vd, v_off = operands[d]
            num, den = _dilated_bounded(_dilated_tables(slopes2, bound_b[0], d), qd, kd, vd, v_off, bsz, s, d)
            if d == 1:
                num, den = num.reshape(bsz * s, BRANCH_W), den.reshape(bsz * s, BRANCH_W)
            dil.append((num, den))
        oc = _diff(True, lam, bound_c, slopes2, q8, k8, vt, subln_pair, bsz, s, lambda_init)
        return _mix_out(True, x2, proj, dil, oc, sgu_g, ws_bf, bs_tab, conv_w, wout_bf, s)

    def running_max_path():
        dil = [_dilated(slopes2, bqn, bkn, proj, bsz, s, d) for (_, d) in DIL_PATTERNS]
        oc = _diff(False, lam, bound_c, slopes2, qt, kcn, vt, subln_pair, bsz, s, lambda_init)
        return _mix_out(False, x2, proj, dil, oc, sgu_g, ws_bf, bs_tab, conv_w, wout_bf, s)

    use_fixed_shift = jnp.maximum(bound_b[0], bound_c[0]) <= MAX_FIXED_SHIFT
    return lax.cond(use_fixed_shift, fixed_shift_path, running_max_path)


def _trunk(x, params):
    bsz, s, _ = x.shape
    x2 = x.reshape(bsz * s, D_MODEL)
    depth = params[0].shape[0]
    for l in range(depth):
        x2 = _layer(x2, bsz, s, l, *[p[l] for p in params])
    return x2.reshape(bsz, s, D_MODEL)


def kernel(x_prompt, x_sample, norm_g, w_in, sgu_g, w_s, b_s, qn_b, kn_b, qn_c, kn_c,
           lam_q1, lam_k1, lam_q2, lam_k2, subln_g, conv_w, w_out):
    params = (norm_g, w_in, sgu_g, w_s, b_s, qn_b, kn_b, qn_c, kn_c,
              lam_q1, lam_k1, lam_q2, lam_k2, subln_g, conv_w, w_out)
    return (_trunk(x_prompt, params), _trunk(x_sample, params))
```

```python
import functools
import math

import jax
import jax.numpy as jnp
from jax import lax
from jax.experimental import pallas as pl
from jax.experimental.pallas import tpu as pltpu

F32 = jnp.float32
BF16 = jnp.bfloat16
F8 = jnp.float8_e4m3fn

D_MODEL = 1024
HEAD_DIM = 64
GROUP_HEADS = 6
BRANCH_W = GROUP_HEADS * HEAD_DIM
N_PIECES = 15
PROJ_W = N_PIECES * BRANCH_W
MIX_W = 4 * BRANCH_W
CHUNK = 128
DIL_PATTERNS = ((128, 1), (512, 4), (2048, 16))
N_SIDE = 64
DIFF_QK_DIM = HEAD_DIM // 2
EPS = 1e-6
NEG = -1e30
LOG2E = math.log2(math.e)

LANES = 128
HEAD_PAIRS = BRANCH_W // LANES
V_ROWS = 80

P_AU, P_AV, P_AG, P_BQ, P_BK, P_BV, P_BG, P_CQ, P_CK, P_CV, P_CG, P_DIN, P_DB, P_DC, P_DG = range(15)

IN_TM = 1024
IN_TN = 1920
PREP_T = 1024
PREP_SUB = 128
MIX_T = 512
MIX_SUB = 512
HALO = 16
DIL_TQ = 128
DIL_WIN = DIL_TQ + 2 * N_SIDE
DIL_ROWS = 4096
DIL_GROUP = 16
DIFF_TQ = 256
DIFF_TQ_FIXED = 512
DIFF_SUBTILES = 2
DIFF_TRIP = 4
DIFF_TK = 256
VMEM_LIMIT = 56 * 1024 * 1024
MAX_FIXED_SHIFT = 40.0


def _cparams(sem):
    return pltpu.CompilerParams(dimension_semantics=sem, vmem_limit_bytes=VMEM_LIMIT)


def _inproj_kernel(x_ref, g_ref, w_ref, o_ref):
    x = x_ref[...]
    ms = jnp.mean(x * x, axis=-1, keepdims=True)
    h = (x * lax.rsqrt(ms + EPS) * g_ref[...]).astype(BF16)
    o_ref[...] = jnp.dot(h, w_ref[...], preferred_element_type=F32).astype(BF16)


def _inproj(x2, norm_g, w_in_bf):
    n = x2.shape[0]
    return pl.pallas_call(
        _inproj_kernel,
        grid=(PROJ_W // IN_TN, n // IN_TM),
        in_specs=[
            pl.BlockSpec((IN_TM, D_MODEL), lambda j, i: (i, 0)),
            pl.BlockSpec((1, D_MODEL), lambda j, i: (0, 0)),
            pl.BlockSpec((D_MODEL, IN_TN), lambda j, i: (0, j)),
        ],
        out_specs=pl.BlockSpec((IN_TM, IN_TN), lambda j, i: (i, j)),
        out_shape=jax.ShapeDtypeStruct((n, PROJ_W), BF16),
        compiler_params=_cparams(("arbitrary", "arbitrary")),
        name="inproj",
    )(x2, norm_g.reshape(1, D_MODEL), w_in_bf)


def _group_rms(x, e_ref, width):
    x2 = x * x
    hi = x2.astype(BF16)
    lo = (x2 - hi.astype(F32)).astype(BF16)
    e = e_ref[...]
    ss = jnp.dot(hi, e, preferred_element_type=F32) + jnp.dot(lo, e, preferred_element_type=F32)
    return x * lax.rsqrt(ss * (1.0 / width) + EPS)


def _prep_kernel(bq_ref, bk_ref, bv_ref, cq_ref, ck_ref, cv_ref, gqb_ref, gkb_ref, gqc_ref, gkc_ref,
                 e64_ref, e32_ref, bqn_ref, bkn_ref, kcn_ref, qt_ref, vt_ref,
                 q4_ref, k4_ref, v4_ref, q16_ref, k16_ref, v16_ref, k8_ref, q8_ref, sq_ref, sk_ref, sv_ref):
    t = bq_ref.shape[0]
    low_lanes = lax.broadcasted_iota(jnp.int32, (PREP_SUB, LANES), 1) < HEAD_DIM
    row = lax.broadcasted_iota(jnp.int32, (V_ROWS - HEAD_DIM, PREP_SUB), 0)
    ones_pad = jnp.where(row == 0, 1.0, 0.0).astype(BF16)
    for cb in range(HEAD_PAIRS):
        sl = slice(cb * LANES, (cb + 1) * LANES)
        for r0 in range(0, t, PREP_SUB):
            rs = slice(r0, r0 + PREP_SUB)
            qb = _group_rms(bq_ref[rs, sl].astype(F32), e64_ref, HEAD_DIM) * gqb_ref[:, sl]
            kb = _group_rms(bk_ref[rs, sl].astype(F32), e64_ref, HEAD_DIM) * gkb_ref[:, sl]
            sq_ref[cb, rs, :] = qb
            sk_ref[cb, rs, :] = kb
            sv_ref[cb, rs, :] = bv_ref[rs, sl].astype(F32)
            bqn_ref[rs, sl] = qb.astype(BF16)
            bkn_ref[rs, sl] = kb.astype(BF16)
            kn = _group_rms(ck_ref[rs, sl].astype(F32), e32_ref, DIFF_QK_DIM) * gkc_ref[:, sl]
            kcn_ref[rs, sl] = kn.astype(BF16)
            k_hi = kn.astype(F8).astype(F32)
            k_lo_swapped = pltpu.roll((kn - k_hi).astype(F8).astype(F32), HEAD_DIM, axis=1)
            zero = jnp.zeros_like(kn)
            k8_ref[0, cb, 0, rs, 0:LANES] = jnp.where(low_lanes, k_hi, k_lo_swapped).astype(F8)
            k8_ref[0, cb, 0, rs, LANES:2 * LANES] = jnp.where(low_lanes, k_hi, zero).astype(F8)
            k8_ref[0, cb, 1, rs, 0:LANES] = jnp.where(low_lanes, k_lo_swapped, k_hi).astype(F8)
            k8_ref[0, cb, 1, rs, LANES:2 * LANES] = jnp.where(low_lanes, zero, k_hi).astype(F8)
            qn_t = (_group_rms(cq_ref[rs, sl].astype(F32), e32_ref, DIFF_QK_DIM) * gqc_ref[:, sl]).T
            qt_ref[0, cb, :, rs] = qn_t.astype(BF16)
            q_hi = qn_t.astype(F8)
            q8_ref[0, cb, 0, :, rs] = q_hi
            q8_ref[0, cb, 1, :, rs] = (qn_t - q_hi.astype(F32)).astype(F8)
            vt = cv_ref[rs, sl].astype(F32).T.astype(BF16)
            for hh in range(2):
                vt_ref[0, cb, hh, 0:HEAD_DIM, rs] = vt[hh * HEAD_DIM:(hh + 1) * HEAD_DIM, :]
                vt_ref[0, cb, hh, HEAD_DIM:V_ROWS, rs] = ones_pad
    for stage, out4, out16 in ((sq_ref, q4_ref, q16_ref), (sk_ref, k4_ref, k16_ref), (sv_ref, v4_ref, v16_ref)):
        for dilation, out in ((4, out4), (16, out16)):
            for r in range(dilation):
                for cb in range(HEAD_PAIRS):
                    out[0, r, :, cb * LANES:(cb + 1) * LANES] = (
                        stage[cb, pl.ds(r, t // dilation, stride=dilation), :].astype(BF16))


def _block_diag_ones(group):
    idx = jnp.arange(LANES) // group
    return (idx[:, None] == idx[None, :]).astype(BF16)


def _prep(proj, bsz, s, gqb, gkb, gqc, gkc):
    n = proj.shape[0]
    tiles_per_seq = s // PREP_T

    def piece(p):
        return pl.BlockSpec((PREP_T, BRANCH_W), lambda i, p=p: (i, p))

    def vec():
        return pl.BlockSpec((1, BRANCH_W), lambda i: (0, 0))

    def tokens():
        return pl.BlockSpec((PREP_T, BRANCH_W), lambda i: (i, 0))

    def residues(d):
        return pl.BlockSpec((1, d, PREP_T // d, BRANCH_W),
                            lambda i: (i // tiles_per_seq, 0, i % tiles_per_seq, 0))

    return pl.pallas_call(
        _prep_kernel,
        grid=(n // PREP_T,),
        in_specs=[piece(P_BQ), piece(P_BK), piece(P_BV), piece(P_CQ), piece(P_CK), piece(P_CV),
                  vec(), vec(), vec(), vec(),
                  pl.BlockSpec((LANES, LANES), lambda i: (0, 0)),
                  pl.BlockSpec((LANES, LANES), lambda i: (0, 0))],
        out_specs=[tokens(), tokens(), tokens(),
                   pl.BlockSpec((1, HEAD_PAIRS, LANES, PREP_T),
                                lambda i: (i // tiles_per_seq, 0, 0, i % tiles_per_seq)),
                   pl.BlockSpec((1, HEAD_PAIRS, 2, V_ROWS, PREP_T),
                                lambda i: (i // tiles_per_seq, 0, 0, 0, i % tiles_per_seq))]
                  + [residues(d) for d in (4, 4, 4, 16, 16, 16)]
                  + [pl.BlockSpec((1, HEAD_PAIRS, 2, PREP_T, 2 * LANES),
                                  lambda i: (i // tiles_per_seq, 0, 0, i % tiles_per_seq, 0)),
                     pl.BlockSpec((1, HEAD_PAIRS, 2, LANES, PREP_T),
                                  lambda i: (i // tiles_per_seq, 0, 0, 0, i % tiles_per_seq))],
        out_shape=[jax.ShapeDtypeStruct((n, BRANCH_W), BF16),
                   jax.ShapeDtypeStruct((n, BRANCH_W), BF16),
                   jax.ShapeDtypeStruct((n, BRANCH_W), BF16),
                   jax.ShapeDtypeStruct((bsz, HEAD_PAIRS, LANES, s), BF16),
                   jax.ShapeDtypeStruct((bsz, HEAD_PAIRS, 2, V_ROWS, s), BF16)]
                  + [jax.ShapeDtypeStruct((bsz, d, s // d, BRANCH_W), BF16) for d in (4, 4, 4, 16, 16, 16)]
                  + [jax.ShapeDtypeStruct((bsz, HEAD_PAIRS, 2, s, 2 * LANES), F8),
                     jax.ShapeDtypeStruct((bsz, HEAD_PAIRS, 2, LANES, s), F8)],
        scratch_shapes=[pltpu.VMEM((HEAD_PAIRS, PREP_T, LANES), F32)] * 3,
        compiler_params=_cparams(("arbitrary",)),
        name="prep",
    )(proj, proj, proj, proj, proj, proj, gqb, gkb, gqc, gkc,
      _block_diag_ones(HEAD_DIM), _block_diag_ones(DIFF_QK_DIM))


def _dilated_kernel(slope_ref, q_ref, k_ref, v_ref, o_ref, lse_ref, *, dilation, seq_sub):
    hp = pl.program_id(2)
    qi = pl.program_id(3)
    tqb = q_ref.shape[1]
    lane = lax.broadcasted_iota(jnp.int32, (DIL_TQ, LANES), 1)
    first_head = lane < HEAD_DIM
    rel0 = (lax.broadcasted_iota(jnp.int32, (DIL_TQ, DIL_WIN), 1)
            - lax.broadcasted_iota(jnp.int32, (DIL_TQ, DIL_WIN), 0))

    def sub_block(sb, carry):
        r0 = pl.multiple_of(sb * DIL_TQ, DIL_TQ)
        i0 = qi * tqb + r0
        ws = pl.multiple_of(jnp.clip(i0 - N_SIDE, 0, seq_sub - DIL_WIN), N_SIDE)
        kw = k_ref[0, pl.ds(ws, DIL_WIN), :]
        vw = v_ref[0, pl.ds(ws, DIL_WIN), :]
        q = q_ref[0, pl.ds(r0, DIL_TQ), :]
        dist = jnp.abs(rel0 + (ws - i0))
        valid = dist <= N_SIDE
        distf = dist.astype(F32)
        outs, lses = [], []
        for hh in range(2):
            slope = slope_ref[2 * hp + hh] * float(dilation)
            qm = jnp.where(first_head if hh == 0 else jnp.logical_not(first_head), q, jnp.zeros_like(q))
            sc = lax.dot_general(qm, kw, (((1,), (1,)), ((), ())), preferred_element_type=F32)
            sc = jnp.where(valid, sc - slope * distf, NEG)
            m = jnp.max(sc, axis=-1, keepdims=True)
            p = jnp.exp2(sc - m)
            den = jnp.sum(p, axis=-1, keepdims=True)
            o = jnp.dot(p.astype(BF16), vw, preferred_element_type=F32) / den
            outs.append(o)
            lses.append(jnp.broadcast_to(m + jnp.log2(den), (DIL_TQ, LANES)))
        o_ref[0, pl.ds(r0, DIL_TQ), :] = jnp.where(first_head, outs[0], outs[1]).astype(o_ref.dtype)
        lse_ref[0, pl.ds(r0, DIL_TQ), :] = jnp.where(first_head, lses[0], lses[1])
        return carry

    lax.fori_loop(0, tqb // DIL_TQ, sub_block, 0)


def _dilated(slopes2, bqn, bkn, proj, bsz, s, dilation):
    seq_sub = s // dilation
    tqb = min(seq_sub, 1024)
    q3 = bqn.reshape(bsz, seq_sub, dilation * BRANCH_W)
    k3 = bkn.reshape(bsz, seq_sub, dilation * BRANCH_W)
    v3 = proj.reshape(bsz, seq_sub, dilation * PROJ_W)
    blocks_q = BRANCH_W // LANES
    blocks_p = PROJ_W // LANES
    v_off = P_BV * blocks_q

    def tile():
        return pl.BlockSpec((1, tqb, LANES), lambda b, r, hp, qi: (b, qi, r * blocks_q + hp))

    o, lse = pl.pallas_call(
        functools.partial(_dilated_kernel, dilation=dilation, seq_sub=seq_sub),
        grid=(bsz, dilation, HEAD_PAIRS, seq_sub // tqb),
        in_specs=[
            pl.BlockSpec(memory_space=pltpu.SMEM),
            tile(),
            pl.BlockSpec((1, seq_sub, LANES), lambda b, r, hp, qi: (b, 0, r * blocks_q + hp)),
            pl.BlockSpec((1, seq_sub, LANES), lambda b, r, hp, qi: (b, 0, r * blocks_p + v_off + hp)),
        ],
        out_specs=[tile(), tile()],
        out_shape=[jax.ShapeDtypeStruct(q3.shape, BF16), jax.ShapeDtypeStruct(q3.shape, F32)],
        compiler_params=_cparams(("arbitrary",) * 4),
        name=f"dilated{dilation}",
    )(slopes2, q3, k3, v3)
    return o.reshape(bsz * s, BRANCH_W), lse.reshape(bsz * s, BRANCH_W)


def _dilated_bounded_kernel(tab_ref, q_ref, k_ref, v_ref, num_ref, den_ref, p_ref, *, seq_sub):
    qi = pl.program_id(3)
    n_res, tqb = q_ref.shape[1], q_ref.shape[2]
    sub_per_res = tqb // DIL_TQ
    first_head_q = lax.broadcasted_iota(jnp.int32, (DIL_TQ, LANES), 1) < HEAD_DIM
    v_lane = lax.broadcasted_iota(jnp.int32, (DIL_WIN, 2 * LANES), 1)
    value_lanes = (v_lane < HEAD_DIM) | (v_lane >= 2 * LANES - HEAD_DIM)

    def window(sb):
        res = sb // sub_per_res
        r0 = pl.multiple_of((sb % sub_per_res) * DIL_TQ, DIL_TQ)
        i0 = qi * tqb + r0
        ws = pl.multiple_of(jnp.clip(i0 - N_SIDE, 0, seq_sub - DIL_WIN), N_SIDE)
        variant = jnp.where(i0 == 0, 0, jnp.where(i0 == seq_sub - DIL_TQ, 2, 1))
        return res, r0, ws, variant

    def probabilities(u):
        for half in range(DIL_GROUP):
            res, r0, ws, variant = window(DIL_GROUP * u + half)
            kw = k_ref[0, res, pl.ds(ws, DIL_WIN), :]
            q = q_ref[0, res, pl.ds(r0, DIL_TQ), :]
            zero = jnp.zeros_like(q)
            q2 = jnp.concatenate([jnp.where(first_head_q, q, zero), jnp.where(first_head_q, zero, q)], axis=0)
            sc = lax.dot_general(q2, kw, (((1,), (1,)), ((), ())), preferred_element_type=F32)
            for hh in range(2):
                p_ref[half, hh * DIL_TQ:(hh + 1) * DIL_TQ, :] = jnp.exp2(
                    sc[hh * DIL_TQ:(hh + 1) * DIL_TQ] + tab_ref[hh, variant]).astype(BF16)

    def outputs(u):
        for half in range(DIL_GROUP):
            res, r0, ws, _ = window(DIL_GROUP * u + half)
            vw = v_ref[0, res, pl.ds(ws, DIL_WIN), :]
            vw2 = jnp.concatenate([vw, vw], axis=1)
            vcat = jnp.where(value_lanes, vw2, jnp.ones_like(vw2))
            a = jnp.dot(p_ref[half], vcat, preferred_element_type=F32)
            a0 = a[0:DIL_TQ, 0:LANES]
            a1 = a[DIL_TQ:2 * DIL_TQ, LANES:2 * LANES]
            num_ref[0, res, pl.ds(r0, DIL_TQ), :] = jnp.where(first_head_q, a0, a1).astype(num_ref.dtype)
            den_ref[0, res, pl.ds(r0, DIL_TQ), :] = pltpu.roll(jnp.where(first_head_q, a1, a0), HEAD_DIM, axis=1)

    n_groups = n_res * sub_per_res // DIL_GROUP
    probabilities(0)

    def step(u, carry):
        outputs(u - 1)
        probabilities(u)
        return carry

    lax.fori_loop(1, n_groups, step, 0)
    outputs(n_groups - 1)


def _dilated_tables(slopes2, bound, dilation):
    rel = ((jnp.arange(DIL_WIN)[None, :] - jnp.arange(DIL_TQ)[:, None])[None]
           - jnp.array([0, N_SIDE, 2 * N_SIDE])[:, None, None])
    dist = jnp.abs(rel).astype(F32)
    bias = -(slopes2 * float(dilation))[:, None, None, None] * dist[None] - bound
    return jnp.where(dist[None] <= N_SIDE, bias, NEG).astype(F32)


def _dilated_bounded(tables, q4d, k4d, v4d, v_block_offset, bsz, s, dilation):
    seq_sub = s // dilation
    tqb = min(seq_sub, DIL_ROWS)
    n_res = min(dilation, DIL_ROWS // tqb)

    def tile():
        return pl.BlockSpec((1, n_res, tqb, LANES), lambda b, r, hp, qi: (b, r, qi, hp))

    out_shape = (bsz, dilation, seq_sub, BRANCH_W)
    return pl.pallas_call(
        functools.partial(_dilated_bounded_kernel, seq_sub=seq_sub),
        grid=(bsz, dilation // n_res, HEAD_PAIRS, seq_sub // tqb),
        in_specs=[
            pl.BlockSpec((2, 3, DIL_TQ, DIL_WIN), lambda b, r, hp, qi: (hp, 0, 0, 0)),
            tile(),
            pl.BlockSpec((1, n_res, seq_sub, LANES), lambda b, r, hp, qi: (b, r, 0, hp)),
            pl.BlockSpec((1, n_res, seq_sub, LANES), lambda b, r, hp, qi: (b, r, 0, v_block_offset + hp)),
        ],
        out_specs=[tile(), tile()],
        out_shape=[jax.ShapeDtypeStruct(out_shape, BF16), jax.ShapeDtypeStruct(out_shape, F32)],
        scratch_shapes=[pltpu.VMEM((DIL_GROUP, 2 * DIL_TQ, DIL_WIN), BF16)],
        compiler_params=_cparams(("arbitrary",) * 4),
        name=f"dilated_bounded{dilation}",
    )(tables, q4d, k4d, v4d)


def _diff_kernel(lam_ref, slope_ref, qt_ref, k_ref, vt_ref, g_ref, o_ref, qm_ref, tm_ref, acc_ref,
                 *, out_scale):
    hp = pl.program_id(1)
    qi = pl.program_id(2)
    tq = qt_ref.shape[3]
    tk = DIFF_TK
    nk = k_ref.shape[1] // tk
    i0 = qi * tq

    qt = qt_ref[0, 0]
    row_group = lax.broadcasted_iota(jnp.int32, (LANES, tq), 0) // DIFF_QK_DIM
    for g in range(4):
        qm_ref[g] = jnp.where(row_group == g, qt, jnp.zeros_like(qt))
    dkq = (lax.broadcasted_iota(jnp.int32, (tk, tq), 0)
           - lax.broadcasted_iota(jnp.int32, (tk, tq), 1)).astype(F32)
    for hh in range(2):
        tm_ref[hh] = slope_ref[2 * hp + hh] * dkq
    acc_ref[...] = jnp.zeros_like(acc_ref)

    def key_tile(j, ms, mode):
        koff = pl.multiple_of(j * tk, tk)
        kt = k_ref[0, pl.ds(koff, tk), :]
        tile_dist = jnp.abs(i0 - koff).astype(F32)
        new_ms = []
        for hh in range(2):
            vt = vt_ref[0, 0, hh, :, pl.ds(koff, tk)]
            shift = slope_ref[2 * hp + hh] * tile_dist
            for c in range(2):
                g = 2 * hh + c
                sc = jnp.dot(kt, qm_ref[g], preferred_element_type=F32)
                if mode < 0:
                    sc = sc + tm_ref[hh]
                elif mode > 0:
                    sc = sc - tm_ref[hh]
                else:
                    sc = sc - jnp.abs(tm_ref[hh])
                m_old = ms[g]
                m_new = jnp.maximum(m_old, jnp.max(sc, axis=0, keepdims=True) - shift)
                p = jnp.exp2(sc - (m_new + shift))
                alpha = jnp.exp2(m_old - m_new)
                acc_ref[g] = alpha * acc_ref[g] + jnp.dot(vt, p.astype(BF16), preferred_element_type=F32)
                new_ms.append(m_new)
        return tuple(new_ms)

    ms = tuple(jnp.full((1, tq), NEG, F32) for _ in range(4))
    ms = lax.fori_loop(0, qi, lambda j, m: key_tile(j, m, -1), ms)
    ms = key_tile(qi, ms, 0)
    lax.fori_loop(qi + 1, nk, lambda j, m: key_tile(j, m, 1), ms)
    _diff_finish(lam_ref, g_ref, o_ref, acc_ref, out_scale)


def _diff_bounded_kernel(lam_ref, slope_ref, qt_ref, k_ref, vt_ref, g_ref, o_ref, qm_ref, tm_ref, acc_ref,
                         p_ref, *, out_scale):
    hp = pl.program_id(1)
    qi = pl.program_id(2)
    n_sub = DIFF_SUBTILES
    tq = qt_ref.shape[4] // n_sub
    tk = DIFF_TK
    tiles_per_q = tq // tk
    nk = k_ref.shape[3] // tk
    centre = 0.5 * lam_ref[1]

    zeros = jnp.zeros((DIFF_QK_DIM, tq), qt_ref.dtype)
    for t in range(n_sub):
        q_hi = qt_ref[0, 0, 0, :, t * tq:(t + 1) * tq]
        q_lo = qt_ref[0, 0, 1, :, t * tq:(t + 1) * tq]
        for g in range(4):
            hh, c = divmod(g, 2)
            hi_block, lo_block, hi2_block = ((0, 2, 4), (2, 0, 6))[hh]
            rows = slice(g * DIFF_QK_DIM, (g + 1) * DIFF_QK_DIM)
            blocks = [zeros] * (2 * LANES // DIFF_QK_DIM)
            blocks[hi_block + c], blocks[lo_block + c], blocks[hi2_block + c] = q_hi[rows], q_hi[rows], q_lo[rows]
            qm_ref[t, g] = jnp.concatenate(blocks, axis=0)
    @pl.when(qi == 0)
    def _():
        dkq = (lax.broadcasted_iota(jnp.int32, (tk, tq), 0)
               - lax.broadcasted_iota(jnp.int32, (tk, tq), 1)).astype(F32)
        for hh in range(2):
            slope = slope_ref[2 * hp + hh]
            tm = slope * dkq
            tm_ref[hh, 0] = tm
            tm_ref[hh, 1] = -tm
            for r in range(tiles_per_q):
                origin = float(r * tk)
                tm_ref[hh, 2 + r] = slope * (origin - jnp.abs(dkq + origin))

    acc_ref[...] = jnp.zeros_like(acc_ref)

    def stage(scores=None, values=None):
        for hh in range(2):
            slope = slope_ref[2 * hp + hh]
            for c in range(2):
                g = 2 * hh + c
                parts = []
                for half in range(2):
                    if values is not None:
                        t, u, slot = values
                        koff = pl.multiple_of((2 * u + half) * tk, tk)
                        parts.append(jnp.dot(vt_ref[0, 0, hh, :, pl.ds(koff, tk)], p_ref[slot, half, g],
                                             preferred_element_type=F32))
                    if scores is not None:
                        t, u, slot = scores
                        j = 2 * u + half
                        koff = pl.multiple_of(j * tk, tk)
                        q_tile = qi * n_sub + t
                        shift = -(slope * jnp.abs(q_tile * tq - koff).astype(F32) + centre)
                        first = tiles_per_q * q_tile
                        table = jnp.where(j < first, 0, jnp.where(j >= first + tiles_per_q, 1, 2 + j - first))
                        sc = jnp.dot(k_ref[0, 0, hh, pl.ds(koff, tk), :], qm_ref[t, g],
                                     preferred_element_type=F32)
                        p_ref[slot, half, g] = jnp.exp2((sc + tm_ref[hh, table] + shift).astype(BF16))
                if parts:
                    acc_ref[values[0], g] += parts[0] + parts[1]

    n_pairs = nk // 2
    stage(scores=(0, 0, 0))
    for t in range(n_sub):
        def steps(first, count, t=t):
            for k in range(count):
                stage(scores=(t, first + k, (1 + k) % 2), values=(t, first + k - 1, k % 2))

        trip_steps = DIFF_TRIP if (n_pairs - 2) // DIFF_TRIP >= 2 else 2

        def trip(w, carry):
            steps(trip_steps * w + 1, trip_steps)
            return carry

        n_trips = (n_pairs - 2) // trip_steps
        lax.fori_loop(0, n_trips, trip, 0)
        steps(trip_steps * n_trips + 1, n_pairs - 2 - trip_steps * n_trips)
        stage(scores=(t, n_pairs - 1, 1), values=(t, n_pairs - 2, 0))
        stage(scores=(t + 1, 0, 0) if t + 1 < n_sub else None, values=(t, n_pairs - 1, 1))
        _diff_finish(lam_ref, g_ref, o_ref, acc_ref.at[t], out_scale, slice(t * tq, (t + 1) * tq))


def _diff_finish(lam_ref, g_ref, o_ref, acc_ref, out_scale, out_rows=slice(None)):
    lam = lam_ref[0]
    heads = []
    for hh in range(2):
        a0 = acc_ref[2 * hh]
        a1 = acc_ref[2 * hh + 1]
        o = (a0[0:HEAD_DIM] / a0[HEAD_DIM:HEAD_DIM + 1]
             - lam * (a1[0:HEAD_DIM] / a1[HEAD_DIM:HEAD_DIM + 1]))
        ms2 = jnp.mean(o * o, axis=0, keepdims=True)
        heads.append(o * lax.rsqrt(ms2 + EPS))
    out = jnp.concatenate(heads, axis=0).T
    o_ref[0, out_rows, :] = (out * (g_ref[...] * out_scale)).astype(o_ref.dtype)


def _diff(bounded, lam, score_bound, slopes2, q_t, k, vt, subln_pair, bsz, s, lambda_init):
    scalars = jnp.concatenate([lam, score_bound]).astype(F32)
    if bounded:
        body, name = _diff_bounded_kernel, "diff_bounded"
        sub = DIFF_TQ_FIXED
        tq = DIFF_SUBTILES * sub
        q_spec = pl.BlockSpec((1, 1, 2, LANES, tq), lambda b, hp, qi: (b, hp, 0, 0, qi))
        k_spec = pl.BlockSpec((1, 1, 2, s, 2 * LANES), lambda b, hp, qi: (b, hp, 0, 0, 0))
        scratch = [pltpu.VMEM((DIFF_SUBTILES, 4, 2 * LANES, sub), F8),
                   pltpu.VMEM((2, 2 + sub // DIFF_TK, DIFF_TK, sub), F32),
                   pltpu.VMEM((DIFF_SUBTILES, 4, V_ROWS, sub), F32),
                   pltpu.VMEM((2, 2, 4, DIFF_TK, sub), BF16)]
    else:
        body, name, tq = _diff_kernel, "diff_online", DIFF_TQ
        k = k.reshape(bsz, s, BRANCH_W)
        q_spec = pl.BlockSpec((1, 1, LANES, tq), lambda b, hp, qi: (b, hp, 0, qi))
        k_spec = pl.BlockSpec((1, s, LANES), lambda b, hp, qi: (b, 0, hp))
        scratch = [pltpu.VMEM((4, LANES, tq), BF16), pltpu.VMEM((2, DIFF_TK, tq), F32),
                   pltpu.VMEM((4, V_ROWS, tq), F32)]
    return pl.pallas_call(
        functools.partial(body, out_scale=1.0 - lambda_init),
        grid=(bsz, HEAD_PAIRS, s // tq),
        in_specs=[
            pl.BlockSpec(memory_space=pltpu.SMEM),
            pl.BlockSpec(memory_space=pltpu.SMEM),
            q_spec,
            k_spec,
            pl.BlockSpec((1, 1, 2, V_ROWS, s), lambda b, hp, qi: (b, hp, 0, 0, 0)),
            pl.BlockSpec((1, LANES), lambda b, hp, qi: (0, 0)),
        ],
        out_specs=pl.BlockSpec((1, tq, LANES), lambda b, hp, qi: (b, qi, hp)),
        out_shape=jax.ShapeDtypeStruct((bsz, s, BRANCH_W), BF16),
        scratch_shapes=scratch,
        compiler_params=_cparams(("arbitrary",) * 3),
        name=name,
    )(scalars, slopes2, q_t, k, vt, subln_pair).reshape(bsz * s, BRANCH_W)


def _silu(x):
    return x * jax.nn.sigmoid(x)


def _mix_kernel(x_ref, au_ref, av_ref, ag_ref, bg_ref, cg_ref, din_ref, db_ref, dc_ref, dg_ref,
                din_p_ref, dc_p_ref, din_n_ref, dc_n_ref,
                o1_ref, o4_ref, o16_ref, l1_ref, l4_ref, l16_ref, oc_ref,
                sgu_ref, ws_ref, bs_ref, cw_ref, wout_ref, y_ref, *stage_refs, tiles_per_seq, bounded):
    i = pl.program_id(0)
    t = x_ref.shape[0]
    first_head = lax.broadcasted_iota(jnp.int32, (CHUNK, LANES), 1) < HEAD_DIM

    if bounded:
        for stage, blk, dilation in zip(stage_refs, (o4_ref, o16_ref, l4_ref, l16_ref), (4, 16, 4, 16)):
            for r in range(dilation):
                for cb in range(HEAD_PAIRS):
                    stage[cb, pl.ds(r, t // dilation, stride=dilation), :] = (
                        blk[0, r, :, cb * LANES:(cb + 1) * LANES].astype(F32))

    z = dc_ref[...].astype(F32) * din_ref[...].astype(F32)
    seq_pos = i % tiles_per_seq
    z_prev = (dc_p_ref[...].astype(F32) * din_p_ref[...].astype(F32))[HALO - 1:HALO, :]
    z_next = (dc_n_ref[...].astype(F32) * din_n_ref[...].astype(F32))[0:1, :]
    z_prev = jnp.where(seq_pos == 0, 0.0, z_prev)
    z_next = jnp.where(seq_pos == tiles_per_seq - 1, 0.0, z_next)
    row = lax.broadcasted_iota(jnp.int32, z.shape, 0)
    z_up = jnp.where(row == 0, z_prev, pltpu.roll(z, 1, axis=0))
    z_dn = jnp.where(row == t - 1, z_next, pltpu.roll(z, t - 1, axis=0))
    cw = cw_ref[...]
    out_d = db_ref[...].astype(F32) * (cw[0:1] * z_up + cw[1:2] * z + cw[2:3] * z_dn)

    for sb in range(t // MIX_SUB):
        rows = slice(sb * MIX_SUB, (sb + 1) * MIX_SUB)

        av = av_ref[rows, :].astype(F32)
        vn = (av * lax.rsqrt(jnp.mean(av * av, axis=-1, keepdims=True) + EPS) * sgu_ref[...]).astype(BF16)
        chunks = []
        for c in range(MIX_SUB // CHUNK):
            vc = vn[c * CHUNK:(c + 1) * CHUNK, :]
            pairs = []
            for hp in range(HEAD_PAIRS):
                vp = vc[:, hp * LANES:(hp + 1) * LANES]
                r0 = jnp.dot(ws_ref[2 * hp], vp, preferred_element_type=F32)
                r1 = jnp.dot(ws_ref[2 * hp + 1], vp, preferred_element_type=F32)
                pairs.append(jnp.where(first_head, r0, r1))
            chunks.append(jnp.concatenate(pairs, axis=1) + bs_ref[...])
        out_a = au_ref[rows, :].astype(F32) * jnp.concatenate(chunks, axis=0)

        if bounded:
            def in_order(stage):
                return jnp.concatenate([stage[cb, rows, :] for cb in range(HEAD_PAIRS)], axis=1)

            num = o1_ref[rows, :].astype(F32) + in_order(stage_refs[0]) + in_order(stage_refs[1])
            den = l1_ref[rows, :] + in_order(stage_refs[2]) + in_order(stage_refs[3])
            out_b = num / den
        else:
            l1, l4, l16 = l1_ref[rows, :], l4_ref[rows, :], l16_ref[rows, :]
            lm = jnp.maximum(jnp.maximum(l1, l4), l16)
            w1, w4, w16 = jnp.exp2(l1 - lm), jnp.exp2(l4 - lm), jnp.exp2(l16 - lm)
            out_b = ((w1 * o1_ref[rows, :].astype(F32) + w4 * o4_ref[rows, :].astype(F32)
                      + w16 * o16_ref[rows, :].astype(F32)) / (w1 + w4 + w16))

        mixed = jnp.concatenate([
            (_silu(ag_ref[rows, :].astype(F32)) * out_a).astype(BF16),
            (_silu(bg_ref[rows, :].astype(F32)) * out_b).astype(BF16),
            (_silu(cg_ref[rows, :].astype(F32)) * oc_ref[rows, :].astype(F32)).astype(BF16),
            (_silu(dg_ref[rows, :].astype(F32)) * out_d[rows]).astype(BF16)], axis=1)
        y_ref[rows, :] = x_ref[rows, :] + jnp.dot(mixed, wout_ref[...], preferred_element_type=F32)


def _mix_out(bounded, x2, proj, dil, oc, sgu_g, ws_bf, bs_tab, conv_w, wout_bf, s):
    n = x2.shape[0]
    tiles_per_seq = s // MIX_T
    halo_per_tile = MIX_T // HALO
    n_halo = n // HALO

    def piece(p):
        return pl.BlockSpec((MIX_T, BRANCH_W), lambda i, p=p: (i, p))

    def halo_prev(p):
        return pl.BlockSpec((HALO, BRANCH_W), lambda i, p=p: (jnp.maximum(i * halo_per_tile - 1, 0), p))

    def halo_next(p):
        return pl.BlockSpec((HALO, BRANCH_W),
                            lambda i, p=p: (jnp.minimum((i + 1) * halo_per_tile, n_halo - 1), p))

    def tokens():
        return pl.BlockSpec((MIX_T, BRANCH_W), lambda i: (i, 0))

    def const(shape):
        return pl.BlockSpec(shape, lambda i: (0,) * len(shape))

    def residues(d):
        return pl.BlockSpec((1, d, MIX_T // d, BRANCH_W),
                            lambda i: (i // tiles_per_seq, 0, i % tiles_per_seq, 0))

    (o1, l1), (o4, l4), (o16, l16) = dil
    if bounded:
        dil_specs = [tokens(), residues(4), residues(16)] * 2
        scratch = [pltpu.VMEM((HEAD_PAIRS, MIX_T, LANES), F32)] * 4
    else:
        dil_specs = [tokens()] * 6
        scratch = []
    return pl.pallas_call(
        functools.partial(_mix_kernel, tiles_per_seq=tiles_per_seq, bounded=bounded),
        grid=(n // MIX_T,),
        in_specs=[pl.BlockSpec((MIX_T, D_MODEL), lambda i: (i, 0)),
                  piece(P_AU), piece(P_AV), piece(P_AG), piece(P_BG), piece(P_CG),
                  piece(P_DIN), piece(P_DB), piece(P_DC), piece(P_DG),
                  halo_prev(P_DIN), halo_prev(P_DC), halo_next(P_DIN), halo_next(P_DC),
                  *dil_specs, tokens(),
                  const((1, BRANCH_W)), const((GROUP_HEADS, CHUNK, CHUNK)), const((CHUNK, BRANCH_W)),
                  const((3, BRANCH_W)), const((MIX_W, D_MODEL))],
        out_specs=pl.BlockSpec((MIX_T, D_MODEL), lambda i: (i, 0)),
        out_shape=jax.ShapeDtypeStruct((n, D_MODEL), F32),
        scratch_shapes=scratch,
        compiler_params=_cparams(("arbitrary",)),
        name="mixout_bounded" if bounded else "mixout",
    )(x2, *([proj] * 13), o1, o4, o16, l1, l4, l16, oc,
      sgu_g.reshape(1, BRANCH_W), ws_bf, bs_tab, conv_w, wout_bf)


def _layer(x2, bsz, s, l, norm_g, w_in, sgu_g, w_s, b_s, qn_b, kn_b, qn_c, kn_c,
           lam_q1, lam_k1, lam_q2, lam_k2, subln_g, conv_w, w_out):
    slopes2 = (2.0 ** (-8.0 * jnp.arange(1, GROUP_HEADS + 1, dtype=F32) / GROUP_HEADS)) * LOG2E
    lambda_init = 0.8 - 0.6 * math.exp(-0.3 * l)
    lam = (jnp.exp(jnp.sum(lam_q1 * lam_k1)) - jnp.exp(jnp.sum(lam_q2 * lam_k2)) + lambda_init).reshape(1)

    proj = _inproj(x2, norm_g, w_in.astype(BF16))
    gqb = (jnp.tile(qn_b, GROUP_HEADS) * (LOG2E / math.sqrt(HEAD_DIM))).reshape(1, BRANCH_W)
    gkb = jnp.tile(kn_b, GROUP_HEADS).reshape(1, BRANCH_W)
    gqc = (jnp.tile(qn_c, 2 * GROUP_HEADS) * (LOG2E / math.sqrt(DIFF_QK_DIM))).reshape(1, BRANCH_W)
    gkc = jnp.tile(kn_c, 2 * GROUP_HEADS).reshape(1, BRANCH_W)
    bqn, bkn, kcn, qt, vt, q4, k4, v4, q16, k16, v16, k8, q8 = _prep(proj, bsz, s, gqb, gkb, gqc, gkc)

    bound_b = (jnp.max(jnp.abs(qn_b)) * jnp.max(jnp.abs(kn_b)) * (LOG2E * math.sqrt(HEAD_DIM))).reshape(1)
    bound_c = (jnp.max(jnp.abs(qn_c)) * jnp.max(jnp.abs(kn_c)) * (LOG2E * math.sqrt(DIFF_QK_DIM))).reshape(1)
    subln_pair = jnp.tile(subln_g, 2).reshape(1, LANES)
    bs_tab = jnp.repeat(b_s.T, HEAD_DIM, axis=1)
    ws_bf, wout_bf = w_s.astype(BF16), w_out.astype(BF16)

    def fixed_shift_path():
        operands = {1: (bqn.reshape(bsz, 1, s, BRANCH_W), bkn.reshape(bsz, 1, s, BRANCH_W),
                        proj.reshape(bsz, 1, s, PROJ_W), P_BV * HEAD_PAIRS),
                    4: (q4, k4, v4, 0), 16: (q16, k16, v16, 0)}
        dil = []
        for _, d in DIL_PATTERNS:
            qd, kd, vd, v_off = operands[d]
            num, den = _dilated_bounded(_dilated_tables(slopes2, bound_b[0], d), qd, kd, vd, v_off, bsz, s, d)
            if d == 1:
                num, den = num.reshape(bsz * s, BRANCH_W), den.reshape(bsz * s, BRANCH_W)
            dil.append((num, den))
        oc = _diff(True, lam, bound_c, slopes2, q8, k8, vt, subln_pair, bsz, s, lambda_init)
        return _mix_out(True, x2, proj, dil, oc, sgu_g, ws_bf, bs_tab, conv_w, wout_bf, s)

    def running_max_path():
        dil = [_dilated(slopes2, bqn, bkn, proj, bsz, s, d) for (_, d) in DIL_PATTERNS]
        oc = _diff(False, lam, bound_c, slopes2, qt, kcn, vt, subln_pair, bsz, s, lambda_init)
        return _mix_out(False, x2, proj, dil, oc, sgu_g, ws_bf, bs_tab, conv_w, wout_bf, s)

    use_fixed_shift = jnp.maximum(bound_b[0], bound_c[0]) <= MAX_FIXED_SHIFT
    return lax.cond(use_fixed_shift, fixed_shift_path, running_max_path)


def _trunk(x, params):
    bsz, s, _ = x.shape
    x2 = x.reshape(bsz * s, D_MODEL)
    depth = params[0].shape[0]
    for l in range(depth):
        x2 = _layer(x2, bsz, s, l, *[p[l] for p in params])
    return x2.reshape(bsz, s, D_MODEL)


def kernel(x_prompt, x_sample, norm_g, w_in, sgu_g, w_s, b_s, qn_b, kn_b, qn_c, kn_c,
           lam_q1, lam_k1, lam_q2, lam_k2, subln_g, conv_w, w_out):
    params = (norm_g, w_in, sgu_g, w_s, b_s, qn_b, kn_b, qn_c, kn_c,
              lam_q1, lam_k1, lam_q2, lam_k2, subln_g, conv_w, w_out)
    return (_trunk(x_prompt, params), _trunk(x_sample, params))
```

```python
import functools
import math

import jax
import jax.numpy as jnp
from jax import lax
from jax.experimental import pallas as pl
from jax.experimental.pallas import tpu as pltpu

F32 = jnp.float32
BF16 = jnp.bfloat16
F8 = jnp.float8_e4m3fn

D_MODEL = 1024
HEAD_DIM = 64
GROUP_HEADS = 6
BRANCH_W = GROUP_HEADS * HEAD_DIM
N_PIECES = 15
PROJ_W = N_PIECES * BRANCH_W
MIX_W = 4 * BRANCH_W
CHUNK = 128
DIL_PATTERNS = ((128, 1), (512, 4), (2048, 16))
N_SIDE = 64
DIFF_QK_DIM = HEAD_DIM // 2
EPS = 1e-6
NEG = -1e30
LOG2E = math.log2(math.e)

LANES = 128
HEAD_PAIRS = BRANCH_W // LANES
V_ROWS = 80

P_AU, P_AV, P_AG, P_BQ, P_BK, P_BV, P_BG, P_CQ, P_CK, P_CV, P_CG, P_DIN, P_DB, P_DC, P_DG = range(15)

IN_TM = 1024
IN_TN = 1920
PREP_T = 1024
PREP_SUB = 128
MIX_T = 512
MIX_SUB = 512
HALO = 16
DIL_TQ = 128
DIL_WIN = DIL_TQ + 2 * N_SIDE
DIL_ROWS = 4096
DIL_GROUP = 16
DIFF_TQ = 256
DIFF_TQ_FIXED = 512
DIFF_SUBTILES = 2
DIFF_TRIP = 4
DIFF_TK = 256
VMEM_LIMIT = 56 * 1024 * 1024
MAX_FIXED_SHIFT = 40.0


def _cparams(sem):
    return pltpu.CompilerParams(dimension_semantics=sem, vmem_limit_bytes=VMEM_LIMIT)


def _inproj_kernel(x_ref, g_ref, w_ref, o_ref):
    x = x_ref[...]
    ms = jnp.mean(x * x, axis=-1, keepdims=True)
    h = (x * lax.rsqrt(ms + EPS) * g_ref[...]).astype(BF16)
    o_ref[...] = jnp.dot(h, w_ref[...], preferred_element_type=F32).astype(BF16)


def _inproj(x2, norm_g, w_in_bf):
    n = x2.shape[0]
    return pl.pallas_call(
        _inproj_kernel,
        grid=(PROJ_W // IN_TN, n // IN_TM),
        in_specs=[
            pl.BlockSpec((IN_TM, D_MODEL), lambda j, i: (i, 0)),
            pl.BlockSpec((1, D_MODEL), lambda j, i: (0, 0)),
            pl.BlockSpec((D_MODEL, IN_TN), lambda j, i: (0, j)),
        ],
        out_specs=pl.BlockSpec((IN_TM, IN_TN), lambda j, i: (i, j)),
        out_shape=jax.ShapeDtypeStruct((n, PROJ_W), BF16),
        compiler_params=_cparams(("arbitrary", "arbitrary")),
        name="inproj",
    )(x2, norm_g.reshape(1, D_MODEL), w_in_bf)


def _group_rms(x, e_ref, width):
    x2 = x * x
    hi = x2.astype(BF16)
    lo = (x2 - hi.astype(F32)).astype(BF16)
    e = e_ref[...]
    ss = jnp.dot(hi, e, preferred_element_type=F32) + jnp.dot(lo, e, preferred_element_type=F32)
    return x * lax.rsqrt(ss * (1.0 / width) + EPS)


def _prep_kernel(bq_ref, bk_ref, bv_ref, cq_ref, ck_ref, cv_ref, gqb_ref, gkb_ref, gqc_ref, gkc_ref,
                 e64_ref, e32_ref, bqn_ref, bkn_ref, kcn_ref, qt_ref, vt_ref,
                 q4_ref, k4_ref, v4_ref, q16_ref, k16_ref, v16_ref, k8_ref, q8_ref, sq_ref, sk_ref, sv_ref):
    t = bq_ref.shape[0]
    low_lanes = lax.broadcasted_iota(jnp.int32, (PREP_SUB, LANES), 1) < HEAD_DIM
    row = lax.broadcasted_iota(jnp.int32, (V_ROWS - HEAD_DIM, PREP_SUB), 0)
    ones_pad = jnp.where(row == 0, 1.0, 0.0).astype(BF16)
    for cb in range(HEAD_PAIRS):
        sl = slice(cb * LANES, (cb + 1) * LANES)
        for r0 in range(0, t, PREP_SUB):
            rs = slice(r0, r0 + PREP_SUB)
            qb = _group_rms(bq_ref[rs, sl].astype(F32), e64_ref, HEAD_DIM) * gqb_ref[:, sl]
            kb = _group_rms(bk_ref[rs, sl].astype(F32), e64_ref, HEAD_DIM) * gkb_ref[:, sl]
            sq_ref[cb, rs, :] = qb
            sk_ref[cb, rs, :] = kb
            sv_ref[cb, rs, :] = bv_ref[rs, sl].astype(F32)
            bqn_ref[rs, sl] = qb.astype(BF16)
            bkn_ref[rs, sl] = kb.astype(BF16)
            kn = _group_rms(ck_ref[rs, sl].astype(F32), e32_ref, DIFF_QK_DIM) * gkc_ref[:, sl]
            kcn_ref[rs, sl] = kn.astype(BF16)
            k_hi = kn.astype(F8).astype(F32)
            k_lo_swapped = pltpu.roll((kn - k_hi).astype(F8).astype(F32), HEAD_DIM, axis=1)
            zero = jnp.zeros_like(kn)
            k8_ref[0, cb, 0, rs, 0:LANES] = jnp.where(low_lanes, k_hi, k_lo_swapped).astype(F8)
            k8_ref[0, cb, 0, rs, LANES:2 * LANES] = jnp.where(low_lanes, k_hi, zero).astype(F8)
            k8_ref[0, cb, 1, rs, 0:LANES] = jnp.where(low_lanes, k_lo_swapped, k_hi).astype(F8)
            k8_ref[0, cb, 1, rs, LANES:2 * LANES] = jnp.where(low_lanes, zero, k_hi).astype(F8)
            qn_t = (_group_rms(cq_ref[rs, sl].astype(F32), e32_ref, DIFF_QK_DIM) * gqc_ref[:, sl]).T
            qt_ref[0, cb, :, rs] = qn_t.astype(BF16)
            q_hi = qn_t.astype(F8)
            q8_ref[0, cb, 0, :, rs] = q_hi
            q8_ref[0, cb, 1, :, rs] = (qn_t - q_hi.astype(F32)).astype(F8)
            vt = cv_ref[rs, sl].astype(F32).T.astype(BF16)
            for hh in range(2):
                vt_ref[0, cb, hh, 0:HEAD_DIM, rs] = vt[hh * HEAD_DIM:(hh + 1) * HEAD_DIM, :]
                vt_ref[0, cb, hh, HEAD_DIM:V_ROWS, rs] = ones_pad
    for stage, out4, out16 in ((sq_ref, q4_ref, q16_ref), (sk_ref, k4_ref, k16_ref), (sv_ref, v4_ref, v16_ref)):
        for dilation, out in ((4, out4), (16, out16)):
            for r in range(dilation):
                for cb in range(HEAD_PAIRS):
                    out[0, r, :, cb * LANES:(cb + 1) * LANES] = (
                        stage[cb, pl.ds(r, t // dilation, stride=dilation), :].astype(BF16))


def _block_diag_ones(group):
    idx = jnp.arange(LANES) // group
    return (idx[:, None] == idx[None, :]).astype(BF16)


def _prep(proj, bsz, s, gqb, gkb, gqc, gkc):
    n = proj.shape[0]
    tiles_per_seq = s // PREP_T

    def piece(p):
        return pl.BlockSpec((PREP_T, BRANCH_W), lambda i, p=p: (i, p))

    def vec():
        return pl.BlockSpec((1, BRANCH_W), lambda i: (0, 0))

    def tokens():
        return pl.BlockSpec((PREP_T, BRANCH_W), lambda i: (i, 0))

    def residues(d):
        return pl.BlockSpec((1, d, PREP_T // d, BRANCH_W),
                            lambda i: (i // tiles_per_seq, 0, i % tiles_per_seq, 0))

    return pl.pallas_call(
        _prep_kernel,
        grid=(n // PREP_T,),
        in_specs=[piece(P_BQ), piece(P_BK), piece(P_BV), piece(P_CQ), piece(P_CK), piece(P_CV),
                  vec(), vec(), vec(), vec(),
                  pl.BlockSpec((LANES, LANES), lambda i: (0, 0)),
                  pl.BlockSpec((LANES, LANES), lambda i: (0, 0))],
        out_specs=[tokens(), tokens(), tokens(),
                   pl.BlockSpec((1, HEAD_PAIRS, LANES, PREP_T),
                                lambda i: (i // tiles_per_seq, 0, 0, i % tiles_per_seq)),
                   pl.BlockSpec((1, HEAD_PAIRS, 2, V_ROWS, PREP_T),
                                lambda i: (i // tiles_per_seq, 0, 0, 0, i % tiles_per_seq))]
                  + [residues(d) for d in (4, 4, 4, 16, 16, 16)]
                  + [pl.BlockSpec((1, HEAD_PAIRS, 2, PREP_T, 2 * LANES),
                                  lambda i: (i // tiles_per_seq, 0, 0, i % tiles_per_seq, 0)),
                     pl.BlockSpec((1, HEAD_PAIRS, 2, LANES, PREP_T),
                                  lambda i: (i // tiles_per_seq, 0, 0, 0, i % tiles_per_seq))],
        out_shape=[jax.ShapeDtypeStruct((n, BRANCH_W), BF16),
                   jax.ShapeDtypeStruct((n, BRANCH_W), BF16),
                   jax.ShapeDtypeStruct((n, BRANCH_W), BF16),
                   jax.ShapeDtypeStruct((bsz, HEAD_PAIRS, LANES, s), BF16),
                   jax.ShapeDtypeStruct((bsz, HEAD_PAIRS, 2, V_ROWS, s), BF16)]
                  + [jax.ShapeDtypeStruct((bsz, d, s // d, BRANCH_W), BF16) for d in (4, 4, 4, 16, 16, 16)]
                  + [jax.ShapeDtypeStruct((bsz, HEAD_PAIRS, 2, s, 2 * LANES), F8),
                     jax.ShapeDtypeStruct((bsz, HEAD_PAIRS, 2, LANES, s), F8)],
        scratch_shapes=[pltpu.VMEM((HEAD_PAIRS, PREP_T, LANES), F32)] * 3,
        compiler_params=_cparams(("arbitrary",)),
        name="prep",
    )(proj, proj, proj, proj, proj, proj, gqb, gkb, gqc, gkc,
      _block_diag_ones(HEAD_DIM), _block_diag_ones(DIFF_QK_DIM))


def _dilated_kernel(slope_ref, q_ref, k_ref, v_ref, o_ref, lse_ref, *, dilation, seq_sub):
    hp = pl.program_id(2)
    qi = pl.program_id(3)
    tqb = q_ref.shape[1]
    lane = lax.broadcasted_iota(jnp.int32, (DIL_TQ, LANES), 1)
    first_head = lane < HEAD_DIM
    rel0 = (lax.broadcasted_iota(jnp.int32, (DIL_TQ, DIL_WIN), 1)
            - lax.broadcasted_iota(jnp.int32, (DIL_TQ, DIL_WIN), 0))

    def sub_block(sb, carry):
        r0 = pl.multiple_of(sb * DIL_TQ, DIL_TQ)
        i0 = qi * tqb + r0
        ws = pl.multiple_of(jnp.clip(i0 - N_SIDE, 0, seq_sub - DIL_WIN), N_SIDE)
        kw = k_ref[0, pl.ds(ws, DIL_WIN), :]
        vw = v_ref[0, pl.ds(ws, DIL_WIN), :]
        q = q_ref[0, pl.ds(r0, DIL_TQ), :]
        dist = jnp.abs(rel0 + (ws - i0))
        valid = dist <= N_SIDE
        distf = dist.astype(F32)
        outs, lses = [], []
        for hh in range(2):
            slope = slope_ref[2 * hp + hh] * float(dilation)
            qm = jnp.where(first_head if hh == 0 else jnp.logical_not(first_head), q, jnp.zeros_like(q))
            sc = lax.dot_general(qm, kw, (((1,), (1,)), ((), ())), preferred_element_type=F32)
            sc = jnp.where(valid, sc - slope * distf, NEG)
            m = jnp.max(sc, axis=-1, keepdims=True)
            p = jnp.exp2(sc - m)
            den = jnp.sum(p, axis=-1, keepdims=True)
            o = jnp.dot(p.astype(BF16), vw, preferred_element_type=F32) / den
            outs.append(o)
            lses.append(jnp.broadcast_to(m + jnp.log2(den), (DIL_TQ, LANES)))
        o_ref[0, pl.ds(r0, DIL_TQ), :] = jnp.where(first_head, outs[0], outs[1]).astype(o_ref.dtype)
        lse_ref[0, pl.ds(r0, DIL_TQ), :] = jnp.where(first_head, lses[0], lses[1])
        return carry

    lax.fori_loop(0, tqb // DIL_TQ, sub_block, 0)


def _dilated(slopes2, bqn, bkn, proj, bsz, s, dilation):
    seq_sub = s // dilation
    tqb = min(seq_sub, 1024)
    q3 = bqn.reshape(bsz, seq_sub, dilation * BRANCH_W)
    k3 = bkn.reshape(bsz, seq_sub, dilation * BRANCH_W)
    v3 = proj.reshape(bsz, seq_sub, dilation * PROJ_W)
    blocks_q = BRANCH_W // LANES
    blocks_p = PROJ_W // LANES
    v_off = P_BV * blocks_q

    def tile():
        return pl.BlockSpec((1, tqb, LANES), lambda b, r, hp, qi: (b, qi, r * blocks_q + hp))

    o, lse = pl.pallas_call(
        functools.partial(_dilated_kernel, dilation=dilation, seq_sub=seq_sub),
        grid=(bsz, dilation, HEAD_PAIRS, seq_sub // tqb),
        in_specs=[
            pl.BlockSpec(memory_space=pltpu.SMEM),
            tile(),
            pl.BlockSpec((1, seq_sub, LANES), lambda b, r, hp, qi: (b, 0, r * blocks_q + hp)),
            pl.BlockSpec((1, seq_sub, LANES), lambda b, r, hp, qi: (b, 0, r * blocks_p + v_off + hp)),
        ],
        out_specs=[tile(), tile()],
        out_shape=[jax.ShapeDtypeStruct(q3.shape, BF16), jax.ShapeDtypeStruct(q3.shape, F32)],
        compiler_params=_cparams(("arbitrary",) * 4),
        name=f"dilated{dilation}",
    )(slopes2, q3, k3, v3)
    return o.reshape(bsz * s, BRANCH_W), lse.reshape(bsz * s, BRANCH_W)


def _dilated_bounded_kernel(tab_ref, q_ref, k_ref, v_ref, num_ref, den_ref, p_ref, *, seq_sub):
    qi = pl.program_id(3)
    n_res, tqb = q_ref.shape[1], q_ref.shape[2]
    sub_per_res = tqb // DIL_TQ
    first_head_q = lax.broadcasted_iota(jnp.int32, (DIL_TQ, LANES), 1) < HEAD_DIM
    v_lane = lax.broadcasted_iota(jnp.int32, (DIL_WIN, 2 * LANES), 1)
    value_lanes = (v_lane < HEAD_DIM) | (v_lane >= 2 * LANES - HEAD_DIM)

    def window(sb):
        res = sb // sub_per_res
        r0 = pl.multiple_of((sb % sub_per_res) * DIL_TQ, DIL_TQ)
        i0 = qi * tqb + r0
        ws = pl.multiple_of(jnp.clip(i0 - N_SIDE, 0, seq_sub - DIL_WIN), N_SIDE)
        variant = jnp.where(i0 == 0, 0, jnp.where(i0 == seq_sub - DIL_TQ, 2, 1))
        return res, r0, ws, variant

    def probabilities(u):
        for half in range(DIL_GROUP):
            res, r0, ws, variant = window(DIL_GROUP * u + half)
            kw = k_ref[0, res, pl.ds(ws, DIL_WIN), :]
            q = q_ref[0, res, pl.ds(r0, DIL_TQ), :]
            zero = jnp.zeros_like(q)
            q2 = jnp.concatenate([jnp.where(first_head_q, q, zero), jnp.where(first_head_q, zero, q)], axis=0)
            sc = lax.dot_general(q2, kw, (((1,), (1,)), ((), ())), preferred_element_type=F32)
            for hh in range(2):
                p_ref[half, hh * DIL_TQ:(hh + 1) * DIL_TQ, :] = jnp.exp2(
                    sc[hh * DIL_TQ:(hh + 1) * DIL_TQ] + tab_ref[hh, variant]).astype(BF16)

    def outputs(u):
        for half in range(DIL_GROUP):
            res, r0, ws, _ = window(DIL_GROUP * u + half)
            vw = v_ref[0, res, pl.ds(ws, DIL_WIN), :]
            vw2 = jnp.concatenate([vw, vw], axis=1)
            vcat = jnp.where(value_lanes, vw2, jnp.ones_like(vw2))
            a = jnp.dot(p_ref[half], vcat, preferred_element_type=F32)
            a0 = a[0:DIL_TQ, 0:LANES]
            a1 = a[DIL_TQ:2 * DIL_TQ, LANES:2 * LANES]
            num_ref[0, res, pl.ds(r0, DIL_TQ), :] = jnp.where(first_head_q, a0, a1).astype(num_ref.dtype)
            den_ref[0, res, pl.ds(r0, DIL_TQ), :] = pltpu.roll(jnp.where(first_head_q, a1, a0), HEAD_DIM, axis=1)

    n_groups = n_res * sub_per_res // DIL_GROUP
    probabilities(0)

    def step(u, carry):
        outputs(u - 1)
        probabilities(u)
        return carry

    lax.fori_loop(1, n_groups, step, 0)
    outputs(n_groups - 1)


def _dilated_tables(slopes2, bound, dilation):
    rel = ((jnp.arange(DIL_WIN)[None, :] - jnp.arange(DIL_TQ)[:, None])[None]
           - jnp.array([0, N_SIDE, 2 * N_SIDE])[:, None, None])
    dist = jnp.abs(rel).astype(F32)
    bias = -(slopes2 * float(dilation))[:, None, None, None] * dist[None] - bound
    return jnp.where(dist[None] <= N_SIDE, bias, NEG).astype(F32)


def _dilated_bounded(tables, q4d, k4d, v4d, v_block_offset, bsz, s, dilation):
    seq_sub = s // dilation
    tqb = min(seq_sub, DIL_ROWS)
    n_res = min(dilation, DIL_ROWS // tqb)

    def tile():
        return pl.BlockSpec((1, n_res, tqb, LANES), lambda b, r, hp, qi: (b, r, qi, hp))

    out_shape = (bsz, dilation, seq_sub, BRANCH_W)
    return pl.pallas_call(
        functools.partial(_dilated_bounded_kernel, seq_sub=seq_sub),
        grid=(bsz, dilation // n_res, HEAD_PAIRS, seq_sub // tqb),
        in_specs=[
            pl.BlockSpec((2, 3, DIL_TQ, DIL_WIN), lambda b, r, hp, qi: (hp, 0, 0, 0)),
            tile(),
            pl.BlockSpec((1, n_res, seq_sub, LANES), lambda b, r, hp, qi: (b, r, 0, hp)),
            pl.BlockSpec((1, n_res, seq_sub, LANES), lambda b, r, hp, qi: (b, r, 0, v_block_offset + hp)),
        ],
        out_specs=[tile(), tile()],
        out_shape=[jax.ShapeDtypeStruct(out_shape, BF16), jax.ShapeDtypeStruct(out_shape, F32)],
        scratch_shapes=[pltpu.VMEM((DIL_GROUP, 2 * DIL_TQ, DIL_WIN), BF16)],
        compiler_params=_cparams(("arbitrary",) * 4),
        name=f"dilated_bounded{dilation}",
    )(tables, q4d, k4d, v4d)


def _diff_kernel(lam_ref, slope_ref, qt_ref, k_ref, vt_ref, g_ref, o_ref, qm_ref, tm_ref, acc_ref,
                 *, out_scale):
    hp = pl.program_id(1)
    qi = pl.program_id(2)
    tq = qt_ref.shape[3]
    tk = DIFF_TK
    nk = k_ref.shape[1] // tk
    i0 = qi * tq

    qt = qt_ref[0, 0]
    row_group = lax.broadcasted_iota(jnp.int32, (LANES, tq), 0) // DIFF_QK_DIM
    for g in range(4):
        qm_ref[g] = jnp.where(row_group == g, qt, jnp.zeros_like(qt))
    dkq = (lax.broadcasted_iota(jnp.int32, (tk, tq), 0)
           - lax.broadcasted_iota(jnp.int32, (tk, tq), 1)).astype(F32)
    for hh in range(2):
        tm_ref[hh] = slope_ref[2 * hp + hh] * dkq
    acc_ref[...] = jnp.zeros_like(acc_ref)

    def key_tile(j, ms, mode):
        koff = pl.multiple_of(j * tk, tk)
        kt = k_ref[0, pl.ds(koff, tk), :]
        tile_dist = jnp.abs(i0 - koff).astype(F32)
        new_ms = []
        for hh in range(2):
            vt = vt_ref[0, 0, hh, :, pl.ds(koff, tk)]
            shift = slope_ref[2 * hp + hh] * tile_dist
            for c in range(2):
                g = 2 * hh + c
                sc = jnp.dot(kt, qm_ref[g], preferred_element_type=F32)
                if mode < 0:
                    sc = sc + tm_ref[hh]
                elif mode > 0:
                    sc = sc - tm_ref[hh]
                else:
                    sc = sc - jnp.abs(tm_ref[hh])
                m_old = ms[g]
                m_new = jnp.maximum(m_old, jnp.max(sc, axis=0, keepdims=True) - shift)
                p = jnp.exp2(sc - (m_new + shift))
                alpha = jnp.exp2(m_old - m_new)
                acc_ref[g] = alpha * acc_ref[g] + jnp.dot(vt, p.astype(BF16), preferred_element_type=F32)
                new_ms.append(m_new)
        return tuple(new_ms)

    ms = tuple(jnp.full((1, tq), NEG, F32) for _ in range(4))
    ms = lax.fori_loop(0, qi, lambda j, m: key_tile(j, m, -1), ms)
    ms = key_tile(qi, ms, 0)
    lax.fori_loop(qi + 1, nk, lambda j, m: key_tile(j, m, 1), ms)
    _diff_finish(lam_ref, g_ref, o_ref, acc_ref, out_scale)


def _diff_bounded_kernel(lam_ref, slope_ref, qt_ref, k_ref, vt_ref, g_ref, o_ref, qm_ref, tm_ref, acc_ref,
                         p_ref, *, out_scale):
    hp = pl.program_id(1)
    qi = pl.program_id(2)
    n_sub = DIFF_SUBTILES
    tq = qt_ref.shape[4] // n_sub
    tk = DIFF_TK
    tiles_per_q = tq // tk
    nk = k_ref.shape[3] // tk
    centre = 0.5 * lam_ref[1]

    zeros = jnp.zeros((DIFF_QK_DIM, tq), qt_ref.dtype)
    for t in range(n_sub):
        q_hi = qt_ref[0, 0, 0, :, t * tq:(t + 1) * tq]
        q_lo = qt_ref[0, 0, 1, :, t * tq:(t + 1) * tq]
        for g in range(4):
            hh, c = divmod(g, 2)
            hi_block, lo_block, hi2_block = ((0, 2, 4), (2, 0, 6))[hh]
            rows = slice(g * DIFF_QK_DIM, (g + 1) * DIFF_QK_DIM)
            blocks = [zeros] * (2 * LANES // DIFF_QK_DIM)
            blocks[hi_block + c], blocks[lo_block + c], blocks[hi2_block + c] = q_hi[rows], q_hi[rows], q_lo[rows]
            qm_ref[t, g] = jnp.concatenate(blocks, axis=0)
    @pl.when(qi == 0)
    def _():
        dkq = (lax.broadcasted_iota(jnp.int32, (tk, tq), 0)
               - lax.broadcasted_iota(jnp.int32, (tk, tq), 1)).astype(F32)
        for hh in range(2):
            slope = slope_ref[2 * hp + hh]
            tm = slope * dkq
            tm_ref[hh, 0] = tm
            tm_ref[hh, 1] = -tm
            for r in range(tiles_per_q):
                origin = float(r * tk)
                tm_ref[hh, 2 + r] = slope * (origin - jnp.abs(dkq + origin))

    acc_ref[...] = jnp.zeros_like(acc_ref)

    def stage(scores=None, values=None):
        for hh in range(2):
            slope = slope_ref[2 * hp + hh]
            for c in range(2):
                g = 2 * hh + c
                parts = []
                for half in range(2):
                    if values is not None:
                        t, u, slot = values
                        koff = pl.multiple_of((2 * u + half) * tk, tk)
                        parts.append(jnp.dot(vt_ref[0, 0, hh, :, pl.ds(koff, tk)], p_ref[slot, half, g],
                                             preferred_element_type=F32))
                    if scores is not None:
                        t, u, slot = scores
                        j = 2 * u + half
                        koff = pl.multiple_of(j * tk, tk)
                        q_tile = qi * n_sub + t
                        shift = -(slope * jnp.abs(q_tile * tq - koff).astype(F32) + centre)
                        first = tiles_per_q * q_tile
                        table = jnp.where(j < first, 0, jnp.where(j >= first + tiles_per_q, 1, 2 + j - first))
                        sc = jnp.dot(k_ref[0, 0, hh, pl.ds(koff, tk), :], qm_ref[t, g],
                                     preferred_element_type=F32)
                        p_ref[slot, half, g] = jnp.exp2((sc + tm_ref[hh, table] + shift).astype(BF16))
                if parts:
                    acc_ref[values[0], g] += parts[0] + parts[1]

    n_pairs = nk // 2
    stage(scores=(0, 0, 0))
    for t in range(n_sub):
        def steps(first, count, t=t):
            for k in range(count):
                stage(scores=(t, first + k, (1 + k) % 2), values=(t, first + k - 1, k % 2))

        trip_steps = DIFF_TRIP if (n_pairs - 2) // DIFF_TRIP >= 2 else 2

        def trip(w, carry):
            steps(trip_steps * w + 1, trip_steps)
            return carry

        n_trips = (n_pairs - 2) // trip_steps
        lax.fori_loop(0, n_trips, trip, 0)
        steps(trip_steps * n_trips + 1, n_pairs - 2 - trip_steps * n_trips)
        stage(scores=(t, n_pairs - 1, 1), values=(t, n_pairs - 2, 0))
        stage(scores=(t + 1, 0, 0) if t + 1 < n_sub else None, values=(t, n_pairs - 1, 1))
        _diff_finish(lam_ref, g_ref, o_ref, acc_ref.at[t], out_scale, slice(t * tq, (t + 1) * tq))


def _diff_finish(lam_ref, g_ref, o_ref, acc_ref, out_scale, out_rows=slice(None)):
    lam = lam_ref[0]
    heads = []
    for hh in range(2):
        a0 = acc_ref[2 * hh]
        a1 = acc_ref[2 * hh + 1]
        o = (a0[0:HEAD_DIM] / a0[HEAD_DIM:HEAD_DIM + 1]
             - lam * (a1[0:HEAD_DIM] / a1[HEAD_DIM:HEAD_DIM + 1]))
        ms2 = jnp.mean(o * o, axis=0, keepdims=True)
        heads.append(o * lax.rsqrt(ms2 + EPS))
    out = jnp.concatenate(heads, axis=0).T
    o_ref[0, out_rows, :] = (out * (g_ref[...] * out_scale)).astype(o_ref.dtype)


def _diff(bounded, lam, score_bound, slopes2, q_t, k, vt, subln_pair, bsz, s, lambda_init):
    scalars = jnp.concatenate([lam, score_bound]).astype(F32)
    if bounded:
        body, name = _diff_bounded_kernel, "diff_bounded"
        sub = DIFF_TQ_FIXED
        tq = DIFF_SUBTILES * sub
        q_spec = pl.BlockSpec((1, 1, 2, LANES, tq), lambda b, hp, qi: (b, hp, 0, 0, qi))
        k_spec = pl.BlockSpec((1, 1, 2, s, 2 * LANES), lambda b, hp, qi: (b, hp, 0, 0, 0))
        scratch = [pltpu.VMEM((DIFF_SUBTILES, 4, 2 * LANES, sub), F8),
                   pltpu.VMEM((2, 2 + sub // DIFF_TK, DIFF_TK, sub), F32),
                   pltpu.VMEM((DIFF_SUBTILES, 4, V_ROWS, sub), F32),
                   pltpu.VMEM((2, 2, 4, DIFF_TK, sub), BF16)]
    else:
        body, name, tq = _diff_kernel, "diff_online", DIFF_TQ
        k = k.reshape(bsz, s, BRANCH_W)
        q_spec = pl.BlockSpec((1, 1, LANES, tq), lambda b, hp, qi: (b, hp, 0, qi))
        k_spec = pl.BlockSpec((1, s, LANES), lambda b, hp, qi: (b, 0, hp))
        scratch = [pltpu.VMEM((4, LANES, tq), BF16), pltpu.VMEM((2, DIFF_TK, tq), F32),
                   pltpu.VMEM((4, V_ROWS, tq), F32)]
    return pl.pallas_call(
        functools.partial(body, out_scale=1.0 - lambda_init),
        grid=(bsz, HEAD_PAIRS, s // tq),
        in_specs=[
            pl.BlockSpec(memory_space=pltpu.SMEM),
            pl.BlockSpec(memory_space=pltpu.SMEM),
            q_spec,
            k_spec,
            pl.BlockSpec((1, 1, 2, V_ROWS, s), lambda b, hp, qi: (b, hp, 0, 0, 0)),
            pl.BlockSpec((1, LANES), lambda b, hp, qi: (0, 0)),
        ],
        out_specs=pl.BlockSpec((1, tq, LANES), lambda b, hp, qi: (b, qi, hp)),
        out_shape=jax.ShapeDtypeStruct((bsz, s, BRANCH_W), BF16),
        scratch_shapes=scratch,
        compiler_params=_cparams(("arbitrary",) * 3),
        name=name,
    )(scalars, slopes2, q_t, k, vt, subln_pair).reshape(bsz * s, BRANCH_W)


def _silu(x):
    half = 0.5 * x
    return half + half * jnp.tanh(half)


def _mix_kernel(x_ref, au_ref, av_ref, ag_ref, bg_ref, cg_ref, din_ref, db_ref, dc_ref, dg_ref,
                din_p_ref, dc_p_ref, din_n_ref, dc_n_ref,
                o1_ref, o4_ref, o16_ref, l1_ref, l4_ref, l16_ref, oc_ref,
                sgu_ref, ws_ref, bs_ref, cw_ref, wout_ref, y_ref, *stage_refs, tiles_per_seq, bounded):
    i = pl.program_id(0)
    t = x_ref.shape[0]
    first_head = lax.broadcasted_iota(jnp.int32, (CHUNK, LANES), 1) < HEAD_DIM

    if bounded:
        for stage, blk, dilation in zip(stage_refs, (o4_ref, o16_ref, l4_ref, l16_ref), (4, 16, 4, 16)):
            for r in range(dilation):
                for cb in range(HEAD_PAIRS):
                    stage[cb, pl.ds(r, t // dilation, stride=dilation), :] = (
                        blk[0, r, :, cb * LANES:(cb + 1) * LANES].astype(F32))

    z = dc_ref[...].astype(F32) * din_ref[...].astype(F32)
    seq_pos = i % tiles_per_seq
    z_prev = (dc_p_ref[...].astype(F32) * din_p_ref[...].astype(F32))[HALO - 1:HALO, :]
    z_next = (dc_n_ref[...].astype(F32) * din_n_ref[...].astype(F32))[0:1, :]
    z_prev = jnp.where(seq_pos == 0, 0.0, z_prev)
    z_next = jnp.where(seq_pos == tiles_per_seq - 1, 0.0, z_next)
    row = lax.broadcasted_iota(jnp.int32, z.shape, 0)
    z_up = jnp.where(row == 0, z_prev, pltpu.roll(z, 1, axis=0))
    z_dn = jnp.where(row == t - 1, z_next, pltpu.roll(z, t - 1, axis=0))
    cw = cw_ref[...]
    out_d = db_ref[...].astype(F32) * (cw[0:1] * z_up + cw[1:2] * z + cw[2:3] * z_dn)

    for sb in range(t // MIX_SUB):
        rows = slice(sb * MIX_SUB, (sb + 1) * MIX_SUB)

        av = av_ref[rows, :].astype(F32)
        vn = (av * lax.rsqrt(jnp.mean(av * av, axis=-1, keepdims=True) + EPS) * sgu_ref[...]).astype(BF16)
        chunks = []
        for c in range(MIX_SUB // CHUNK):
            vc = vn[c * CHUNK:(c + 1) * CHUNK, :]
            pairs = []
            for hp in range(HEAD_PAIRS):
                vp = vc[:, hp * LANES:(hp + 1) * LANES]
                r0 = jnp.dot(ws_ref[2 * hp], vp, preferred_element_type=F32)
                r1 = jnp.dot(ws_ref[2 * hp + 1], vp, preferred_element_type=F32)
                pairs.append(jnp.where(first_head, r0, r1))
            chunks.append(jnp.concatenate(pairs, axis=1) + bs_ref[...])
        out_a = au_ref[rows, :].astype(F32) * jnp.concatenate(chunks, axis=0)

        if bounded:
            def in_order(stage):
                return jnp.concatenate([stage[cb, rows, :] for cb in range(HEAD_PAIRS)], axis=1)

            num = o1_ref[rows, :].astype(F32) + in_order(stage_refs[0]) + in_order(stage_refs[1])
            den = l1_ref[rows, :] + in_order(stage_refs[2]) + in_order(stage_refs[3])
            out_b = num / den
        else:
            l1, l4, l16 = l1_ref[rows, :], l4_ref[rows, :], l16_ref[rows, :]
            lm = jnp.maximum(jnp.maximum(l1, l4), l16)
            w1, w4, w16 = jnp.exp2(l1 - lm), jnp.exp2(l4 - lm), jnp.exp2(l16 - lm)
            out_b = ((w1 * o1_ref[rows, :].astype(F32) + w4 * o4_ref[rows, :].astype(F32)
                      + w16 * o16_ref[rows, :].astype(F32)) / (w1 + w4 + w16))

        mixed = jnp.concatenate([
            (_silu(ag_ref[rows, :].astype(F32)) * out_a).astype(BF16),
            (_silu(bg_ref[rows, :].astype(F32)) * out_b).astype(BF16),
            (_silu(cg_ref[rows, :].astype(F32)) * oc_ref[rows, :].astype(F32)).astype(BF16),
            (_silu(dg_ref[rows, :].astype(F32)) * out_d[rows]).astype(BF16)], axis=1)
        y_ref[rows, :] = x_ref[rows, :] + jnp.dot(mixed, wout_ref[...], preferred_element_type=F32)


def _mix_out(bounded, x2, proj, dil, oc, sgu_g, ws_bf, bs_tab, conv_w, wout_bf, s):
    n = x2.shape[0]
    tiles_per_seq = s // MIX_T
    halo_per_tile = MIX_T // HALO
    n_halo = n // HALO

    def piece(p):
        return pl.BlockSpec((MIX_T, BRANCH_W), lambda i, p=p: (i, p))

    def halo_prev(p):
        return pl.BlockSpec((HALO, BRANCH_W), lambda i, p=p: (jnp.maximum(i * halo_per_tile - 1, 0), p))

    def halo_next(p):
        return pl.BlockSpec((HALO, BRANCH_W),
                            lambda i, p=p: (jnp.minimum((i + 1) * halo_per_tile, n_halo - 1), p))

    def tokens():
        return pl.BlockSpec((MIX_T, BRANCH_W), lambda i: (i, 0))

    def const(shape):
        return pl.BlockSpec(shape, lambda i: (0,) * len(shape))

    def residues(d):
        return pl.BlockSpec((1, d, MIX_T // d, BRANCH_W),
                            lambda i: (i // tiles_per_seq, 0, i % tiles_per_seq, 0))

    (o1, l1), (o4, l4), (o16, l16) = dil
    if bounded:
        dil_specs = [tokens(), residues(4), residues(16)] * 2
        scratch = [pltpu.VMEM((HEAD_PAIRS, MIX_T, LANES), F32)] * 4
    else:
        dil_specs = [tokens()] * 6
        scratch = []
    return pl.pallas_call(
        functools.partial(_mix_kernel, tiles_per_seq=tiles_per_seq, bounded=bounded),
        grid=(n // MIX_T,),
        in_specs=[pl.BlockSpec((MIX_T, D_MODEL), lambda i: (i, 0)),
                  piece(P_AU), piece(P_AV), piece(P_AG), piece(P_BG), piece(P_CG),
                  piece(P_DIN), piece(P_DB), piece(P_DC), piece(P_DG),
                  halo_prev(P_DIN), halo_prev(P_DC), halo_next(P_DIN), halo_next(P_DC),
                  *dil_specs, tokens(),
                  const((1, BRANCH_W)), const((GROUP_HEADS, CHUNK, CHUNK)), const((CHUNK, BRANCH_W)),
                  const((3, BRANCH_W)), const((MIX_W, D_MODEL))],
        out_specs=pl.BlockSpec((MIX_T, D_MODEL), lambda i: (i, 0)),
        out_shape=jax.ShapeDtypeStruct((n, D_MODEL), F32),
        scratch_shapes=scratch,
        compiler_params=_cparams(("arbitrary",)),
        name="mixout_bounded" if bounded else "mixout",
    )(x2, *([proj] * 13), o1, o4, o16, l1, l4, l16, oc,
      sgu_g.reshape(1, BRANCH_W), ws_bf, bs_tab, conv_w, wout_bf)


def _layer(x2, bsz, s, l, norm_g, w_in, sgu_g, w_s, b_s, qn_b, kn_b, qn_c, kn_c,
           lam_q1, lam_k1, lam_q2, lam_k2, subln_g, conv_w, w_out):
    slopes2 = (2.0 ** (-8.0 * jnp.arange(1, GROUP_HEADS + 1, dtype=F32) / GROUP_HEADS)) * LOG2E
    lambda_init = 0.8 - 0.6 * math.exp(-0.3 * l)
    lam = (jnp.exp(jnp.sum(lam_q1 * lam_k1)) - jnp.exp(jnp.sum(lam_q2 * lam_k2)) + lambda_init).reshape(1)

    proj = _inproj(x2, norm_g, w_in.astype(BF16))
    gqb = (jnp.tile(qn_b, GROUP_HEADS) * (LOG2E / math.sqrt(HEAD_DIM))).reshape(1, BRANCH_W)
    gkb = jnp.tile(kn_b, GROUP_HEADS).reshape(1, BRANCH_W)
    gqc = (jnp.tile(qn_c, 2 * GROUP_HEADS) * (LOG2E / math.sqrt(DIFF_QK_DIM))).reshape(1, BRANCH_W)
    gkc = jnp.tile(kn_c, 2 * GROUP_HEADS).reshape(1, BRANCH_W)
    bqn, bkn, kcn, qt, vt, q4, k4, v4, q16, k16, v16, k8, q8 = _prep(proj, bsz, s, gqb, gkb, gqc, gkc)

    bound_b = (jnp.max(jnp.abs(qn_b)) * jnp.max(jnp.abs(kn_b)) * (LOG2E * math.sqrt(HEAD_DIM))).reshape(1)
    bound_c = (jnp.max(jnp.abs(qn_c)) * jnp.max(jnp.abs(kn_c)) * (LOG2E * math.sqrt(DIFF_QK_DIM))).reshape(1)
    subln_pair = jnp.tile(subln_g, 2).reshape(1, LANES)
    bs_tab = jnp.repeat(b_s.T, HEAD_DIM, axis=1)
    ws_bf, wout_bf = w_s.astype(BF16), w_out.astype(BF16)

    def fixed_shift_path():
        operands = {1: (bqn.reshape(bsz, 1, s, BRANCH_W), bkn.reshape(bsz, 1, s, BRANCH_W),
                        proj.reshape(bsz, 1, s, PROJ_W), P_BV * HEAD_PAIRS),
                    4: (q4, k4, v4, 0), 16: (q16, k16, v16, 0)}
        dil = []
        for _, d in DIL_PATTERNS:
            qd, kd, vd, v_off = operands[d]
            num, den = _dilated_bounded(_dilated_tables(slopes2, bound_b[0], d), qd, kd, vd, v_off, bsz, s, d)
            if d == 1:
                num, den = num.reshape(bsz * s, BRANCH_W), den.reshape(bsz * s, BRANCH_W)
            dil.append((num, den))
        oc = _diff(True, lam, bound_c, slopes2, q8, k8, vt, subln_pair, bsz, s, lambda_init)
        return _mix_out(True, x2, proj, dil, oc, sgu_g, ws_bf, bs_tab, conv_w, wout_bf, s)

    def running_max_path():
        dil = [_dilated(slopes2, bqn, bkn, proj, bsz, s, d) for (_, d) in DIL_PATTERNS]
        oc = _diff(False, lam, bound_c, slopes2, qt, kcn, vt, subln_pair, bsz, s, lambda_init)
        return _mix_out(False, x2, proj, dil, oc, sgu_g, ws_bf, bs_tab, conv_w, wout_bf, s)

    use_fixed_shift = jnp.maximum(bound_b[0], bound_c[0]) <= MAX_FIXED_SHIFT
    return lax.cond(use_fixed_shift, fixed_shift_path, running_max_path)


def _trunk(x, params):
    bsz, s, _ = x.shape
    x2 = x.reshape(bsz * s, D_MODEL)
    depth = params[0].shape[0]
    for l in range(depth):
        x2 = _layer(x2, bsz, s, l, *[p[l] for p in params])
    return x2.reshape(bsz, s, D_MODEL)


def kernel(x_prompt, x_sample, norm_g, w_in, sgu_g, w_s, b_s, qn_b, kn_b, qn_c, kn_c,
           lam_q1, lam_k1, lam_q2, lam_k2, subln_g, conv_w, w_out):
    params = (norm_g, w_in, sgu_g, w_s, b_s, qn_b, kn_b, qn_c, kn_c,
              lam_q1, lam_k1, lam_q2, lam_k2, subln_g, conv_w, w_out)
    return (_trunk(x_prompt, params), _trunk(x_sample, params))
```

```python
import functools
import math

import jax
import jax.numpy as jnp
from jax import lax
from jax.experimental import pallas as pl
from jax.experimental.pallas import tpu as pltpu

F32 = jnp.float32
BF16 = jnp.bfloat16
F8 = jnp.float8_e4m3fn

D_MODEL = 1024
HEAD_DIM = 64
GROUP_HEADS = 6
BRANCH_W = GROUP_HEADS * HEAD_DIM
N_PIECES = 15
PROJ_W = N_PIECES * BRANCH_W
MIX_W = 4 * BRANCH_W
CHUNK = 128
DIL_PATTERNS = ((128, 1), (512, 4), (2048, 16))
N_SIDE = 64
DIFF_QK_DIM = HEAD_DIM // 2
EPS = 1e-6
NEG = -1e30
LOG2E = math.log2(math.e)

LANES = 128
HEAD_PAIRS = BRANCH_W // LANES
V_ROWS = 80

P_AU, P_AV, P_AG, P_BQ, P_BK, P_BV, P_BG, P_CQ, P_CK, P_CV, P_CG, P_DIN, P_DB, P_DC, P_DG = range(15)

IN_TM = 1024
IN_TN = 1920
PREP_T = 1024
PREP_SUB = 128
MIX_T = 512
MIX_SUB = 512
HALO = 16
DIL_TQ = 128
DIL_WIN = DIL_TQ + 2 * N_SIDE
DIL_ROWS = 4096
DIL_GROUP = 16
DIFF_TQ = 256
DIFF_TQ_FIXED = 512
DIFF_SUBTILES = 2
DIFF_TRIP = 6
DIFF_TK = 256
VMEM_LIMIT = 56 * 1024 * 1024
MAX_FIXED_SHIFT = 40.0


def _cparams(sem):
    return pltpu.CompilerParams(dimension_semantics=sem, vmem_limit_bytes=VMEM_LIMIT)


def _inproj_kernel(x_ref, g_ref, w_ref, o_ref):
    x = x_ref[...]
    ms = jnp.mean(x * x, axis=-1, keepdims=True)
    h = (x * lax.rsqrt(ms + EPS) * g_ref[...]).astype(BF16)
    o_ref[...] = jnp.dot(h, w_ref[...], preferred_element_type=F32).astype(BF16)


def _inproj(x2, norm_g, w_in_bf):
    n = x2.shape[0]
    return pl.pallas_call(
        _inproj_kernel,
        grid=(PROJ_W // IN_TN, n // IN_TM),
        in_specs=[
            pl.BlockSpec((IN_TM, D_MODEL), lambda j, i: (i, 0)),
            pl.BlockSpec((1, D_MODEL), lambda j, i: (0, 0)),
            pl.BlockSpec((D_MODEL, IN_TN), lambda j, i: (0, j)),
        ],
        out_specs=pl.BlockSpec((IN_TM, IN_TN), lambda j, i: (i, j)),
        out_shape=jax.ShapeDtypeStruct((n, PROJ_W), BF16),
        compiler_params=_cparams(("arbitrary", "arbitrary")),
        name="inproj",
    )(x2, norm_g.reshape(1, D_MODEL), w_in_bf)


def _group_rms(x, e_ref, width):
    x2 = x * x
    hi = x2.astype(BF16)
    lo = (x2 - hi.astype(F32)).astype(BF16)
    e = e_ref[...]
    ss = jnp.dot(hi, e, preferred_element_type=F32) + jnp.dot(lo, e, preferred_element_type=F32)
    return x * lax.rsqrt(ss * (1.0 / width) + EPS)


def _prep_kernel(bq_ref, bk_ref, bv_ref, cq_ref, ck_ref, cv_ref, gqb_ref, gkb_ref, gqc_ref, gkc_ref,
                 e64_ref, e32_ref, bqn_ref, bkn_ref, kcn_ref, qt_ref, vt_ref,
                 q4_ref, k4_ref, v4_ref, q16_ref, k16_ref, v16_ref, k8_ref, q8_ref, sq_ref, sk_ref, sv_ref):
    t = bq_ref.shape[0]
    low_lanes = lax.broadcasted_iota(jnp.int32, (PREP_SUB, LANES), 1) < HEAD_DIM
    row = lax.broadcasted_iota(jnp.int32, (V_ROWS - HEAD_DIM, PREP_SUB), 0)
    ones_pad = jnp.where(row == 0, 1.0, 0.0).astype(BF16)
    for cb in range(HEAD_PAIRS):
        sl = slice(cb * LANES, (cb + 1) * LANES)
        for r0 in range(0, t, PREP_SUB):
            rs = slice(r0, r0 + PREP_SUB)
            qb = _group_rms(bq_ref[rs, sl].astype(F32), e64_ref, HEAD_DIM) * gqb_ref[:, sl]
            kb = _group_rms(bk_ref[rs, sl].astype(F32), e64_ref, HEAD_DIM) * gkb_ref[:, sl]
            sq_ref[cb, rs, :] = qb
            sk_ref[cb, rs, :] = kb
            sv_ref[cb, rs, :] = bv_ref[rs, sl].astype(F32)
            bqn_ref[rs, sl] = qb.astype(BF16)
            bkn_ref[rs, sl] = kb.astype(BF16)
            kn = _group_rms(ck_ref[rs, sl].astype(F32), e32_ref, DIFF_QK_DIM) * gkc_ref[:, sl]
            kcn_ref[rs, sl] = kn.astype(BF16)
            k_hi = kn.astype(F8).astype(F32)
            k_lo_swapped = pltpu.roll((kn - k_hi).astype(F8).astype(F32), HEAD_DIM, axis=1)
            zero = jnp.zeros_like(kn)
            k8_ref[0, cb, 0, rs, 0:LANES] = jnp.where(low_lanes, k_hi, k_lo_swapped).astype(F8)
            k8_ref[0, cb, 0, rs, LANES:2 * LANES] = jnp.where(low_lanes, k_hi, zero).astype(F8)
            k8_ref[0, cb, 1, rs, 0:LANES] = jnp.where(low_lanes, k_lo_swapped, k_hi).astype(F8)
            k8_ref[0, cb, 1, rs, LANES:2 * LANES] = jnp.where(low_lanes, zero, k_hi).astype(F8)
            qn_t = (_group_rms(cq_ref[rs, sl].astype(F32), e32_ref, DIFF_QK_DIM) * gqc_ref[:, sl]).T
            qt_ref[0, cb, :, rs] = qn_t.astype(BF16)
            q_hi = qn_t.astype(F8)
            q8_ref[0, cb, 0, :, rs] = q_hi
            q8_ref[0, cb, 1, :, rs] = (qn_t - q_hi.astype(F32)).astype(F8)
            vt = cv_ref[rs, sl].astype(F32).T.astype(BF16)
            for hh in range(2):
                vt_ref[0, cb, hh, 0:HEAD_DIM, rs] = vt[hh * HEAD_DIM:(hh + 1) * HEAD_DIM, :]
                vt_ref[0, cb, hh, HEAD_DIM:V_ROWS, rs] = ones_pad
    for stage, out4, out16 in ((sq_ref, q4_ref, q16_ref), (sk_ref, k4_ref, k16_ref), (sv_ref, v4_ref, v16_ref)):
        for dilation, out in ((4, out4), (16, out16)):
            for r in range(dilation):
                for cb in range(HEAD_PAIRS):
                    out[0, r, :, cb * LANES:(cb + 1) * LANES] = (
                        stage[cb, pl.ds(r, t // dilation, stride=dilation), :].astype(BF16))


def _block_diag_ones(group):
    idx = jnp.arange(LANES) // group
    return (idx[:, None] == idx[None, :]).astype(BF16)


def _prep(proj, bsz, s, gqb, gkb, gqc, gkc):
    n = proj.shape[0]
    tiles_per_seq = s // PREP_T

    def piece(p):
        return pl.BlockSpec((PREP_T, BRANCH_W), lambda i, p=p: (i, p))

    def vec():
        return pl.BlockSpec((1, BRANCH_W), lambda i: (0, 0))

    def tokens():
        return pl.BlockSpec((PREP_T, BRANCH_W), lambda i: (i, 0))

    def residues(d):
        return pl.BlockSpec((1, d, PREP_T // d, BRANCH_W),
                            lambda i: (i // tiles_per_seq, 0, i % tiles_per_seq, 0))

    return pl.pallas_call(
        _prep_kernel,
        grid=(n // PREP_T,),
        in_specs=[piece(P_BQ), piece(P_BK), piece(P_BV), piece(P_CQ), piece(P_CK), piece(P_CV),
                  vec(), vec(), vec(), vec(),
                  pl.BlockSpec((LANES, LANES), lambda i: (0, 0)),
                  pl.BlockSpec((LANES, LANES), lambda i: (0, 0))],
        out_specs=[tokens(), tokens(), tokens(),
                   pl.BlockSpec((1, HEAD_PAIRS, LANES, PREP_T),
                                lambda i: (i // tiles_per_seq, 0, 0, i % tiles_per_seq)),
                   pl.BlockSpec((1, HEAD_PAIRS, 2, V_ROWS, PREP_T),
                                lambda i: (i // tiles_per_seq, 0, 0, 0, i % tiles_per_seq))]
                  + [residues(d) for d in (4, 4, 4, 16, 16, 16)]
                  + [pl.BlockSpec((1, HEAD_PAIRS, 2, PREP_T, 2 * LANES),
                                  lambda i: (i // tiles_per_seq, 0, 0, i % tiles_per_seq, 0)),
                     pl.BlockSpec((1, HEAD_PAIRS, 2, LANES, PREP_T),
                                  lambda i: (i // tiles_per_seq, 0, 0, 0, i % tiles_per_seq))],
        out_shape=[jax.ShapeDtypeStruct((n, BRANCH_W), BF16),
                   jax.ShapeDtypeStruct((n, BRANCH_W), BF16),
                   jax.ShapeDtypeStruct((n, BRANCH_W), BF16),
                   jax.ShapeDtypeStruct((bsz, HEAD_PAIRS, LANES, s), BF16),
                   jax.ShapeDtypeStruct((bsz, HEAD_PAIRS, 2, V_ROWS, s), BF16)]
                  + [jax.ShapeDtypeStruct((bsz, d, s // d, BRANCH_W), BF16) for d in (4, 4, 4, 16, 16, 16)]
                  + [jax.ShapeDtypeStruct((bsz, HEAD_PAIRS, 2, s, 2 * LANES), F8),
                     jax.ShapeDtypeStruct((bsz, HEAD_PAIRS, 2, LANES, s), F8)],
        scratch_shapes=[pltpu.VMEM((HEAD_PAIRS, PREP_T, LANES), F32)] * 3,
        compiler_params=_cparams(("arbitrary",)),
        name="prep",
    )(proj, proj, proj, proj, proj, proj, gqb, gkb, gqc, gkc,
      _block_diag_ones(HEAD_DIM), _block_diag_ones(DIFF_QK_DIM))


def _dilated_kernel(slope_ref, q_ref, k_ref, v_ref, o_ref, lse_ref, *, dilation, seq_sub):
    hp = pl.program_id(2)
    qi = pl.program_id(3)
    tqb = q_ref.shape[1]
    lane = lax.broadcasted_iota(jnp.int32, (DIL_TQ, LANES), 1)
    first_head = lane < HEAD_DIM
    rel0 = (lax.broadcasted_iota(jnp.int32, (DIL_TQ, DIL_WIN), 1)
            - lax.broadcasted_iota(jnp.int32, (DIL_TQ, DIL_WIN), 0))

    def sub_block(sb, carry):
        r0 = pl.multiple_of(sb * DIL_TQ, DIL_TQ)
        i0 = qi * tqb + r0
        ws = pl.multiple_of(jnp.clip(i0 - N_SIDE, 0, seq_sub - DIL_WIN), N_SIDE)
        kw = k_ref[0, pl.ds(ws, DIL_WIN), :]
        vw = v_ref[0, pl.ds(ws, DIL_WIN), :]
        q = q_ref[0, pl.ds(r0, DIL_TQ), :]
        dist = jnp.abs(rel0 + (ws - i0))
        valid = dist <= N_SIDE
        distf = dist.astype(F32)
        outs, lses = [], []
        for hh in range(2):
            slope = slope_ref[2 * hp + hh] * float(dilation)
            qm = jnp.where(first_head if hh == 0 else jnp.logical_not(first_head), q, jnp.zeros_like(q))
            sc = lax.dot_general(qm, kw, (((1,), (1,)), ((), ())), preferred_element_type=F32)
            sc = jnp.where(valid, sc - slope * distf, NEG)
            m = jnp.max(sc, axis=-1, keepdims=True)
            p = jnp.exp2(sc - m)
            den = jnp.sum(p, axis=-1, keepdims=True)
            o = jnp.dot(p.astype(BF16), vw, preferred_element_type=F32) / den
            outs.append(o)
            lses.append(jnp.broadcast_to(m + jnp.log2(den), (DIL_TQ, LANES)))
        o_ref[0, pl.ds(r0, DIL_TQ), :] = jnp.where(first_head, outs[0], outs[1]).astype(o_ref.dtype)
        lse_ref[0, pl.ds(r0, DIL_TQ), :] = jnp.where(first_head, lses[0], lses[1])
        return carry

    lax.fori_loop(0, tqb // DIL_TQ, sub_block, 0)


def _dilated(slopes2, bqn, bkn, proj, bsz, s, dilation):
    seq_sub = s // dilation
    tqb = min(seq_sub, 1024)
    q3 = bqn.reshape(bsz, seq_sub, dilation * BRANCH_W)
    k3 = bkn.reshape(bsz, seq_sub, dilation * BRANCH_W)
    v3 = proj.reshape(bsz, seq_sub, dilation * PROJ_W)
    blocks_q = BRANCH_W // LANES
    blocks_p = PROJ_W // LANES
    v_off = P_BV * blocks_q

    def tile():
        return pl.BlockSpec((1, tqb, LANES), lambda b, r, hp, qi: (b, qi, r * blocks_q + hp))

    o, lse = pl.pallas_call(
        functools.partial(_dilated_kernel, dilation=dilation, seq_sub=seq_sub),
        grid=(bsz, dilation, HEAD_PAIRS, seq_sub // tqb),
        in_specs=[
            pl.BlockSpec(memory_space=pltpu.SMEM),
            tile(),
            pl.BlockSpec((1, seq_sub, LANES), lambda b, r, hp, qi: (b, 0, r * blocks_q + hp)),
            pl.BlockSpec((1, seq_sub, LANES), lambda b, r, hp, qi: (b, 0, r * blocks_p + v_off + hp)),
        ],
        out_specs=[tile(), tile()],
        out_shape=[jax.ShapeDtypeStruct(q3.shape, BF16), jax.ShapeDtypeStruct(q3.shape, F32)],
        compiler_params=_cparams(("arbitrary",) * 4),
        name=f"dilated{dilation}",
    )(slopes2, q3, k3, v3)
    return o.reshape(bsz * s, BRANCH_W), lse.reshape(bsz * s, BRANCH_W)


def _dilated_bounded_kernel(tab_ref, q_ref, k_ref, v_ref, num_ref, den_ref, p_ref, *, seq_sub):
    qi = pl.program_id(3)
    n_res, tqb = q_ref.shape[1], q_ref.shape[2]
    sub_per_res = tqb // DIL_TQ
    first_head_q = lax.broadcasted_iota(jnp.int32, (DIL_TQ, LANES), 1) < HEAD_DIM
    v_lane = lax.broadcasted_iota(jnp.int32, (DIL_WIN, 2 * LANES), 1)
    value_lanes = (v_lane < HEAD_DIM) | (v_lane >= 2 * LANES - HEAD_DIM)

    def window(sb):
        res = sb // sub_per_res
        r0 = pl.multiple_of((sb % sub_per_res) * DIL_TQ, DIL_TQ)
        i0 = qi * tqb + r0
        ws = pl.multiple_of(jnp.clip(i0 - N_SIDE, 0, seq_sub - DIL_WIN), N_SIDE)
        variant = jnp.where(i0 == 0, 0, jnp.where(i0 == seq_sub - DIL_TQ, 2, 1))
        return res, r0, ws, variant

    def probabilities(u):
        for half in range(DIL_GROUP):
            res, r0, ws, variant = window(DIL_GROUP * u + half)
            kw = k_ref[0, res, pl.ds(ws, DIL_WIN), :]
            q = q_ref[0, res, pl.ds(r0, DIL_TQ), :]
            zero = jnp.zeros_like(q)
            q2 = jnp.concatenate([jnp.where(first_head_q, q, zero), jnp.where(first_head_q, zero, q)], axis=0)
            sc = lax.dot_general(q2, kw, (((1,), (1,)), ((), ())), preferred_element_type=F32)
            for hh in range(2):
                p_ref[half, hh * DIL_TQ:(hh + 1) * DIL_TQ, :] = jnp.exp2(
                    sc[hh * DIL_TQ:(hh + 1) * DIL_TQ] + tab_ref[hh, variant]).astype(BF16)

    def outputs(u):
        for half in range(DIL_GROUP):
            res, r0, ws, _ = window(DIL_GROUP * u + half)
            vw = v_ref[0, res, pl.ds(ws, DIL_WIN), :]
            vw2 = jnp.concatenate([vw, vw], axis=1)
            vcat = jnp.where(value_lanes, vw2, jnp.ones_like(vw2))
            a = jnp.dot(p_ref[half], vcat, preferred_element_type=F32)
            a0 = a[0:DIL_TQ, 0:LANES]
            a1 = a[DIL_TQ:2 * DIL_TQ, LANES:2 * LANES]
            num_ref[0, res, pl.ds(r0, DIL_TQ), :] = jnp.where(first_head_q, a0, a1).astype(num_ref.dtype)
            den_ref[0, res, pl.ds(r0, DIL_TQ), :] = pltpu.roll(jnp.where(first_head_q, a1, a0), HEAD_DIM, axis=1)

    n_groups = n_res * sub_per_res // DIL_GROUP
    probabilities(0)

    def step(u, carry):
        outputs(u - 1)
        probabilities(u)
        return carry

    lax.fori_loop(1, n_groups, step, 0)
    outputs(n_groups - 1)


def _dilated_tables(slopes2, bound, dilation):
    rel = ((jnp.arange(DIL_WIN)[None, :] - jnp.arange(DIL_TQ)[:, None])[None]
           - jnp.array([0, N_SIDE, 2 * N_SIDE])[:, None, None])
    dist = jnp.abs(rel).astype(F32)
    bias = -(slopes2 * float(dilation))[:, None, None, None] * dist[None] - bound
    return jnp.where(dist[None] <= N_SIDE, bias, NEG).astype(F32)


def _dilated_bounded(tables, q4d, k4d, v4d, v_block_offset, bsz, s, dilation):
    seq_sub = s // dilation
    tqb = min(seq_sub, DIL_ROWS)
    n_res = min(dilation, DIL_ROWS // tqb)

    def tile():
        return pl.BlockSpec((1, n_res, tqb, LANES), lambda b, r, hp, qi: (b, r, qi, hp))

    out_shape = (bsz, dilation, seq_sub, BRANCH_W)
    return pl.pallas_call(
        functools.partial(_dilated_bounded_kernel, seq_sub=seq_sub),
        grid=(bsz, dilation // n_res, HEAD_PAIRS, seq_sub // tqb),
        in_specs=[
            pl.BlockSpec((2, 3, DIL_TQ, DIL_WIN), lambda b, r, hp, qi: (hp, 0, 0, 0)),
            tile(),
            pl.BlockSpec((1, n_res, seq_sub, LANES), lambda b, r, hp, qi: (b, r, 0, hp)),
            pl.BlockSpec((1, n_res, seq_sub, LANES), lambda b, r, hp, qi: (b, r, 0, v_block_offset + hp)),
        ],
        out_specs=[tile(), tile()],
        out_shape=[jax.ShapeDtypeStruct(out_shape, BF16), jax.ShapeDtypeStruct(out_shape, F32)],
        scratch_shapes=[pltpu.VMEM((DIL_GROUP, 2 * DIL_TQ, DIL_WIN), BF16)],
        compiler_params=_cparams(("arbitrary",) * 4),
        name=f"dilated_bounded{dilation}",
    )(tables, q4d, k4d, v4d)


def _diff_kernel(lam_ref, slope_ref, qt_ref, k_ref, vt_ref, g_ref, o_ref, qm_ref, tm_ref, acc_ref,
                 *, out_scale):
    hp = pl.program_id(1)
    qi = pl.program_id(2)
    tq = qt_ref.shape[3]
    tk = DIFF_TK
    nk = k_ref.shape[1] // tk
    i0 = qi * tq

    qt = qt_ref[0, 0]
    row_group = lax.broadcasted_iota(jnp.int32, (LANES, tq), 0) // DIFF_QK_DIM
    for g in range(4):
        qm_ref[g] = jnp.where(row_group == g, qt, jnp.zeros_like(qt))
    dkq = (lax.broadcasted_iota(jnp.int32, (tk, tq), 0)
           - lax.broadcasted_iota(jnp.int32, (tk, tq), 1)).astype(F32)
    for hh in range(2):
        tm_ref[hh] = slope_ref[2 * hp + hh] * dkq
    acc_ref[...] = jnp.zeros_like(acc_ref)

    def key_tile(j, ms, mode):
        koff = pl.multiple_of(j * tk, tk)
        kt = k_ref[0, pl.ds(koff, tk), :]
        tile_dist = jnp.abs(i0 - koff).astype(F32)
        new_ms = []
        for hh in range(2):
            vt = vt_ref[0, 0, hh, :, pl.ds(koff, tk)]
            shift = slope_ref[2 * hp + hh] * tile_dist
            for c in range(2):
                g = 2 * hh + c
                sc = jnp.dot(kt, qm_ref[g], preferred_element_type=F32)
                if mode < 0:
                    sc = sc + tm_ref[hh]
                elif mode > 0:
                    sc = sc - tm_ref[hh]
                else:
                    sc = sc - jnp.abs(tm_ref[hh])
                m_old = ms[g]
                m_new = jnp.maximum(m_old, jnp.max(sc, axis=0, keepdims=True) - shift)
                p = jnp.exp2(sc - (m_new + shift))
                alpha = jnp.exp2(m_old - m_new)
                acc_ref[g] = alpha * acc_ref[g] + jnp.dot(vt, p.astype(BF16), preferred_element_type=F32)
                new_ms.append(m_new)
        return tuple(new_ms)

    ms = tuple(jnp.full((1, tq), NEG, F32) for _ in range(4))
    ms = lax.fori_loop(0, qi, lambda j, m: key_tile(j, m, -1), ms)
    ms = key_tile(qi, ms, 0)
    lax.fori_loop(qi + 1, nk, lambda j, m: key_tile(j, m, 1), ms)
    _diff_finish(lam_ref, g_ref, o_ref, acc_ref, out_scale)


def _diff_bounded_kernel(lam_ref, slope_ref, qt_ref, k_ref, vt_ref, g_ref, o_ref, qm_ref, tm_ref, acc_ref,
                         p_ref, *, out_scale):
    hp = pl.program_id(1)
    qi = pl.program_id(2)
    n_sub = DIFF_SUBTILES
    tq = qt_ref.shape[4] // n_sub
    tk = DIFF_TK
    tiles_per_q = tq // tk
    nk = k_ref.shape[3] // tk
    centre = 0.5 * lam_ref[1]

    zeros = jnp.zeros((DIFF_QK_DIM, tq), qt_ref.dtype)
    for t in range(n_sub):
        q_hi = qt_ref[0, 0, 0, :, t * tq:(t + 1) * tq]
        q_lo = qt_ref[0, 0, 1, :, t * tq:(t + 1) * tq]
        for g in range(4):
            hh, c = divmod(g, 2)
            hi_block, lo_block, hi2_block = ((0, 2, 4), (2, 0, 6))[hh]
            rows = slice(g * DIFF_QK_DIM, (g + 1) * DIFF_QK_DIM)
            blocks = [zeros] * (2 * LANES // DIFF_QK_DIM)
            blocks[hi_block + c], blocks[lo_block + c], blocks[hi2_block + c] = q_hi[rows], q_hi[rows], q_lo[rows]
            qm_ref[t, g] = jnp.concatenate(blocks, axis=0)
    @pl.when(qi == 0)
    def _():
        dkq = (lax.broadcasted_iota(jnp.int32, (tk, tq), 0)
               - lax.broadcasted_iota(jnp.int32, (tk, tq), 1)).astype(F32)
        for hh in range(2):
            slope = slope_ref[2 * hp + hh]
            tm = slope * dkq
            tm_ref[hh, 0] = tm
            tm_ref[hh, 1] = -tm
            for r in range(tiles_per_q):
                origin = float(r * tk)
                tm_ref[hh, 2 + r] = slope * (origin - jnp.abs(dkq + origin))

    acc_ref[...] = jnp.zeros_like(acc_ref)

    def stage(scores=None, values=None):
        for hh in range(2):
            slope = slope_ref[2 * hp + hh]
            for c in range(2):
                g = 2 * hh + c
                parts = []
                for half in range(2):
                    if values is not None:
                        t, u, slot = values
                        koff = pl.multiple_of((2 * u + half) * tk, tk)
                        parts.append(jnp.dot(vt_ref[0, 0, hh, :, pl.ds(koff, tk)], p_ref[slot, half, g],
                                             preferred_element_type=F32))
                    if scores is not None:
                        t, u, slot = scores
                        j = 2 * u + half
                        koff = pl.multiple_of(j * tk, tk)
                        q_tile = qi * n_sub + t
                        shift = -(slope * jnp.abs(q_tile * tq - koff).astype(F32) + centre)
                        first = tiles_per_q * q_tile
                        table = jnp.where(j < first, 0, jnp.where(j >= first + tiles_per_q, 1, 2 + j - first))
                        sc = jnp.dot(k_ref[0, 0, hh, pl.ds(koff, tk), :], qm_ref[t, g],
                                     preferred_element_type=F32)
                        p_ref[slot, half, g] = jnp.exp2((sc + tm_ref[hh, table] + shift).astype(BF16))
                if parts:
                    acc_ref[values[0], g] += parts[0] + parts[1]

    n_pairs = nk // 2
    stage(scores=(0, 0, 0))
    for t in range(n_sub):
        def steps(first, count, t=t):
            for k in range(count):
                stage(scores=(t, first + k, (1 + k) % 2), values=(t, first + k - 1, k % 2))

        trip_steps = DIFF_TRIP if (n_pairs - 2) // DIFF_TRIP >= 2 else 2

        def trip(w, carry):
            steps(trip_steps * w + 1, trip_steps)
            return carry

        n_trips = (n_pairs - 2) // trip_steps
        lax.fori_loop(0, n_trips, trip, 0)
        steps(trip_steps * n_trips + 1, n_pairs - 2 - trip_steps * n_trips)
        stage(scores=(t, n_pairs - 1, 1), values=(t, n_pairs - 2, 0))
        stage(scores=(t + 1, 0, 0) if t + 1 < n_sub else None, values=(t, n_pairs - 1, 1))
        _diff_finish(lam_ref, g_ref, o_ref, acc_ref.at[t], out_scale, slice(t * tq, (t + 1) * tq))


def _diff_finish(lam_ref, g_ref, o_ref, acc_ref, out_scale, out_rows=slice(None)):
    lam = lam_ref[0]
    heads = []
    for hh in range(2):
        a0 = acc_ref[2 * hh]
        a1 = acc_ref[2 * hh + 1]
        o = (a0[0:HEAD_DIM] / a0[HEAD_DIM:HEAD_DIM + 1]
             - lam * (a1[0:HEAD_DIM] / a1[HEAD_DIM:HEAD_DIM + 1]))
        ms2 = jnp.mean(o * o, axis=0, keepdims=True)
        heads.append(o * lax.rsqrt(ms2 + EPS))
    out = jnp.concatenate(heads, axis=0).T
    o_ref[0, out_rows, :] = (out * (g_ref[...] * out_scale)).astype(o_ref.dtype)


def _diff(bounded, lam, score_bound, slopes2, q_t, k, vt, subln_pair, bsz, s, lambda_init):
    scalars = jnp.concatenate([lam, score_bound]).astype(F32)
    if bounded:
        body, name = _diff_bounded_kernel, "diff_bounded"
        sub = DIFF_TQ_FIXED
        tq = DIFF_SUBTILES * sub
        q_spec = pl.BlockSpec((1, 1, 2, LANES, tq), lambda b, hp, qi: (b, hp, 0, 0, qi))
        k_spec = pl.BlockSpec((1, 1, 2, s, 2 * LANES), lambda b, hp, qi: (b, hp, 0, 0, 0))
        scratch = [pltpu.VMEM((DIFF_SUBTILES, 4, 2 * LANES, sub), F8),
                   pltpu.VMEM((2, 2 + sub // DIFF_TK, DIFF_TK, sub), F32),
                   pltpu.VMEM((DIFF_SUBTILES, 4, V_ROWS, sub), F32),
                   pltpu.VMEM((2, 2, 4, DIFF_TK, sub), BF16)]
    else:
        body, name, tq = _diff_kernel, "diff_online", DIFF_TQ
        k = k.reshape(bsz, s, BRANCH_W)
        q_spec = pl.BlockSpec((1, 1, LANES, tq), lambda b, hp, qi: (b, hp, 0, qi))
        k_spec = pl.BlockSpec((1, s, LANES), lambda b, hp, qi: (b, 0, hp))
        scratch = [pltpu.VMEM((4, LANES, tq), BF16), pltpu.VMEM((2, DIFF_TK, tq), F32),
                   pltpu.VMEM((4, V_ROWS, tq), F32)]
    return pl.pallas_call(
        functools.partial(body, out_scale=1.0 - lambda_init),
        grid=(bsz, HEAD_PAIRS, s // tq),
        in_specs=[
            pl.BlockSpec(memory_space=pltpu.SMEM),
            pl.BlockSpec(memory_space=pltpu.SMEM),
            q_spec,
            k_spec,
            pl.BlockSpec((1, 1, 2, V_ROWS, s), lambda b, hp, qi: (b, hp, 0, 0, 0)),
            pl.BlockSpec((1, LANES), lambda b, hp, qi: (0, 0)),
        ],
        out_specs=pl.BlockSpec((1, tq, LANES), lambda b, hp, qi: (b, qi, hp)),
        out_shape=jax.ShapeDtypeStruct((bsz, s, BRANCH_W), BF16),
        scratch_shapes=scratch,
        compiler_params=_cparams(("arbitrary",) * 3),
        name=name,
    )(scalars, slopes2, q_t, k, vt, subln_pair).reshape(bsz * s, BRANCH_W)


def _silu(x):
    half = 0.5 * x
    return half + half * jnp.tanh(half)


def _mix_kernel(x_ref, au_ref, av_ref, ag_ref, bg_ref, cg_ref, din_ref, db_ref, dc_ref, dg_ref,
                din_p_ref, dc_p_ref, din_n_ref, dc_n_ref,
                o1_ref, o4_ref, o16_ref, l1_ref, l4_ref, l16_ref, oc_ref,
                sgu_ref, ws_ref, bs_ref, cw_ref, wout_ref, y_ref, *stage_refs, tiles_per_seq, bounded):
    i = pl.program_id(0)
    t = x_ref.shape[0]
    first_head = lax.broadcasted_iota(jnp.int32, (CHUNK, LANES), 1) < HEAD_DIM

    if bounded:
        for stage, blk, dilation in zip(stage_refs, (o4_ref, o16_ref, l4_ref, l16_ref), (4, 16, 4, 16)):
            for r in range(dilation):
                for cb in range(HEAD_PAIRS):
                    stage[cb, pl.ds(r, t // dilation, stride=dilation), :] = (
                        blk[0, r, :, cb * LANES:(cb + 1) * LANES].astype(F32))

    z = dc_ref[...].astype(F32) * din_ref[...].astype(F32)
    seq_pos = i % tiles_per_seq
    z_prev = (dc_p_ref[...].astype(F32) * din_p_ref[...].astype(F32))[HALO - 1:HALO, :]
    z_next = (dc_n_ref[...].astype(F32) * din_n_ref[...].astype(F32))[0:1, :]
    z_prev = jnp.where(seq_pos == 0, 0.0, z_prev)
    z_next = jnp.where(seq_pos == tiles_per_seq - 1, 0.0, z_next)
    row = lax.broadcasted_iota(jnp.int32, z.shape, 0)
    z_up = jnp.where(row == 0, z_prev, pltpu.roll(z, 1, axis=0))
    z_dn = jnp.where(row == t - 1, z_next, pltpu.roll(z, t - 1, axis=0))
    cw = cw_ref[...]
    out_d = db_ref[...].astype(F32) * (cw[0:1] * z_up + cw[1:2] * z + cw[2:3] * z_dn)

    for sb in range(t // MIX_SUB):
        rows = slice(sb * MIX_SUB, (sb + 1) * MIX_SUB)

        av = av_ref[rows, :].astype(F32)
        vn = (av * lax.rsqrt(jnp.mean(av * av, axis=-1, keepdims=True) + EPS) * sgu_ref[...]).astype(BF16)
        chunks = []
        for c in range(MIX_SUB // CHUNK):
            vc = vn[c * CHUNK:(c + 1) * CHUNK, :]
            pairs = []
            for hp in range(HEAD_PAIRS):
                vp = vc[:, hp * LANES:(hp + 1) * LANES]
                r0 = jnp.dot(ws_ref[2 * hp], vp, preferred_element_type=F32)
                r1 = jnp.dot(ws_ref[2 * hp + 1], vp, preferred_element_type=F32)
                pairs.append(jnp.where(first_head, r0, r1))
            chunks.append(jnp.concatenate(pairs, axis=1) + bs_ref[...])
        out_a = au_ref[rows, :].astype(F32) * jnp.concatenate(chunks, axis=0)

        if bounded:
            def in_order(stage):
                return jnp.concatenate([stage[cb, rows, :] for cb in range(HEAD_PAIRS)], axis=1)

            num = o1_ref[rows, :].astype(F32) + in_order(stage_refs[0]) + in_order(stage_refs[1])
            den = l1_ref[rows, :] + in_order(stage_refs[2]) + in_order(stage_refs[3])
            out_b = num / den
        else:
            l1, l4, l16 = l1_ref[rows, :], l4_ref[rows, :], l16_ref[rows, :]
            lm = jnp.maximum(jnp.maximum(l1, l4), l16)
            w1, w4, w16 = jnp.exp2(l1 - lm), jnp.exp2(l4 - lm), jnp.exp2(l16 - lm)
            out_b = ((w1 * o1_ref[rows, :].astype(F32) + w4 * o4_ref[rows, :].astype(F32)
                      + w16 * o16_ref[rows, :].astype(F32)) / (w1 + w4 + w16))

        mixed = jnp.concatenate([
            (_silu(ag_ref[rows, :].astype(F32)) * out_a).astype(BF16),
            (_silu(bg_ref[rows, :].astype(F32)) * out_b).astype(BF16),
            (_silu(cg_ref[rows, :].astype(F32)) * oc_ref[rows, :].astype(F32)).astype(BF16),
            (_silu(dg_ref[rows, :].astype(F32)) * out_d[rows]).astype(BF16)], axis=1)
        y_ref[rows, :] = x_ref[rows, :] + jnp.dot(mixed, wout_ref[...], preferred_element_type=F32)


def _mix_out(bounded, x2, proj, dil, oc, sgu_g, ws_bf, bs_tab, conv_w, wout_bf, s):
    n = x2.shape[0]
    tiles_per_seq = s // MIX_T
    halo_per_tile = MIX_T // HALO
    n_halo = n // HALO

    def piece(p):
        return pl.BlockSpec((MIX_T, BRANCH_W), lambda i, p=p: (i, p))

    def halo_prev(p):
        return pl.BlockSpec((HALO, BRANCH_W), lambda i, p=p: (jnp.maximum(i * halo_per_tile - 1, 0), p))

    def halo_next(p):
        return pl.BlockSpec((HALO, BRANCH_W),
                            lambda i, p=p: (jnp.minimum((i + 1) * halo_per_tile, n_halo - 1), p))

    def tokens():
        return pl.BlockSpec((MIX_T, BRANCH_W), lambda i: (i, 0))

    def const(shape):
        return pl.BlockSpec(shape, lambda i: (0,) * len(shape))

    def residues(d):
        return pl.BlockSpec((1, d, MIX_T // d, BRANCH_W),
                            lambda i: (i // tiles_per_seq, 0, i % tiles_per_seq, 0))

    (o1, l1), (o4, l4), (o16, l16) = dil
    if bounded:
        dil_specs = [tokens(), residues(4), residues(16)] * 2
        scratch = [pltpu.VMEM((HEAD_PAIRS, MIX_T, LANES), F32)] * 4
    else:
        dil_specs = [tokens()] * 6
        scratch = []
    return pl.pallas_call(
        functools.partial(_mix_kernel, tiles_per_seq=tiles_per_seq, bounded=bounded),
        grid=(n // MIX_T,),
        in_specs=[pl.BlockSpec((MIX_T, D_MODEL), lambda i: (i, 0)),
                  piece(P_AU), piece(P_AV), piece(P_AG), piece(P_BG), piece(P_CG),
                  piece(P_DIN), piece(P_DB), piece(P_DC), piece(P_DG),
                  halo_prev(P_DIN), halo_prev(P_DC), halo_next(P_DIN), halo_next(P_DC),
                  *dil_specs, tokens(),
                  const((1, BRANCH_W)), const((GROUP_HEADS, CHUNK, CHUNK)), const((CHUNK, BRANCH_W)),
                  const((3, BRANCH_W)), const((MIX_W, D_MODEL))],
        out_specs=pl.BlockSpec((MIX_T, D_MODEL), lambda i: (i, 0)),
        out_shape=jax.ShapeDtypeStruct((n, D_MODEL), F32),
        scratch_shapes=scratch,
        compiler_params=_cparams(("arbitrary",)),
        name="mixout_bounded" if bounded else "mixout",
    )(x2, *([proj] * 13), o1, o4, o16, l1, l4, l16, oc,
      sgu_g.reshape(1, BRANCH_W), ws_bf, bs_tab, conv_w, wout_bf)


def _layer(x2, bsz, s, l, norm_g, w_in, sgu_g, w_s, b_s, qn_b, kn_b, qn_c, kn_c,
           lam_q1, lam_k1, lam_q2, lam_k2, subln_g, conv_w, w_out):
    slopes2 = (2.0 ** (-8.0 * jnp.arange(1, GROUP_HEADS + 1, dtype=F32) / GROUP_HEADS)) * LOG2E
    lambda_init = 0.8 - 0.6 * math.exp(-0.3 * l)
    lam = (jnp.exp(jnp.sum(lam_q1 * lam_k1)) - jnp.exp(jnp.sum(lam_q2 * lam_k2)) + lambda_init).reshape(1)

    proj = _inproj(x2, norm_g, w_in.astype(BF16))
    gqb = (jnp.tile(qn_b, GROUP_HEADS) * (LOG2E / math.sqrt(HEAD_DIM))).reshape(1, BRANCH_W)
    gkb = jnp.tile(kn_b, GROUP_HEADS).reshape(1, BRANCH_W)
    gqc = (jnp.tile(qn_c, 2 * GROUP_HEADS) * (LOG2E / math.sqrt(DIFF_QK_DIM))).reshape(1, BRANCH_W)
    gkc = jnp.tile(kn_c, 2 * GROUP_HEADS).reshape(1, BRANCH_W)
    bqn, bkn, kcn, qt, vt, q4, k4, v4, q16, k16, v16, k8, q8 = _prep(proj, bsz, s, gqb, gkb, gqc, gkc)

    bound_b = (jnp.max(jnp.abs(qn_b)) * jnp.max(jnp.abs(kn_b)) * (LOG2E * math.sqrt(HEAD_DIM))).reshape(1)
    bound_c = (jnp.max(jnp.abs(qn_c)) * jnp.max(jnp.abs(kn_c)) * (LOG2E * math.sqrt(DIFF_QK_DIM))).reshape(1)
    subln_pair = jnp.tile(subln_g, 2).reshape(1, LANES)
    bs_tab = jnp.repeat(b_s.T, HEAD_DIM, axis=1)
    ws_bf, wout_bf = w_s.astype(BF16), w_out.astype(BF16)

    def fixed_shift_path():
        operands = {1: (bqn.reshape(bsz, 1, s, BRANCH_W), bkn.reshape(bsz, 1, s, BRANCH_W),
                        proj.reshape(bsz, 1, s, PROJ_W), P_BV * HEAD_PAIRS),
                    4: (q4, k4, v4, 0), 16: (q16, k16, v16, 0)}
        dil = []
        for _, d in DIL_PATTERNS:
            qd, kd, vd, v_off = operands[d]
            num, den = _dilated_bounded(_dilated_tables(slopes2, bound_b[0], d), qd, kd, vd, v_off, bsz, s, d)
            if d == 1:
                num, den = num.reshape(bsz * s, BRANCH_W), den.reshape(bsz * s, BRANCH_W)
            dil.append((num, den))
        oc = _diff(True, lam, bound_c, slopes2, q8, k8, vt, subln_pair, bsz, s, lambda_init)
        return _mix_out(True, x2, proj, dil, oc, sgu_g, ws_bf, bs_tab, conv_w, wout_bf, s)

    def running_max_path():
        dil = [_dilated(slopes2, bqn, bkn, proj, bsz, s, d) for (_, d) in DIL_PATTERNS]
        oc = _diff(False, lam, bound_c, slopes2, qt, kcn, vt, subln_pair, bsz, s, lambda_init)
        return _mix_out(False, x2, proj, dil, oc, sgu_g, ws_bf, bs_tab, conv_w, wout_bf, s)

    use_fixed_shift = jnp.maximum(bound_b[0], bound_c[0]) <= MAX_FIXED_SHIFT
    return lax.cond(use_fixed_shift, fixed_shift_path, running_max_path)


def _trunk(x, params):
    bsz, s, _ = x.shape
    x2 = x.reshape(bsz * s, D_MODEL)
    depth = params[0].shape[0]
    for l in range(depth):
        x2 = _layer(x2, bsz, s, l, *[p[l] for p in params])
    return x2.reshape(bsz, s, D_MODEL)


def kernel(x_prompt, x_sample, norm_g, w_in, sgu_g, w_s, b_s, qn_b, kn_b, qn_c, kn_c,
           lam_q1, lam_k1, lam_q2, lam_k2, subln_g, conv_w, w_out):
    params = (norm_g, w_in, sgu_g, w_s, b_s, qn_b, kn_b, qn_c, kn_c,
              lam_q1, lam_k1, lam_q2, lam_k2, subln_g, conv_w, w_out)
    return (_trunk(x_prompt, params), _trunk(x_sample, params))
```
